```python
import jax, jax.numpy as jnp
from jax import lax
import numpy as np

D_MODEL = 1024
BATCH = 8
SEQ = 2048
DEPTH = 1

MIX_WIDTH = D_MODEL
POOL_WIDTH = MIX_WIDTH // 2
POOL_WINDOWS = (2, 4, 8, 16)
POOL_GROUP = POOL_WIDTH // 4
HEAD_DIM = 64
N_Q_HEADS = (MIX_WIDTH - POOL_WIDTH) // HEAD_DIM
N_KV_HEADS = 2
Q_PER_KV = N_Q_HEADS // N_KV_HEADS
IDX_HEADS = 8
IDX_DIM = 32
TOPK_MAX = 256
Q_BLOCK = 128
ROPE_THETA = 10000.0
D_FF = 2816
CONV_WIDTH = 3
EPS = 1e-6
NEG = -1e30

IN_SPLITS = (POOL_WIDTH, N_Q_HEADS * HEAD_DIM, N_KV_HEADS * HEAD_DIM, N_KV_HEADS * HEAD_DIM,
             IDX_HEADS * IDX_DIM, IDX_DIM, IDX_HEADS)
IN_WIDTH = POOL_WIDTH + N_Q_HEADS * HEAD_DIM + 2 * N_KV_HEADS * HEAD_DIM + IDX_HEADS * IDX_DIM + IDX_DIM + IDX_HEADS

kernel_name = "hybrid_pool_dsa_convffn"


def rms_norm(x, g):
    xf = x.astype(jnp.float32)
    y = xf * lax.rsqrt(jnp.mean(xf * xf, axis=-1, keepdims=True) + EPS)
    return (y * g.astype(jnp.float32)).astype(x.dtype)


def rope(x, pos):
    d = x.shape[-1]
    half = d // 2
    inv = jnp.exp(-jnp.log(jnp.float32(ROPE_THETA)) * jnp.arange(half, dtype=jnp.float32) / half)
    ang = pos.astype(jnp.float32)[:, None] * inv[None, :]
    cos = jnp.cos(ang)[:, None, :]
    sin = jnp.sin(ang)[:, None, :]
    xf = x.astype(jnp.float32)
    x1, x2 = xf[..., :half], xf[..., half:]
    out = jnp.concatenate([x1 * cos - x2 * sin, x2 * cos + x1 * sin], axis=-1)
    return out.astype(x.dtype)


def pool_mixer(v, pool_w, pool_scale):
    B, T, C = v.shape
    vf = v.astype(jnp.float32)
    csum = jnp.cumsum(vf, axis=1)
    t = jnp.arange(T)
    outs = []
    for g, w in enumerate(POOL_WINDOWS):
        sl = slice(g * POOL_GROUP, (g + 1) * POOL_GROUP)
        cg = csum[..., sl]
        lag = jnp.pad(cg, ((0, 0), (w, 0), (0, 0)))[:, :T]
        cnt = jnp.minimum(t + 1, w).astype(jnp.float32)[None, :, None]
        outs.append((cg - lag) / cnt - vf[..., sl])
    p = jnp.stack(outs, axis=2)
    y = jnp.einsum('btgc,gcd->btgd', p, pool_w.astype(jnp.float32)).reshape(B, T, C)
    return (y * pool_scale.astype(jnp.float32)).astype(v.dtype)


def dsa_mixer(q, k, v, qi, ki, wi):
    B, T = q.shape[0], q.shape[1]
    topk = min(TOPK_MAX, T // 4)
    nb = T // Q_BLOCK
    kpos = jnp.arange(T)
    b_idx = jnp.arange(B)[:, None, None]
    kif = ki.astype(jnp.float32)
    scale = HEAD_DIM ** -0.5

    def to_blocks(a):
        return a.reshape((a.shape[0], nb, Q_BLOCK) + a.shape[2:]).swapaxes(0, 1)

    def block(args):
        qb, qib, wib, t0 = args
        qpos = t0 + jnp.arange(Q_BLOCK)
        causal = kpos[None, :] <= qpos[:, None]
        rel = jax.nn.relu(jnp.einsum('bqhd,bsd->bqsh', qib.astype(jnp.float32), kif))
        score = jnp.einsum('bqsh,bqh->bqs', rel, wib.astype(jnp.float32))
        score = jnp.where(causal[None], score, NEG)
        _, idx = lax.top_k(score, topk)
        kg = k[b_idx, idx].astype(jnp.float32)
        vg = v[b_idx, idx].astype(jnp.float32)
        qg = qb.reshape(B, Q_BLOCK, N_KV_HEADS, Q_PER_KV, HEAD_DIM).astype(jnp.float32)
        logits = jnp.einsum('bqjgd,bqnjd->bqjgn', qg, kg) * scale
        valid = (idx <= qpos[None, :, None])[:, :, None, None, :]
        probs = jax.nn.softmax(jnp.where(valid, logits, NEG), axis=-1)
        o = jnp.einsum('bqjgn,bqnjd->bqjgd', probs, vg)
        return o.reshape(B, Q_BLOCK, N_Q_HEADS * HEAD_DIM).astype(q.dtype)

    outs = lax.map(block, (to_blocks(q), to_blocks(qi), to_blocks(wi), jnp.arange(nb) * Q_BLOCK))
    return outs.swapaxes(0, 1).reshape(B, T, N_Q_HEADS * HEAD_DIM)


def conv_ffn(h, w_up, conv_w, conv_b, w_down):
    u = h @ w_up
    T = u.shape[1]
    up = jnp.pad(u, ((0, 0), (CONV_WIDTH - 1, 0), (0, 0)))
    c = conv_b
    for j in range(CONV_WIDTH):
        c = c + up[:, j:j + T] * conv_w[j]
    gate, val = jnp.split(c, 2, axis=-1)
    return (jax.nn.silu(gate) * val) @ w_down


def setup_inputs(seed: int = 0) -> dict:
    key = jax.random.key(seed)
    ks = jax.random.split(key, 13)
    f32 = jnp.float32
    n = lambda k, s: jax.random.normal(k, s, dtype=f32)
    return {
        "x": n(ks[0], (BATCH, SEQ, D_MODEL)),
        "norm1_g": 1.0 + 0.02 * n(ks[1], (DEPTH, D_MODEL)),
        "w_in": n(ks[2], (DEPTH, D_MODEL, IN_WIDTH)) * D_MODEL ** -0.5,
        "q_norm_g": 1.0 + 0.02 * n(ks[3], (DEPTH, HEAD_DIM)),
        "k_norm_g": 1.0 + 0.02 * n(ks[4], (DEPTH, HEAD_DIM)),
        "pool_w": n(ks[5], (DEPTH, 4, POOL_GROUP, POOL_GROUP)) * POOL_GROUP ** -0.5,
        "pool_scale": 1.0 + 0.02 * n(ks[6], (DEPTH, POOL_WIDTH)),
        "w_out": n(ks[7], (DEPTH, MIX_WIDTH, D_MODEL)) * MIX_WIDTH ** -0.5,
        "norm2_g": 1.0 + 0.02 * n(ks[8], (DEPTH, D_MODEL)),
        "w_up": n(ks[9], (DEPTH, D_MODEL, 2 * D_FF)) * D_MODEL ** -0.5,
        "conv_w": n(ks[10], (DEPTH, CONV_WIDTH, 2 * D_FF)) * CONV_WIDTH ** -0.5,
        "conv_b": 0.02 * n(ks[11], (DEPTH, 2 * D_FF)),
        "w_down": n(ks[12], (DEPTH, D_FF, D_MODEL)) * D_FF ** -0.5,
    }


def reference(x, norm1_g, w_in, q_norm_g, k_norm_g, pool_w, pool_scale, w_out,
              norm2_g, w_up, conv_w, conv_b, w_down):
    B, T, _ = x.shape
    pos = jnp.arange(T)
    cuts = [int(c) for c in np.cumsum(IN_SPLITS)[:-1]]
    for l in range(DEPTH):
        h = rms_norm(x, norm1_g[l])
        proj = h @ w_in[l]
        v_pool, q, k, v, qi, ki, wi = jnp.split(proj, cuts, axis=-1)
        q = rope(rms_norm(q.reshape(B, T, N_Q_HEADS, HEAD_DIM), q_norm_g[l]), pos)
        k = rope(rms_norm(k.reshape(B, T, N_KV_HEADS, HEAD_DIM), k_norm_g[l]), pos)
        v = v.reshape(B, T, N_KV_HEADS, HEAD_DIM)
        qi = rope(qi.reshape(B, T, IDX_HEADS, IDX_DIM), pos)
        ki = rope(ki.reshape(B, T, 1, IDX_DIM), pos)[:, :, 0]
        wi = wi * (IDX_HEADS ** -0.5 * IDX_DIM ** -0.5)
        a_out = pool_mixer(v_pool, pool_w[l], pool_scale[l])
        b_out = dsa_mixer(q, k, v, qi, ki, wi)
        x = x + jnp.concatenate([a_out, b_out], axis=-1) @ w_out[l]
        x = x + conv_ffn(rms_norm(x, norm2_g[l]), w_up[l], conv_w[l], conv_b[l], w_down[l])
    return x
```

```python
import functools

import jax
import jax.numpy as jnp
from jax import lax
from jax.experimental import pallas as pl
from jax.experimental.pallas import tpu as pltpu

F32 = jnp.float32
BF16 = jnp.bfloat16

D_MODEL = 1024
POOL_WIDTH = 512
POOL_WINDOWS = (2, 4, 8, 16)
POOL_GROUP = 128
HEAD_DIM = 64
N_Q_HEADS = 8
N_KV_HEADS = 2
Q_PER_KV = 4
IDX_HEADS = 8
IDX_DIM = 32
TOPK_MAX = 256
ROPE_THETA = 10000.0
D_FF = 2816
CONV_WIDTH = 3
EPS = 1e-6
NEG = -1e30

LANES = 128
SUBLANES = 8
POOL_HALO = 16
QPAD = N_Q_HEADS * LANES
IDXW = 3 * LANES
WI_LANE = 3 * IDX_DIM
MAIN_W = POOL_WIDTH + QPAD + 2 * LANES

INPROJ_ROWS = 512
DSA_ROWS = 256
MIX_ROWS = 512
FFN_ROWS = 256
FFN_HALO = SUBLANES

NT_DIMS = (((1,), (1,)), ((), ()))


def _dot(a, b):
    return jnp.dot(a, b, preferred_element_type=F32)


def _dot_nt(a, b):
    return lax.dot_general(a, b, NT_DIMS, preferred_element_type=F32)


def _split_bf16(x):
    hi = x.astype(BF16).astype(F32)
    return hi, x - hi


def _swap_halves(xc, half):
    lane = lax.broadcasted_iota(jnp.int32, xc.shape, 1)
    first = (lane % (2 * half)) < half
    return jnp.where(first, pltpu.roll(xc, LANES - half, 1), pltpu.roll(xc, half, 1))


def _inproj_kernel(x_ref, g1_ref, wmain_ref, whi_ref, wlo_ref, qg_ref, kg_ref,
                   qcos_ref, qsin_ref, icos_ref, isin_ref,
                   pool_ref, q_ref, k_ref, v_ref, idx_ref, kcat_ref):
    x = x_ref[...]
    ms = jnp.mean(x * x, axis=-1, keepdims=True)
    h = x * lax.rsqrt(ms + EPS) * g1_ref[...]
    h_hi = h.astype(BF16)
    h_lo = (h - h_hi.astype(F32)).astype(BF16)

    proj = _dot(h_hi, wmain_ref[...])
    pool_ref[...] = proj[:, :POOL_WIDTH]

    qcos = qcos_ref[...]
    qsin = qsin_ref[...]
    lane = lax.broadcasted_iota(jnp.int32, (x.shape[0], LANES), 1)

    for hd in range(N_Q_HEADS):
        c0 = POOL_WIDTH + hd * LANES
        qc = proj[:, c0:c0 + LANES]
        ss = jnp.sum(qc * qc, axis=-1, keepdims=True) * (1.0 / HEAD_DIM)
        qn = qc * lax.rsqrt(ss + EPS) * qg_ref[:, hd * LANES:(hd + 1) * LANES]
        qr = qn * qcos + _swap_halves(qn, HEAD_DIM // 2) * qsin
        q_ref[:, hd * LANES:(hd + 1) * LANES] = (qr * (HEAD_DIM ** -0.5)).astype(BF16)

    kc = proj[:, POOL_WIDTH + QPAD:POOL_WIDTH + QPAD + LANES]
    ksq = kc * kc
    s0 = jnp.sum(jnp.where(lane < HEAD_DIM, ksq, 0.0), axis=-1, keepdims=True)
    s1 = jnp.sum(jnp.where(lane < HEAD_DIM, 0.0, ksq), axis=-1, keepdims=True)
    kss = jnp.where(lane < HEAD_DIM, s0, s1) * (1.0 / HEAD_DIM)
    kn = kc * lax.rsqrt(kss + EPS) * kg_ref[...]
    k_ref[...] = (kn * qcos + _swap_halves(kn, HEAD_DIM // 2) * qsin).astype(BF16)

    v_ref[...] = proj[:, POOL_WIDTH + QPAD + LANES:].astype(BF16)

    whi = whi_ref[...]
    idxp = _dot(h_hi, whi) + _dot(h_lo, whi) + _dot(h_hi, wlo_ref[...])
    for c in range(IDXW // LANES):
        sl = slice(c * LANES, (c + 1) * LANES)
        ic = idxp[:, sl]
        tail = ic * icos_ref[:, sl] + _swap_halves(ic, IDX_DIM // 2) * isin_ref[:, sl]
        idx_ref[:, sl] = tail

    t_hi, t_lo = _split_bf16(tail)
    mid = jnp.logical_and(lane >= IDX_DIM, lane < 2 * IDX_DIM)
    cat = jnp.where(mid, t_lo, t_hi)
    kcat_ref[...] = jnp.where(lane < WI_LANE, cat, 0.0).astype(BF16)


def _inproj(x2, g1, wmain, whi, wlo, qg, kg, qcos, qsin, icos, isin, seq):
    n = x2.shape[0]
    rows = INPROJ_ROWS
    tper = seq // rows
    row = lambda w: pl.BlockSpec((rows, w), lambda i: (i, 0))
    const = lambda a: pl.BlockSpec(a.shape, lambda i: (0, 0))
    table = lambda w: pl.BlockSpec((rows, w), lambda i: (i % tper, 0))
    return pl.pallas_call(
        _inproj_kernel,
        grid=(n // rows,),
        in_specs=[row(D_MODEL), const(g1), const(wmain), const(whi), const(wlo), const(qg), const(kg),
                  table(LANES), table(LANES), table(IDXW), table(IDXW)],
        out_specs=[row(POOL_WIDTH), row(QPAD), row(LANES), row(LANES), row(IDXW), row(LANES)],
        out_shape=[jax.ShapeDtypeStruct((n, POOL_WIDTH), F32),
                   jax.ShapeDtypeStruct((n, QPAD), BF16),
                   jax.ShapeDtypeStruct((n, LANES), BF16),
                   jax.ShapeDtypeStruct((n, LANES), BF16),
                   jax.ShapeDtypeStruct((n, IDXW), F32),
                   jax.ShapeDtypeStruct((n, LANES), BF16)],
        compiler_params=pltpu.CompilerParams(dimension_semantics=("arbitrary",),
                                             vmem_limit_bytes=48 * 1024 * 1024),
        name="inproj",
    )(x2, g1, wmain, whi, wlo, qg, kg, qcos, qsin, icos, isin)


def _key_to_float(key):
    bits = key ^ ((key >> 31) & jnp.int32(0x7FFFFFFF))
    return lax.bitcast_convert_type(bits, F32)


def _dsa_kernel(q_ref, idxq_ref, kcat_ref, k_ref, v_ref, out_ref, sc_ref, wb_ref, qcat_ref, *, seq, topk):
    rows = DSA_ROWS
    ck = DSA_ROWS
    nsub = ck // LANES
    qb = pl.program_id(1)
    nch = qb + 1
    lane = lax.broadcasted_iota(jnp.int32, (rows, LANES), 1)
    rowg = qb * rows + lax.broadcasted_iota(jnp.int32, (rows, LANES), 0)
    ones_bf = jnp.ones((LANES, LANES), BF16)

    for c in range(IDX_HEADS * IDX_DIM // LANES):
        hi, lo = _split_bf16(idxq_ref[:, c * LANES:(c + 1) * LANES])
        for r in range(LANES // IDX_DIM):
            s0 = (LANES - IDX_DIM * r) % LANES
            a = pltpu.roll(hi, s0, 1) if s0 else hi
            b = pltpu.roll(hi, (s0 + IDX_DIM) % LANES, 1)
            d = pltpu.roll(lo, (s0 + 2 * IDX_DIM) % LANES, 1)
            cat = jnp.where(lane < IDX_DIM, a,
                            jnp.where(lane < 2 * IDX_DIM, b, jnp.where(lane < 3 * IDX_DIM, d, 0.0)))
            qcat_ref[c * (LANES // IDX_DIM) + r] = cat.astype(BF16)
    tail = idxq_ref[:, 2 * LANES:]
    for hd in range(IDX_HEADS):
        wb_ref[hd] = jnp.broadcast_to(tail[:, WI_LANE + hd:WI_LANE + hd + 1], (rows, LANES))

    def score_body(j, carry):
        kc = kcat_ref[pl.ds(pl.multiple_of(j * ck, ck), ck), :]
        acc = [jnp.zeros((rows, LANES), F32) for _ in range(nsub)]
        for hd in range(IDX_HEADS):
            y = _dot_nt(qcat_ref[hd], kc)
            w = wb_ref[hd]
            for s in range(nsub):
                acc[s] = acc[s] + w * jnp.maximum(y[:, s * LANES:(s + 1) * LANES], 0.0)
        for s in range(nsub):
            colg = j * ck + s * LANES + lane
            sc_ref[j, :, s * LANES:(s + 1) * LANES] = jnp.where(colg <= rowg, acc[s], NEG)
        return carry

    lax.fori_loop(0, nch, score_body, 0)

    n_virtual = (seq - nch * ck).astype(F32)

    def count(pred, thr):
        def body(j, acc):
            for s in range(nsub):
                acc = acc + jnp.where(pred(sc_ref[j, :, s * LANES:(s + 1) * LANES], thr), 1.0, 0.0)
            return acc
        acc = lax.fori_loop(0, nch, body, jnp.zeros((rows, LANES), F32))
        tot = _dot(acc.astype(BF16), ones_bf)
        return tot + jnp.where(pred(jnp.float32(NEG), thr), n_virtual, 0.0)

    ge = lambda a, b: a >= b
    gt = lambda a, b: a > b
    kf = jnp.float32(topk)

    c0 = count(ge, jnp.zeros((rows, LANES), F32))
    key = jnp.where(c0 >= kf, jnp.int32(0), jnp.int32(-2 ** 31))

    def bit_body(i, key):
        cand = key | jnp.left_shift(jnp.int32(1), 30 - i)
        cnt = count(ge, _key_to_float(cand))
        return jnp.where(cnt >= kf, cand, key)

    key = lax.fori_loop(0, 31, bit_body, key)
    thr = _key_to_float(key)
    need = kf - count(gt, thr)

    sub_r = lax.broadcasted_iota(jnp.int32, (LANES, LANES), 0)
    sub_c = lax.broadcasted_iota(jnp.int32, (LANES, LANES), 1)
    before = jnp.where(sub_r < sub_c, 1.0, 0.0).astype(BF16)

    def select_body(j, run):
        for s in range(nsub):
            sl = slice(s * LANES, (s + 1) * LANES)
            sc = sc_ref[j, :, sl]
            eq = sc == thr
            eqb = jnp.where(eq, 1.0, 0.0).astype(BF16)
            prior = run + _dot(eqb, before)
            sel = jnp.logical_or(sc > thr, jnp.logical_and(eq, prior < need))
            colg = j * ck + s * LANES + lane
            sel = jnp.logical_and(sel, colg <= rowg)
            sc_ref[j, :, sl] = jnp.where(sel, 0.0, NEG)
            run = run + _dot(eqb, ones_bf)
        return run

    lax.fori_loop(0, nch, select_body, jnp.zeros((rows, LANES), F32))

    outs = []
    for hd in range(N_Q_HEADS):
        qh = q_ref[:, hd * LANES:(hd + 1) * LANES]

        def att_body(j, carry, qh=qh):
            m, l, acc = carry
            start = pl.multiple_of(j * ck, ck)
            s = _dot_nt(qh, k_ref[pl.ds(start, ck), :]) + sc_ref[j]
            m_new = jnp.maximum(m, jnp.max(s, axis=-1, keepdims=True))
            alpha = jnp.exp(m - m_new)
            p = jnp.exp(s - m_new)
            l = alpha * l + jnp.sum(p, axis=-1, keepdims=True)
            acc = alpha * acc + _dot(p.astype(BF16), v_ref[pl.ds(start, ck), :])
            return m_new, l, acc

        m0 = jnp.full((rows, 1), -jnp.inf, F32)
        l0 = jnp.zeros((rows, 1), F32)
        a0 = jnp.zeros((rows, LANES), F32)
        _, l, acc = lax.fori_loop(0, nch, att_body, (m0, l0, a0))
        outs.append(acc / l)

    for pair in range(N_Q_HEADS // 2):
        a, b = outs[2 * pair], outs[2 * pair + 1]
        if (2 * pair) // Q_PER_KV == 0:
            col = jnp.where(lane < HEAD_DIM, a, pltpu.roll(b, HEAD_DIM, 1))
        else:
            col = jnp.where(lane < HEAD_DIM, pltpu.roll(a, HEAD_DIM, 1), b)
        out_ref[:, pair * LANES:(pair + 1) * LANES] = col.astype(BF16)


def _dsa(q, idx, kcat, k, v, batch, seq):
    rows = DSA_ROWS
    nq = seq // rows
    topk = min(TOPK_MAX, seq // 4)
    assert topk <= rows and seq % rows == 0
    q3 = q.reshape(batch, seq, QPAD)
    idx3 = idx.reshape(batch, seq, IDXW)
    full = lambda a: pl.BlockSpec((None, seq, LANES), lambda b, i: (b, 0, 0))
    kern = functools.partial(_dsa_kernel, seq=seq, topk=topk)
    return pl.pallas_call(
        kern,
        grid=(batch, nq),
        in_specs=[pl.BlockSpec((None, rows, QPAD), lambda b, i: (b, i, 0)),
                  pl.BlockSpec((None, rows, IDXW), lambda b, i: (b, i, 0)),
                  full(kcat), full(k), full(v)],
        out_specs=pl.BlockSpec((None, rows, N_Q_HEADS * HEAD_DIM), lambda b, i: (b, i, 0)),
        out_shape=jax.ShapeDtypeStruct((batch, seq, N_Q_HEADS * HEAD_DIM), BF16),
        scratch_shapes=[pltpu.VMEM((nq, rows, rows), F32),
                        pltpu.VMEM((IDX_HEADS, rows, LANES), F32),
                        pltpu.VMEM((IDX_HEADS, rows, LANES), BF16)],
        compiler_params=pltpu.CompilerParams(dimension_semantics=("arbitrary", "arbitrary"),
                                             vmem_limit_bytes=48 * 1024 * 1024),
        name="dsa",
    )(q3, idx3, kcat.reshape(batch, seq, LANES), k.reshape(batch, seq, LANES), v.reshape(batch, seq, LANES))


def _mixout_kernel(pool_ref, prev_ref, b_ref, x_ref, pw_ref, ps_ref, woa_ref, wob_ref, out_ref):
    rows = MIX_ROWS
    ti = pl.program_id(1)
    prev = jnp.where(ti == 0, 0.0, prev_ref[...])
    ext = jnp.concatenate([prev, pool_ref[...]], axis=0)
    t = ti * rows + lax.broadcasted_iota(jnp.int32, (rows, 1), 0)
    parts = []
    for g, w in enumerate(POOL_WINDOWS):
        sl = slice(g * POOL_GROUP, (g + 1) * POOL_GROUP)
        a = ext[:, sl]
        span = 1
        while span < w:
            a = a + pltpu.roll(a, span, 0)
            span *= 2
        cnt = jnp.minimum(t + 1, w).astype(F32)
        p = a[POOL_HALO:] / cnt - ext[POOL_HALO:, sl]
        y = _dot(p.astype(BF16), pw_ref[g])
        parts.append(y * ps_ref[:, sl])
    a_out = jnp.concatenate(parts, axis=1).astype(BF16)
    out_ref[...] = x_ref[...] + _dot(a_out, woa_ref[...]) + _dot(b_ref[...], wob_ref[...])


def _mixout(pool, b_out, x2, pw, ps, woa, wob, batch, seq):
    rows = MIX_ROWS
    per = seq // rows
    hb = rows // POOL_HALO
    tile = lambda w: pl.BlockSpec((None, rows, w), lambda b, i: (b, i, 0))
    const = lambda a: pl.BlockSpec(a.shape, lambda b, i: (0,) * a.ndim)
    return pl.pallas_call(
        _mixout_kernel,
        grid=(batch, per),
        in_specs=[tile(POOL_WIDTH),
                  pl.BlockSpec((None, POOL_HALO, POOL_WIDTH), lambda b, i: (b, jnp.maximum(i * hb - 1, 0), 0)),
                  tile(POOL_WIDTH), tile(D_MODEL), const(pw), const(ps), const(woa), const(wob)],
        out_specs=tile(D_MODEL),
        out_shape=jax.ShapeDtypeStruct((batch, seq, D_MODEL), F32),
        compiler_params=pltpu.CompilerParams(dimension_semantics=("arbitrary", "arbitrary"),
                                             vmem_limit_bytes=48 * 1024 * 1024),
        name="mixout",
    )(pool.reshape(batch, seq, POOL_WIDTH), pool.reshape(batch, seq, POOL_WIDTH), b_out,
      x2.reshape(batch, seq, D_MODEL), pw, ps, woa, wob)


def _ffn_kernel(x_ref, prev_ref, g2_ref, wup_ref, cw_ref, cb_ref, wdn_ref, out_ref, act_ref):
    rows = FFN_ROWS
    ti = pl.program_id(1)
    g2 = g2_ref[...]

    def norm(v):
        return v * lax.rsqrt(jnp.mean(v * v, axis=-1, keepdims=True) + EPS) * g2

    x = x_ref[...]
    hp = jnp.where(ti == 0, 0.0, norm(prev_ref[...]))
    h_ext = jnp.concatenate([hp, norm(x)], axis=0).astype(BF16)

    chunk = 2 * LANES
    for c in range(D_FF // chunk):
        halves = []
        for off in (0, D_FF):
            sl = slice(off + c * chunk, off + (c + 1) * chunk)
            u = _dot(h_ext, wup_ref[:, sl])
            u1 = pltpu.roll(u, 1, 0)
            u2 = pltpu.roll(u, 2, 0)
            cv = cb_ref[:, sl] + u2[FFN_HALO:] * cw_ref[0:1, sl]
            cv = cv + u1[FFN_HALO:] * cw_ref[1:2, sl]
            cv = cv + u[FFN_HALO:] * cw_ref[2:3, sl]
            halves.append(cv)
        gate, val = halves
        act = gate * (1.0 / (1.0 + jnp.exp(-gate))) * val
        act_ref[:, c * chunk:(c + 1) * chunk] = act.astype(BF16)
    out_ref[...] = x + _dot(act_ref[...], wdn_ref[...])


def _ffn(x1, g2, wup, cw, cb, wdn, batch, seq):
    rows = FFN_ROWS
    per = seq // rows
    hb = rows // FFN_HALO
    tile = pl.BlockSpec((None, rows, D_MODEL), lambda b, i: (b, i, 0))
    const = lambda a: pl.BlockSpec(a.shape, lambda b, i: (0,) * a.ndim, pipeline_mode=pl.Buffered(1))
    return pl.pallas_call(
        _ffn_kernel,
        grid=(batch, per),
        in_specs=[tile,
                  pl.BlockSpec((None, FFN_HALO, D_MODEL), lambda b, i: (b, jnp.maximum(i * hb - 1, 0), 0)),
                  const(g2), const(wup), const(cw), const(cb), const(wdn)],
        out_specs=tile,
        out_shape=jax.ShapeDtypeStruct((batch, seq, D_MODEL), F32),
        scratch_shapes=[pltpu.VMEM((rows, D_FF), BF16)],
        compiler_params=pltpu.CompilerParams(dimension_semantics=("arbitrary", "arbitrary"),
                                             vmem_limit_bytes=56 * 1024 * 1024),
        name="ffn",
    )(x1, x1, g2, wup, cw, cb, wdn)


def _rope_tables(seq, dim):
    half = dim // 2
    inv = jnp.exp(-jnp.log(jnp.float32(ROPE_THETA)) * jnp.arange(half, dtype=F32) / half)
    ang = jnp.arange(seq).astype(F32)[:, None] * inv[None, :]
    cos, sin = jnp.cos(ang), jnp.sin(ang)
    return jnp.concatenate([cos, cos], axis=1), jnp.concatenate([-sin, sin], axis=1)


def _layer(x, norm1_g, w_in, q_norm_g, k_norm_g, pool_w, pool_scale, w_out, norm2_g, w_up, conv_w, conv_b, w_down):
    batch, seq, _ = x.shape
    n = batch * seq
    cuts = [POOL_WIDTH, POOL_WIDTH + N_Q_HEADS * HEAD_DIM]
    cuts += [cuts[-1] + N_KV_HEADS * HEAD_DIM]
    cuts += [cuts[-1] + N_KV_HEADS * HEAD_DIM]
    cuts += [cuts[-1] + IDX_HEADS * IDX_DIM]
    cuts += [cuts[-1] + IDX_DIM]
    w_pool, w_q, w_k, w_v, w_qi, w_ki, w_wi = jnp.split(w_in, cuts, axis=1)

    zq = jnp.zeros((D_MODEL, HEAD_DIM), F32)
    qcols, qgains = [], []
    for hd in range(N_Q_HEADS):
        blk = w_q[:, hd * HEAD_DIM:(hd + 1) * HEAD_DIM]
        first = hd // Q_PER_KV == 0
        qcols += [blk, zq] if first else [zq, blk]
        qgains += [q_norm_g, jnp.zeros_like(q_norm_g)] if first else [jnp.zeros_like(q_norm_g), q_norm_g]
    wmain = jnp.concatenate([w_pool] + qcols + [w_k, w_v], axis=1).astype(BF16)
    qg = jnp.concatenate(qgains)[None, :]
    kg = jnp.concatenate([k_norm_g, k_norm_g])[None, :]
    w_idx = jnp.concatenate([w_qi, w_ki, w_ki, w_ki, w_wi,
                             jnp.zeros((D_MODEL, IDXW - WI_LANE - IDX_HEADS - IDX_HEADS * IDX_DIM), F32)], axis=1)
    whi = w_idx.astype(BF16)
    wlo = (w_idx - whi.astype(F32)).astype(BF16)

    c64, s64 = _rope_tables(seq, HEAD_DIM)
    qcos, qsin = jnp.tile(c64, (1, LANES // HEAD_DIM)), jnp.tile(s64, (1, LANES // HEAD_DIM))
    c32, s32 = _rope_tables(seq, IDX_DIM)
    nrot = (IDX_HEADS + 3) * IDX_DIM
    wscale = jnp.full((seq, IDX_HEADS), IDX_HEADS ** -0.5 * IDX_DIM ** -0.5, F32)
    icos = jnp.concatenate([jnp.tile(c32, (1, nrot // IDX_DIM)), wscale,
                            jnp.zeros((seq, IDXW - nrot - IDX_HEADS), F32)], axis=1)
    isin = jnp.concatenate([jnp.tile(s32, (1, nrot // IDX_DIM)), jnp.zeros((seq, IDXW - nrot), F32)], axis=1)

    x2 = x.reshape(n, D_MODEL)
    pool, q, k, v, idx, kcat = _inproj(x2, norm1_g[None, :], wmain, whi, wlo, qg, kg, qcos, qsin, icos, isin, seq)
    b_out = _dsa(q, idx, kcat, k, v, batch, seq)
    x1 = _mixout(pool, b_out, x2, pool_w.astype(BF16), pool_scale[None, :],
                 w_out[:POOL_WIDTH].astype(BF16), w_out[POOL_WIDTH:].astype(BF16), batch, seq)
    return _ffn(x1, norm2_g[None, :], w_up.astype(BF16), conv_w, conv_b[None, :], w_down.astype(BF16), batch, seq)


def kernel(x, norm1_g, w_in, q_norm_g, k_norm_g, pool_w, pool_scale, w_out, norm2_g, w_up, conv_w, conv_b, w_down):
    for l in range(norm1_g.shape[0]):
        x = _layer(x, norm1_g[l], w_in[l], q_norm_g[l], k_norm_g[l], pool_w[l], pool_scale[l], w_out[l],
                   norm2_g[l], w_up[l], conv_w[l], conv_b[l], w_down[l])
    return x
```

```python
import functools

import jax
import jax.numpy as jnp
from jax import lax
from jax.experimental import pallas as pl
from jax.experimental.pallas import tpu as pltpu

F32 = jnp.float32
BF16 = jnp.bfloat16

D_MODEL = 1024
POOL_WIDTH = 512
POOL_WINDOWS = (2, 4, 8, 16)
POOL_GROUP = 128
HEAD_DIM = 64
N_Q_HEADS = 8
N_KV_HEADS = 2
Q_PER_KV = 4
IDX_HEADS = 8
IDX_DIM = 32
TOPK_MAX = 256
ROPE_THETA = 10000.0
D_FF = 2816
CONV_WIDTH = 3
EPS = 1e-6
NEG = -1e30

LANES = 128
SUBLANES = 8
POOL_HALO = 16
QPAD = N_Q_HEADS * LANES
IDXW = 3 * LANES
WI_LANE = 3 * IDX_DIM
MAIN_W = POOL_WIDTH + QPAD + 2 * LANES

INPROJ_ROWS = 512
DSA_ROWS = 256
MIX_ROWS = 512
FFN_ROWS = 256
FFN_HALO = SUBLANES

NT_DIMS = (((1,), (1,)), ((), ()))


def _dot(a, b):
    return jnp.dot(a, b, preferred_element_type=F32)


def _dot_nt(a, b):
    return lax.dot_general(a, b, NT_DIMS, preferred_element_type=F32)


def _split_bf16(x):
    hi = x.astype(BF16).astype(F32)
    return hi, x - hi


def _swap_halves(xc, half):
    lane = lax.broadcasted_iota(jnp.int32, xc.shape, 1)
    first = (lane % (2 * half)) < half
    return jnp.where(first, pltpu.roll(xc, LANES - half, 1), pltpu.roll(xc, half, 1))


def _inproj_kernel(x_ref, g1_ref, wmain_ref, whi_ref, wlo_ref, qg_ref, kg_ref,
                   qcos_ref, qsin_ref, icos_ref, isin_ref,
                   pool_ref, q_ref, k_ref, v_ref, idx_ref, kcat_ref):
    x = x_ref[...]
    ms = jnp.mean(x * x, axis=-1, keepdims=True)
    h = x * lax.rsqrt(ms + EPS) * g1_ref[...]
    h_hi = h.astype(BF16)
    h_lo = (h - h_hi.astype(F32)).astype(BF16)

    proj = _dot(h_hi, wmain_ref[...])
    pool_ref[...] = proj[:, :POOL_WIDTH]

    qcos = qcos_ref[...]
    qsin = qsin_ref[...]
    lane = lax.broadcasted_iota(jnp.int32, (x.shape[0], LANES), 1)

    for hd in range(N_Q_HEADS):
        c0 = POOL_WIDTH + hd * LANES
        qc = proj[:, c0:c0 + LANES]
        ss = jnp.sum(qc * qc, axis=-1, keepdims=True) * (1.0 / HEAD_DIM)
        qn = qc * lax.rsqrt(ss + EPS) * qg_ref[:, hd * LANES:(hd + 1) * LANES]
        qr = qn * qcos + _swap_halves(qn, HEAD_DIM // 2) * qsin
        q_ref[:, hd * LANES:(hd + 1) * LANES] = (qr * (HEAD_DIM ** -0.5)).astype(BF16)

    kc = proj[:, POOL_WIDTH + QPAD:POOL_WIDTH + QPAD + LANES]
    ksq = kc * kc
    s0 = jnp.sum(jnp.where(lane < HEAD_DIM, ksq, 0.0), axis=-1, keepdims=True)
    s1 = jnp.sum(jnp.where(lane < HEAD_DIM, 0.0, ksq), axis=-1, keepdims=True)
    kss = jnp.where(lane < HEAD_DIM, s0, s1) * (1.0 / HEAD_DIM)
    kn = kc * lax.rsqrt(kss + EPS) * kg_ref[...]
    k_ref[...] = (kn * qcos + _swap_halves(kn, HEAD_DIM // 2) * qsin).astype(BF16)

    v_ref[...] = proj[:, POOL_WIDTH + QPAD + LANES:].astype(BF16)

    whi = whi_ref[...]
    idxp = _dot(h_hi, whi) + _dot(h_lo, whi) + _dot(h_hi, wlo_ref[...])
    for c in range(IDXW // LANES):
        sl = slice(c * LANES, (c + 1) * LANES)
        ic = idxp[:, sl]
        tail = ic * icos_ref[:, sl] + _swap_halves(ic, IDX_DIM // 2) * isin_ref[:, sl]
        idx_ref[:, sl] = tail

    t_hi, t_lo = _split_bf16(tail)
    mid = jnp.logical_and(lane >= IDX_DIM, lane < 2 * IDX_DIM)
    cat = jnp.where(mid, t_lo, t_hi)
    kcat_ref[...] = jnp.where(lane < WI_LANE, cat, 0.0).astype(BF16)


def _inproj(x2, g1, wmain, whi, wlo, qg, kg, qcos, qsin, icos, isin, seq):
    n = x2.shape[0]
    rows = INPROJ_ROWS
    tper = seq // rows
    row = lambda w: pl.BlockSpec((rows, w), lambda i: (i, 0))
    const = lambda a: pl.BlockSpec(a.shape, lambda i: (0, 0))
    table = lambda w: pl.BlockSpec((rows, w), lambda i: (i % tper, 0))
    return pl.pallas_call(
        _inproj_kernel,
        grid=(n // rows,),
        in_specs=[row(D_MODEL), const(g1), const(wmain), const(whi), const(wlo), const(qg), const(kg),
                  table(LANES), table(LANES), table(IDXW), table(IDXW)],
        out_specs=[row(POOL_WIDTH), row(QPAD), row(LANES), row(LANES), row(IDXW), row(LANES)],
        out_shape=[jax.ShapeDtypeStruct((n, POOL_WIDTH), F32),
                   jax.ShapeDtypeStruct((n, QPAD), BF16),
                   jax.ShapeDtypeStruct((n, LANES), BF16),
                   jax.ShapeDtypeStruct((n, LANES), BF16),
                   jax.ShapeDtypeStruct((n, IDXW), F32),
                   jax.ShapeDtypeStruct((n, LANES), BF16)],
        compiler_params=pltpu.CompilerParams(dimension_semantics=("arbitrary",),
                                             vmem_limit_bytes=48 * 1024 * 1024),
        name="inproj",
    )(x2, g1, wmain, whi, wlo, qg, kg, qcos, qsin, icos, isin)


def _key_to_float(key):
    bits = key ^ ((key >> 31) & jnp.int32(0x7FFFFFFF))
    return lax.bitcast_convert_type(bits, F32)


def _fold_groups(x, op, ways=4):
    n = x.shape[0] // SUBLANES
    accs = [x[g * SUBLANES:(g + 1) * SUBLANES, :] for g in range(ways)]
    for g in range(ways, n):
        accs[g % ways] = op(accs[g % ways], x[g * SUBLANES:(g + 1) * SUBLANES, :])
    while len(accs) > 1:
        accs = [op(accs[i], accs[i + 1]) for i in range(0, len(accs), 2)]
    return accs[0]


def _fold_sublanes(x, op):
    for shift in (4, 2, 1):
        x = op(x, pltpu.roll(x, shift, 0))
    return x


def _dsa_kernel(q_ref, idxq_ref, kcat_ref, k_ref, vt_ref, out_ref, sc_ref, qcat_ref, acc_ref, p_ref,
                *, seq, topk):
    rows = DSA_ROWS
    ck = DSA_ROWS
    ngrp = ck // SUBLANES
    qb = pl.program_id(1)
    nch = qb + 1
    lane = lax.broadcasted_iota(jnp.int32, (rows, LANES), 1)
    qidx = qb * rows + lax.broadcasted_iota(jnp.int32, (ck, rows), 1)
    krow = lax.broadcasted_iota(jnp.int32, (ck, rows), 0)

    for c in range(IDX_HEADS * IDX_DIM // LANES):
        hi, lo = _split_bf16(idxq_ref[:, c * LANES:(c + 1) * LANES])
        for r in range(LANES // IDX_DIM):
            s0 = (LANES - IDX_DIM * r) % LANES
            a = pltpu.roll(hi, s0, 1) if s0 else hi
            b = pltpu.roll(hi, (s0 + IDX_DIM) % LANES, 1)
            d = pltpu.roll(lo, (s0 + 2 * IDX_DIM) % LANES, 1)
            cat = jnp.where(lane < IDX_DIM, a,
                            jnp.where(lane < 2 * IDX_DIM, b, jnp.where(lane < 3 * IDX_DIM, d, 0.0)))
            qcat_ref[c * (LANES // IDX_DIM) + r] = cat.astype(BF16)
    wt = idxq_ref[:, 2 * LANES:].T[WI_LANE:WI_LANE + IDX_HEADS, :]

    def score_body(j, carry):
        kc = kcat_ref[pl.ds(pl.multiple_of(j * ck, ck), ck), :]
        acc = jnp.zeros((ck, rows), F32)
        for hd in range(IDX_HEADS):
            y = _dot_nt(kc, qcat_ref[hd])
            acc = acc + wt[hd:hd + 1, :] * jnp.maximum(y, 0.0)
        sc_ref[j] = jnp.where(j * ck + krow <= qidx, acc, NEG)
        return carry

    lax.fori_loop(0, nch, score_body, 0)

    n_virtual = (seq - nch * ck).astype(F32)

    def count(pred, thr):
        def body(j, accs):
            accs = list(accs)
            for g in range(ngrp):
                hit = pred(sc_ref[j, g * SUBLANES:(g + 1) * SUBLANES, :], thr)
                accs[g % len(accs)] = accs[g % len(accs)] + jnp.where(hit, 1.0, 0.0)
            return tuple(accs)
        zero = jnp.zeros((SUBLANES, rows), F32)
        accs = lax.fori_loop(0, nch, body, (zero, zero, zero, zero))
        tot = _fold_sublanes((accs[0] + accs[1]) + (accs[2] + accs[3]), jnp.add)
        return tot + jnp.where(pred(jnp.float32(NEG), thr), n_virtual, 0.0)

    ge = lambda a, b: a >= b
    gt = lambda a, b: a > b
    kf = jnp.float32(topk)

    c0 = count(ge, jnp.zeros((SUBLANES, rows), F32))
    key = jnp.where(c0 >= kf, jnp.int32(0), jnp.int32(-2 ** 31))

    def bit_body(i, key):
        cand = key | jnp.left_shift(jnp.int32(1), 30 - i)
        cnt = count(ge, _key_to_float(cand))
        return jnp.where(cnt >= kf, cand, key)

    key = lax.fori_loop(0, 31, bit_body, key)
    thr8 = _key_to_float(key)
    thr = thr8[0:1, :]
    need = (kf - count(gt, thr8))[0:1, :]

    ki = lax.broadcasted_iota(jnp.int32, (ck, ck), 0)
    kj = lax.broadcasted_iota(jnp.int32, (ck, ck), 1)
    lower = jnp.where(kj < ki, 1.0, 0.0).astype(BF16)
    ones8 = jnp.ones((SUBLANES, ck), BF16)

    def select_body(j, run):
        sc = sc_ref[j]
        eq = sc == thr
        eqb = jnp.where(eq, 1.0, 0.0).astype(BF16)
        prior = run[0:1, :] + _dot(lower, eqb)
        sel = jnp.logical_or(sc > thr, jnp.logical_and(eq, prior < need))
        sel = jnp.logical_and(sel, j * ck + krow <= qidx)
        sc_ref[j] = jnp.where(sel, 0.0, NEG)
        return run + _dot(ones8, eqb)

    lax.fori_loop(0, nch, select_body, jnp.zeros((SUBLANES, rows), F32))

    def logits(j, hd):
        kc = k_ref[pl.ds(pl.multiple_of(j * ck, ck), ck), :]
        return _dot_nt(kc, q_ref[:, hd * LANES:(hd + 1) * LANES]) + sc_ref[j]

    def max_body(j, ms):
        return tuple(jnp.maximum(ms[hd], _fold_groups(logits(j, hd), jnp.maximum)) for hd in range(N_Q_HEADS))

    ms = lax.fori_loop(0, nch, max_body,
                       tuple(jnp.full((SUBLANES, rows), -jnp.inf, F32) for _ in range(N_Q_HEADS)))
    ms = [_fold_sublanes(m, jnp.maximum)[0:1, :] for m in ms]

    acc_ref[...] = jnp.zeros(acc_ref.shape, F32)

    def pv_body(j, ls):
        out = []
        for hd in range(N_Q_HEADS):
            p = jnp.exp(logits(j, hd) - ms[hd])
            out.append(ls[hd] + _fold_groups(p, jnp.add))
            p_ref[hd] = p.astype(BF16)
        vt = vt_ref[j]
        for hd in range(N_Q_HEADS):
            acc_ref[hd] += _dot(vt, p_ref[hd])
        return tuple(out)

    ls = lax.fori_loop(0, nch, pv_body, tuple(jnp.zeros((SUBLANES, rows), F32) for _ in range(N_Q_HEADS)))

    outs = []
    for hd in range(N_Q_HEADS):
        l = _fold_sublanes(ls[hd], jnp.add)
        r0 = HEAD_DIM * (hd // Q_PER_KV)
        outs.append(acc_ref[hd, r0:r0 + HEAD_DIM, :] / l[0:1, :])
    out_ref[...] = jnp.concatenate(outs, axis=0).T.astype(BF16)


def _dsa(q, idx, kcat, k, v, batch, seq):
    rows = DSA_ROWS
    nq = seq // rows
    topk = min(TOPK_MAX, seq // 4)
    assert topk <= rows and seq % rows == 0
    q3 = q.reshape(batch, seq, QPAD)
    idx3 = idx.reshape(batch, seq, IDXW)
    vt = v.reshape(batch, nq, rows, LANES).transpose(0, 1, 3, 2)
    full = pl.BlockSpec((None, seq, LANES), lambda b, i: (b, 0, 0))
    kern = functools.partial(_dsa_kernel, seq=seq, topk=topk)
    return pl.pallas_call(
        kern,
        grid=(batch, nq),
        in_specs=[pl.BlockSpec((None, rows, QPAD), lambda b, i: (b, i, 0)),
                  pl.BlockSpec((None, rows, IDXW), lambda b, i: (b, i, 0)),
                  full, full,
                  pl.BlockSpec((None, nq, LANES, rows), lambda b, i: (b, 0, 0, 0))],
        out_specs=pl.BlockSpec((None, rows, N_Q_HEADS * HEAD_DIM), lambda b, i: (b, i, 0)),
        out_shape=jax.ShapeDtypeStruct((batch, seq, N_Q_HEADS * HEAD_DIM), BF16),
        scratch_shapes=[pltpu.VMEM((nq, rows, rows), F32),
                        pltpu.VMEM((IDX_HEADS, rows, LANES), BF16),
                        pltpu.VMEM((N_Q_HEADS, LANES, rows), F32),
                        pltpu.VMEM((N_Q_HEADS, rows, rows), BF16)],
        compiler_params=pltpu.CompilerParams(dimension_semantics=("arbitrary", "arbitrary"),
                                             vmem_limit_bytes=48 * 1024 * 1024),
        name="dsa",
    )(q3, idx3, kcat.reshape(batch, seq, LANES), k.reshape(batch, seq, LANES), vt)


def _mixout_kernel(pool_ref, prev_ref, b_ref, x_ref, pw_ref, ps_ref, woa_ref, wob_ref, out_ref):
    rows = MIX_ROWS
    ti = pl.program_id(1)
    prev = jnp.where(ti == 0, 0.0, prev_ref[...])
    ext = jnp.concatenate([prev, pool_ref[...]], axis=0)
    t = ti * rows + lax.broadcasted_iota(jnp.int32, (rows, 1), 0)
    parts = []
    for g, w in enumerate(POOL_WINDOWS):
        sl = slice(g * POOL_GROUP, (g + 1) * POOL_GROUP)
        a = ext[:, sl]
        span = 1
        while span < w:
            a = a + pltpu.roll(a, span, 0)
            span *= 2
        cnt = jnp.minimum(t + 1, w).astype(F32)
        p = a[POOL_HALO:] / cnt - ext[POOL_HALO:, sl]
        y = _dot(p.astype(BF16), pw_ref[g])
        parts.append(y * ps_ref[:, sl])
    a_out = jnp.concatenate(parts, axis=1).astype(BF16)
    out_ref[...] = x_ref[...] + _dot(a_out, woa_ref[...]) + _dot(b_ref[...], wob_ref[...])


def _mixout(pool, b_out, x2, pw, ps, woa, wob, batch, seq):
    rows = MIX_ROWS
    per = seq // rows
    hb = rows // POOL_HALO
    tile = lambda w: pl.BlockSpec((None, rows, w), lambda b, i: (b, i, 0))
    const = lambda a: pl.BlockSpec(a.shape, lambda b, i: (0,) * a.ndim)
    return pl.pallas_call(
        _mixout_kernel,
        grid=(batch, per),
        in_specs=[tile(POOL_WIDTH),
                  pl.BlockSpec((None, POOL_HALO, POOL_WIDTH), lambda b, i: (b, jnp.maximum(i * hb - 1, 0), 0)),
                  tile(POOL_WIDTH), tile(D_MODEL), const(pw), const(ps), const(woa), const(wob)],
        out_specs=tile(D_MODEL),
        out_shape=jax.ShapeDtypeStruct((batch, seq, D_MODEL), F32),
        compiler_params=pltpu.CompilerParams(dimension_semantics=("arbitrary", "arbitrary"),
                                             vmem_limit_bytes=48 * 1024 * 1024),
        name="mixout",
    )(pool.reshape(batch, seq, POOL_WIDTH), pool.reshape(batch, seq, POOL_WIDTH), b_out,
      x2.reshape(batch, seq, D_MODEL), pw, ps, woa, wob)


def _ffn_kernel(x_ref, prev_ref, g2_ref, wup_ref, cw_ref, cb_ref, wdn_ref, out_ref, act_ref):
    rows = FFN_ROWS
    ti = pl.program_id(1)
    g2 = g2_ref[...]

    def norm(v):
        return v * lax.rsqrt(jnp.mean(v * v, axis=-1, keepdims=True) + EPS) * g2

    x = x_ref[...]
    hp = jnp.where(ti == 0, 0.0, norm(prev_ref[...]))
    h_ext = jnp.concatenate([hp, norm(x)], axis=0).astype(BF16)

    chunk = 2 * LANES
    for c in range(D_FF // chunk):
        halves = []
        for off in (0, D_FF):
            sl = slice(off + c * chunk, off + (c + 1) * chunk)
            u = _dot(h_ext, wup_ref[:, sl])
            u1 = pltpu.roll(u, 1, 0)
            u2 = pltpu.roll(u, 2, 0)
            cv = cb_ref[:, sl] + u2[FFN_HALO:] * cw_ref[0:1, sl]
            cv = cv + u1[FFN_HALO:] * cw_ref[1:2, sl]
            cv = cv + u[FFN_HALO:] * cw_ref[2:3, sl]
            halves.append(cv)
        gate, val = halves
        act = gate * (1.0 / (1.0 + jnp.exp(-gate))) * val
        act_ref[:, c * chunk:(c + 1) * chunk] = act.astype(BF16)
    out_ref[...] = x + _dot(act_ref[...], wdn_ref[...])


def _ffn(x1, g2, wup, cw, cb, wdn, batch, seq):
    rows = FFN_ROWS
    per = seq // rows
    hb = rows // FFN_HALO
    tile = pl.BlockSpec((None, rows, D_MODEL), lambda b, i: (b, i, 0))
    const = lambda a: pl.BlockSpec(a.shape, lambda b, i: (0,) * a.ndim, pipeline_mode=pl.Buffered(1))
    return pl.pallas_call(
        _ffn_kernel,
        grid=(batch, per),
        in_specs=[tile,
                  pl.BlockSpec((None, FFN_HALO, D_MODEL), lambda b, i: (b, jnp.maximum(i * hb - 1, 0), 0)),
                  const(g2), const(wup), const(cw), const(cb), const(wdn)],
        out_specs=tile,
        out_shape=jax.ShapeDtypeStruct((batch, seq, D_MODEL), F32),
        scratch_shapes=[pltpu.VMEM((rows, D_FF), BF16)],
        compiler_params=pltpu.CompilerParams(dimension_semantics=("arbitrary", "arbitrary"),
                                             vmem_limit_bytes=56 * 1024 * 1024),
        name="ffn",
    )(x1, x1, g2, wup, cw, cb, wdn)


def _rope_tables(seq, dim):
    half = dim // 2
    inv = jnp.exp(-jnp.log(jnp.float32(ROPE_THETA)) * jnp.arange(half, dtype=F32) / half)
    ang = jnp.arange(seq).astype(F32)[:, None] * inv[None, :]
    cos, sin = jnp.cos(ang), jnp.sin(ang)
    return jnp.concatenate([cos, cos], axis=1), jnp.concatenate([-sin, sin], axis=1)


def _layer(x, norm1_g, w_in, q_norm_g, k_norm_g, pool_w, pool_scale, w_out, norm2_g, w_up, conv_w, conv_b, w_down):
    batch, seq, _ = x.shape
    n = batch * seq
    cuts = [POOL_WIDTH, POOL_WIDTH + N_Q_HEADS * HEAD_DIM]
    cuts += [cuts[-1] + N_KV_HEADS * HEAD_DIM]
    cuts += [cuts[-1] + N_KV_HEADS * HEAD_DIM]
    cuts += [cuts[-1] + IDX_HEADS * IDX_DIM]
    cuts += [cuts[-1] + IDX_DIM]
    w_pool, w_q, w_k, w_v, w_qi, w_ki, w_wi = jnp.split(w_in, cuts, axis=1)

    zq = jnp.zeros((D_MODEL, HEAD_DIM), F32)
    qcols, qgains = [], []
    for hd in range(N_Q_HEADS):
        blk = w_q[:, hd * HEAD_DIM:(hd + 1) * HEAD_DIM]
        first = hd // Q_PER_KV == 0
        qcols += [blk, zq] if first else [zq, blk]
        qgains += [q_norm_g, jnp.zeros_like(q_norm_g)] if first else [jnp.zeros_like(q_norm_g), q_norm_g]
    wmain = jnp.concatenate([w_pool] + qcols + [w_k, w_v], axis=1).astype(BF16)
    qg = jnp.concatenate(qgains)[None, :]
    kg = jnp.concatenate([k_norm_g, k_norm_g])[None, :]
    w_idx = jnp.concatenate([w_qi, w_ki, w_ki, w_ki, w_wi,
                             jnp.zeros((D_MODEL, IDXW - WI_LANE - IDX_HEADS - IDX_HEADS * IDX_DIM), F32)], axis=1)
    whi = w_idx.astype(BF16)
    wlo = (w_idx - whi.astype(F32)).astype(BF16)

    c64, s64 = _rope_tables(seq, HEAD_DIM)
    qcos, qsin = jnp.tile(c64, (1, LANES // HEAD_DIM)), jnp.tile(s64, (1, LANES // HEAD_DIM))
    c32, s32 = _rope_tables(seq, IDX_DIM)
    nrot = (IDX_HEADS + 3) * IDX_DIM
    wscale = jnp.full((seq, IDX_HEADS), IDX_HEADS ** -0.5 * IDX_DIM ** -0.5, F32)
    icos = jnp.concatenate([jnp.tile(c32, (1, nrot // IDX_DIM)), wscale,
                            jnp.zeros((seq, IDXW - nrot - IDX_HEADS), F32)], axis=1)
    isin = jnp.concatenate([jnp.tile(s32, (1, nrot // IDX_DIM)), jnp.zeros((seq, IDXW - nrot), F32)], axis=1)

    x2 = x.reshape(n, D_MODEL)
    pool, q, k, v, idx, kcat = _inproj(x2, norm1_g[None, :], wmain, whi, wlo, qg, kg, qcos, qsin, icos, isin, seq)
    b_out = _dsa(q, idx, kcat, k, v, batch, seq)
    x1 = _mixout(pool, b_out, x2, pool_w.astype(BF16), pool_scale[None, :],
                 w_out[:POOL_WIDTH].astype(BF16), w_out[POOL_WIDTH:].astype(BF16), batch, seq)
    return _ffn(x1, norm2_g[None, :], w_up.astype(BF16), conv_w, conv_b[None, :], w_down.astype(BF16), batch, seq)


def kernel(x, norm1_g, w_in, q_norm_g, k_norm_g, pool_w, pool_scale, w_out, norm2_g, w_up, conv_w, conv_b, w_down):
    for l in range(norm1_g.shape[0]):
        x = _layer(x, norm1_g[l], w_in[l], q_norm_g[l], k_norm_g[l], pool_w[l], pool_scale[l], w_out[l],
                   norm2_g[l], w_up[l], conv_w[l], conv_b[l], w_down[l])
    return x
```

```python
import functools

import jax
import jax.numpy as jnp
from jax import lax
from jax.experimental import pallas as pl
from jax.experimental.pallas import tpu as pltpu

F32 = jnp.float32
BF16 = jnp.bfloat16

D_MODEL = 1024
POOL_WIDTH = 512
POOL_WINDOWS = (2, 4, 8, 16)
POOL_GROUP = 128
HEAD_DIM = 64
N_Q_HEADS = 8
N_KV_HEADS = 2
Q_PER_KV = 4
IDX_HEADS = 8
IDX_DIM = 32
TOPK_MAX = 256
ROPE_THETA = 10000.0
D_FF = 2816
CONV_WIDTH = 3
EPS = 1e-6
NEG = -1e30

LANES = 128
SUBLANES = 8
POOL_HALO = 16
QPAD = N_Q_HEADS * LANES
IDXW = 3 * LANES
WI_LANE = 3 * IDX_DIM
MAIN_W = POOL_WIDTH + QPAD + 2 * LANES

INPROJ_ROWS = 512
DSA_ROWS = 256
MIX_ROWS = 512
FFN_ROWS = 256
FFN_HALO = SUBLANES

NT_DIMS = (((1,), (1,)), ((), ()))


def _dot(a, b):
    return jnp.dot(a, b, preferred_element_type=F32)


def _dot_nt(a, b):
    return lax.dot_general(a, b, NT_DIMS, preferred_element_type=F32)


def _split_bf16(x):
    hi = x.astype(BF16).astype(F32)
    return hi, x - hi


def _swap_halves(xc, half):
    lane = lax.broadcasted_iota(jnp.int32, xc.shape, 1)
    first = (lane % (2 * half)) < half
    return jnp.where(first, pltpu.roll(xc, LANES - half, 1), pltpu.roll(xc, half, 1))


def _inproj_kernel(x_ref, g1_ref, wmain_ref, whi_ref, wlo_ref, qg_ref, kg_ref,
                   qcos_ref, qsin_ref, icos_ref, isin_ref,
                   pool_ref, q_ref, k_ref, v_ref, idx_ref, kcat_ref):
    x = x_ref[...]
    ms = jnp.mean(x * x, axis=-1, keepdims=True)
    h = x * lax.rsqrt(ms + EPS) * g1_ref[...]
    h_hi = h.astype(BF16)
    h_lo = (h - h_hi.astype(F32)).astype(BF16)

    proj = _dot(h_hi, wmain_ref[...])
    pool_ref[...] = proj[:, :POOL_WIDTH]

    qcos = qcos_ref[...]
    qsin = qsin_ref[...]
    lane = lax.broadcasted_iota(jnp.int32, (x.shape[0], LANES), 1)

    for hd in range(N_Q_HEADS):
        c0 = POOL_WIDTH + hd * LANES
        qc = proj[:, c0:c0 + LANES]
        ss = jnp.sum(qc * qc, axis=-1, keepdims=True) * (1.0 / HEAD_DIM)
        qn = qc * lax.rsqrt(ss + EPS) * qg_ref[:, hd * LANES:(hd + 1) * LANES]
        qr = qn * qcos + _swap_halves(qn, HEAD_DIM // 2) * qsin
        q_ref[:, hd * LANES:(hd + 1) * LANES] = (qr * (HEAD_DIM ** -0.5)).astype(BF16)

    kc = proj[:, POOL_WIDTH + QPAD:POOL_WIDTH + QPAD + LANES]
    ksq = kc * kc
    s0 = jnp.sum(jnp.where(lane < HEAD_DIM, ksq, 0.0), axis=-1, keepdims=True)
    s1 = jnp.sum(jnp.where(lane < HEAD_DIM, 0.0, ksq), axis=-1, keepdims=True)
    kss = jnp.where(lane < HEAD_DIM, s0, s1) * (1.0 / HEAD_DIM)
    kn = kc * lax.rsqrt(kss + EPS) * kg_ref[...]
    k_ref[...] = (kn * qcos + _swap_halves(kn, HEAD_DIM // 2) * qsin).astype(BF16)

    v_ref[...] = proj[:, POOL_WIDTH + QPAD + LANES:].astype(BF16)

    whi = whi_ref[...]
    idxp = _dot(h_hi, whi) + _dot(h_lo, whi) + _dot(h_hi, wlo_ref[...])
    for c in range(IDXW // LANES):
        sl = slice(c * LANES, (c + 1) * LANES)
        ic = idxp[:, sl]
        tail = ic * icos_ref[:, sl] + _swap_halves(ic, IDX_DIM // 2) * isin_ref[:, sl]
        idx_ref[:, sl] = tail

    t_hi, t_lo = _split_bf16(tail)
    mid = jnp.logical_and(lane >= IDX_DIM, lane < 2 * IDX_DIM)
    cat = jnp.where(mid, t_lo, t_hi)
    kcat_ref[...] = jnp.where(lane < WI_LANE, cat, 0.0).astype(BF16)


def _inproj(x2, g1, wmain, whi, wlo, qg, kg, qcos, qsin, icos, isin, seq):
    n = x2.shape[0]
    rows = INPROJ_ROWS
    tper = seq // rows
    row = lambda w: pl.BlockSpec((rows, w), lambda i: (i, 0))
    const = lambda a: pl.BlockSpec(a.shape, lambda i: (0, 0))
    table = lambda w: pl.BlockSpec((rows, w), lambda i: (i % tper, 0))
    return pl.pallas_call(
        _inproj_kernel,
        grid=(n // rows,),
        in_specs=[row(D_MODEL), const(g1), const(wmain), const(whi), const(wlo), const(qg), const(kg),
                  table(LANES), table(LANES), table(IDXW), table(IDXW)],
        out_specs=[row(POOL_WIDTH), row(QPAD), row(LANES), row(LANES), row(IDXW), row(LANES)],
        out_shape=[jax.ShapeDtypeStruct((n, POOL_WIDTH), F32),
                   jax.ShapeDtypeStruct((n, QPAD), BF16),
                   jax.ShapeDtypeStruct((n, LANES), BF16),
                   jax.ShapeDtypeStruct((n, LANES), BF16),
                   jax.ShapeDtypeStruct((n, IDXW), F32),
                   jax.ShapeDtypeStruct((n, LANES), BF16)],
        compiler_params=pltpu.CompilerParams(dimension_semantics=("arbitrary",),
                                             vmem_limit_bytes=48 * 1024 * 1024),
        name="inproj",
    )(x2, g1, wmain, whi, wlo, qg, kg, qcos, qsin, icos, isin)


def _key_to_float(key):
    bits = key ^ ((key >> 31) & jnp.int32(0x7FFFFFFF))
    return lax.bitcast_convert_type(bits, F32)


def _fold_groups(x, op, ways=4):
    n = x.shape[0] // SUBLANES
    accs = [x[g * SUBLANES:(g + 1) * SUBLANES, :] for g in range(ways)]
    for g in range(ways, n):
        accs[g % ways] = op(accs[g % ways], x[g * SUBLANES:(g + 1) * SUBLANES, :])
    while len(accs) > 1:
        accs = [op(accs[i], accs[i + 1]) for i in range(0, len(accs), 2)]
    return accs[0]


def _fold_sublanes(x, op):
    for shift in (4, 2, 1):
        x = op(x, pltpu.roll(x, shift, 0))
    return x


def _dsa_kernel(q_ref, idxq_ref, kcat_ref, k_ref, vt_ref, out_ref, sc_ref, qcat_ref, acc_ref, p_ref, lg_ref,
                *, seq, topk):
    rows = DSA_ROWS
    ck = DSA_ROWS
    ngrp = ck // SUBLANES
    qb = pl.program_id(1)
    nch = qb + 1
    lane = lax.broadcasted_iota(jnp.int32, (rows, LANES), 1)
    qidx = qb * rows + lax.broadcasted_iota(jnp.int32, (ck, rows), 1)
    krow = lax.broadcasted_iota(jnp.int32, (ck, rows), 0)

    for c in range(IDX_HEADS * IDX_DIM // LANES):
        hi, lo = _split_bf16(idxq_ref[:, c * LANES:(c + 1) * LANES])
        for r in range(LANES // IDX_DIM):
            s0 = (LANES - IDX_DIM * r) % LANES
            a = pltpu.roll(hi, s0, 1) if s0 else hi
            b = pltpu.roll(hi, (s0 + IDX_DIM) % LANES, 1)
            d = pltpu.roll(lo, (s0 + 2 * IDX_DIM) % LANES, 1)
            cat = jnp.where(lane < IDX_DIM, a,
                            jnp.where(lane < 2 * IDX_DIM, b, jnp.where(lane < 3 * IDX_DIM, d, 0.0)))
            qcat_ref[c * (LANES // IDX_DIM) + r] = cat.astype(BF16)
    wt = idxq_ref[:, 2 * LANES:].T[WI_LANE:WI_LANE + IDX_HEADS, :]

    def score_body(j, carry):
        kc = kcat_ref[pl.ds(pl.multiple_of(j * ck, ck), ck), :]
        acc = jnp.zeros((ck, rows), F32)
        for hd in range(IDX_HEADS):
            y = _dot_nt(kc, qcat_ref[hd])
            acc = acc + wt[hd:hd + 1, :] * jnp.maximum(y, 0.0)
        sc_ref[j] = jnp.where(j * ck + krow <= qidx, acc, NEG)
        return carry

    lax.fori_loop(0, nch, score_body, 0)

    n_virtual = (seq - nch * ck).astype(F32)

    def count(pred, thr):
        def body(j, accs):
            accs = list(accs)
            for g in range(ngrp):
                hit = pred(sc_ref[j, g * SUBLANES:(g + 1) * SUBLANES, :], thr)
                accs[g % len(accs)] = accs[g % len(accs)] + jnp.where(hit, 1.0, 0.0)
            return tuple(accs)
        zero = jnp.zeros((SUBLANES, rows), F32)
        accs = lax.fori_loop(0, nch, body, (zero, zero, zero, zero))
        tot = _fold_sublanes((accs[0] + accs[1]) + (accs[2] + accs[3]), jnp.add)
        return tot + jnp.where(pred(jnp.float32(NEG), thr), n_virtual, 0.0)

    ge = lambda a, b: a >= b
    gt = lambda a, b: a > b
    kf = jnp.float32(topk)

    c0 = count(ge, jnp.zeros((SUBLANES, rows), F32))
    key = jnp.where(c0 >= kf, jnp.int32(0), jnp.int32(-2 ** 31))

    def bit_body(i, key):
        cand = key | jnp.left_shift(jnp.int32(1), 30 - i)
        cnt = count(ge, _key_to_float(cand))
        return jnp.where(cnt >= kf, cand, key)

    key = lax.fori_loop(0, 31, bit_body, key)
    thr8 = _key_to_float(key)
    thr = thr8[0:1, :]
    need = (kf - count(gt, thr8))[0:1, :]

    ki = lax.broadcasted_iota(jnp.int32, (ck, ck), 0)
    kj = lax.broadcasted_iota(jnp.int32, (ck, ck), 1)
    lower = jnp.where(kj < ki, 1.0, 0.0).astype(BF16)
    ones8 = jnp.ones((SUBLANES, ck), BF16)

    def select_body(j, run):
        sc = sc_ref[j]
        eq = sc == thr
        eqb = jnp.where(eq, 1.0, 0.0).astype(BF16)
        prior = run[0:1, :] + _dot(lower, eqb)
        tie = jnp.where(prior < need, 0.0, NEG)
        bias = jnp.where(sc > thr, 0.0, jnp.where(eq, tie, NEG))
        sc_ref[j] = jnp.where(j * ck + krow <= qidx, bias, NEG)
        return run + _dot(ones8, eqb)

    lax.fori_loop(0, nch, select_body, jnp.zeros((SUBLANES, rows), F32))

    def max_body(j, ms):
        kc = k_ref[pl.ds(pl.multiple_of(j * ck, ck), ck), :]
        bias = sc_ref[j]
        out = []
        for hd in range(N_Q_HEADS):
            s = _dot_nt(kc, q_ref[:, hd * LANES:(hd + 1) * LANES]) + bias
            lg_ref[hd, j] = s
            out.append(jnp.maximum(ms[hd], _fold_groups(s, jnp.maximum)))
        return tuple(out)

    ms = lax.fori_loop(0, nch, max_body,
                       tuple(jnp.full((SUBLANES, rows), -jnp.inf, F32) for _ in range(N_Q_HEADS)))
    ms = [_fold_sublanes(m, jnp.maximum)[0:1, :] for m in ms]

    acc_ref[...] = jnp.zeros(acc_ref.shape, F32)

    def pv_body(j, ls):
        out = []
        for hd in range(N_Q_HEADS):
            p = jnp.exp(lg_ref[hd, j] - ms[hd])
            out.append(ls[hd] + _fold_groups(p, jnp.add))
            p_ref[hd] = p.astype(BF16)
        vt = vt_ref[j]
        for hd in range(N_Q_HEADS):
            acc_ref[hd] += _dot(vt, p_ref[hd])
        return tuple(out)

    ls = lax.fori_loop(0, nch, pv_body, tuple(jnp.zeros((SUBLANES, rows), F32) for _ in range(N_Q_HEADS)))

    outs = []
    for hd in range(N_Q_HEADS):
        l = _fold_sublanes(ls[hd], jnp.add)
        r0 = HEAD_DIM * (hd // Q_PER_KV)
        outs.append(acc_ref[hd, r0:r0 + HEAD_DIM, :] / l[0:1, :])
    out_ref[...] = jnp.concatenate(outs, axis=0).T.astype(BF16)


def _dsa(q, idx, kcat, k, v, batch, seq):
    rows = DSA_ROWS
    nq = seq // rows
    topk = min(TOPK_MAX, seq // 4)
    assert topk <= rows and seq % rows == 0
    q3 = q.reshape(batch, seq, QPAD)
    idx3 = idx.reshape(batch, seq, IDXW)
    vt = v.reshape(batch, nq, rows, LANES).transpose(0, 1, 3, 2)
    full = pl.BlockSpec((None, seq, LANES), lambda b, i: (b, 0, 0))
    kern = functools.partial(_dsa_kernel, seq=seq, topk=topk)
    return pl.pallas_call(
        kern,
        grid=(batch, nq),
        in_specs=[pl.BlockSpec((None, rows, QPAD), lambda b, i: (b, i, 0)),
                  pl.BlockSpec((None, rows, IDXW), lambda b, i: (b, i, 0)),
                  full, full,
                  pl.BlockSpec((None, nq, LANES, rows), lambda b, i: (b, 0, 0, 0))],
        out_specs=pl.BlockSpec((None, rows, N_Q_HEADS * HEAD_DIM), lambda b, i: (b, i, 0)),
        out_shape=jax.ShapeDtypeStruct((batch, seq, N_Q_HEADS * HEAD_DIM), BF16),
        scratch_shapes=[pltpu.VMEM((nq, rows, rows), F32),
                        pltpu.VMEM((IDX_HEADS, rows, LANES), BF16),
                        pltpu.VMEM((N_Q_HEADS, LANES, rows), F32),
                        pltpu.VMEM((N_Q_HEADS, rows, rows), BF16),
                        pltpu.VMEM((N_Q_HEADS, nq, rows, rows), F32)],
        compiler_params=pltpu.CompilerParams(dimension_semantics=("arbitrary", "arbitrary"),
                                             vmem_limit_bytes=48 * 1024 * 1024),
        name="dsa",
    )(q3, idx3, kcat.reshape(batch, seq, LANES), k.reshape(batch, seq, LANES), vt)


def _mixout_kernel(pool_ref, prev_ref, b_ref, x_ref, pw_ref, ps_ref, woa_ref, wob_ref, out_ref):
    rows = MIX_ROWS
    ti = pl.program_id(1)
    prev = jnp.where(ti == 0, 0.0, prev_ref[...])
    ext = jnp.concatenate([prev, pool_ref[...]], axis=0)
    t = ti * rows + lax.broadcasted_iota(jnp.int32, (rows, 1), 0)
    parts = []
    for g, w in enumerate(POOL_WINDOWS):
        sl = slice(g * POOL_GROUP, (g + 1) * POOL_GROUP)
        a = ext[:, sl]
        span = 1
        while span < w:
            a = a + pltpu.roll(a, span, 0)
            span *= 2
        cnt = jnp.minimum(t + 1, w).astype(F32)
        p = a[POOL_HALO:] / cnt - ext[POOL_HALO:, sl]
        y = _dot(p.astype(BF16), pw_ref[g])
        parts.append(y * ps_ref[:, sl])
    a_out = jnp.concatenate(parts, axis=1).astype(BF16)
    out_ref[...] = x_ref[...] + _dot(a_out, woa_ref[...]) + _dot(b_ref[...], wob_ref[...])


def _mixout(pool, b_out, x2, pw, ps, woa, wob, batch, seq):
    rows = MIX_ROWS
    per = seq // rows
    hb = rows // POOL_HALO
    tile = lambda w: pl.BlockSpec((None, rows, w), lambda b, i: (b, i, 0))
    const = lambda a: pl.BlockSpec(a.shape, lambda b, i: (0,) * a.ndim)
    return pl.pallas_call(
        _mixout_kernel,
        grid=(batch, per),
        in_specs=[tile(POOL_WIDTH),
                  pl.BlockSpec((None, POOL_HALO, POOL_WIDTH), lambda b, i: (b, jnp.maximum(i * hb - 1, 0), 0)),
                  tile(POOL_WIDTH), tile(D_MODEL), const(pw), const(ps), const(woa), const(wob)],
        out_specs=tile(D_MODEL),
        out_shape=jax.ShapeDtypeStruct((batch, seq, D_MODEL), F32),
        compiler_params=pltpu.CompilerParams(dimension_semantics=("arbitrary", "arbitrary"),
                                             vmem_limit_bytes=48 * 1024 * 1024),
        name="mixout",
    )(pool.reshape(batch, seq, POOL_WIDTH), pool.reshape(batch, seq, POOL_WIDTH), b_out,
      x2.reshape(batch, seq, D_MODEL), pw, ps, woa, wob)


def _ffn_kernel(x_ref, prev_ref, g2_ref, wup_ref, cw_ref, cb_ref, wdn_ref, out_ref, act_ref):
    rows = FFN_ROWS
    ti = pl.program_id(1)
    g2 = g2_ref[...]

    def norm(v):
        return v * lax.rsqrt(jnp.mean(v * v, axis=-1, keepdims=True) + EPS) * g2

    x = x_ref[...]
    hp = jnp.where(ti == 0, 0.0, norm(prev_ref[...]))
    h_ext = jnp.concatenate([hp, norm(x)], axis=0).astype(BF16)

    chunk = 2 * LANES
    for c in range(D_FF // chunk):
        halves = []
        for off in (0, D_FF):
            sl = slice(off + c * chunk, off + (c + 1) * chunk)
            u = _dot(h_ext, wup_ref[:, sl])
            u1 = pltpu.roll(u, 1, 0)
            u2 = pltpu.roll(u, 2, 0)
            cv = cb_ref[:, sl] + u2[FFN_HALO:] * cw_ref[0:1, sl]
            cv = cv + u1[FFN_HALO:] * cw_ref[1:2, sl]
            cv = cv + u[FFN_HALO:] * cw_ref[2:3, sl]
            halves.append(cv)
        gate, val = halves
        act = gate * (1.0 / (1.0 + jnp.exp(-gate))) * val
        act_ref[:, c * chunk:(c + 1) * chunk] = act.astype(BF16)
    out_ref[...] = x + _dot(act_ref[...], wdn_ref[...])


def _ffn(x1, g2, wup, cw, cb, wdn, batch, seq):
    rows = FFN_ROWS
    per = seq // rows
    hb = rows // FFN_HALO
    tile = pl.BlockSpec((None, rows, D_MODEL), lambda b, i: (b, i, 0))
    const = lambda a: pl.BlockSpec(a.shape, lambda b, i: (0,) * a.ndim, pipeline_mode=pl.Buffered(1))
    return pl.pallas_call(
        _ffn_kernel,
        grid=(batch, per),
        in_specs=[tile,
                  pl.BlockSpec((None, FFN_HALO, D_MODEL), lambda b, i: (b, jnp.maximum(i * hb - 1, 0), 0)),
                  const(g2), const(wup), const(cw), const(cb), const(wdn)],
        out_specs=tile,
        out_shape=jax.ShapeDtypeStruct((batch, seq, D_MODEL), F32),
        scratch_shapes=[pltpu.VMEM((rows, D_FF), BF16)],
        compiler_params=pltpu.CompilerParams(dimension_semantics=("arbitrary", "arbitrary"),
                                             vmem_limit_bytes=56 * 1024 * 1024),
        name="ffn",
    )(x1, x1, g2, wup, cw, cb, wdn)


def _rope_tables(seq, dim):
    half = dim // 2
    inv = jnp.exp(-jnp.log(jnp.float32(ROPE_THETA)) * jnp.arange(half, dtype=F32) / half)
    ang = jnp.arange(seq).astype(F32)[:, None] * inv[None, :]
    cos, sin = jnp.cos(ang), jnp.sin(ang)
    return jnp.concatenate([cos, cos], axis=1), jnp.concatenate([-sin, sin], axis=1)


def _layer(x, norm1_g, w_in, q_norm_g, k_norm_g, pool_w, pool_scale, w_out, norm2_g, w_up, conv_w, conv_b, w_down):
    batch, seq, _ = x.shape
    n = batch * seq
    cuts = [POOL_WIDTH, POOL_WIDTH + N_Q_HEADS * HEAD_DIM]
    cuts += [cuts[-1] + N_KV_HEADS * HEAD_DIM]
    cuts += [cuts[-1] + N_KV_HEADS * HEAD_DIM]
    cuts += [cuts[-1] + IDX_HEADS * IDX_DIM]
    cuts += [cuts[-1] + IDX_DIM]
    w_pool, w_q, w_k, w_v, w_qi, w_ki, w_wi = jnp.split(w_in, cuts, axis=1)

    zq = jnp.zeros((D_MODEL, HEAD_DIM), F32)
    qcols, qgains = [], []
    for hd in range(N_Q_HEADS):
        blk = w_q[:, hd * HEAD_DIM:(hd + 1) * HEAD_DIM]
        first = hd // Q_PER_KV == 0
        qcols += [blk, zq] if first else [zq, blk]
        qgains += [q_norm_g, jnp.zeros_like(q_norm_g)] if first else [jnp.zeros_like(q_norm_g), q_norm_g]
    wmain = jnp.concatenate([w_pool] + qcols + [w_k, w_v], axis=1).astype(BF16)
    qg = jnp.concatenate(qgains)[None, :]
    kg = jnp.concatenate([k_norm_g, k_norm_g])[None, :]
    w_idx = jnp.concatenate([w_qi, w_ki, w_ki, w_ki, w_wi,
                             jnp.zeros((D_MODEL, IDXW - WI_LANE - IDX_HEADS - IDX_HEADS * IDX_DIM), F32)], axis=1)
    whi = w_idx.astype(BF16)
    wlo = (w_idx - whi.astype(F32)).astype(BF16)

    c64, s64 = _rope_tables(seq, HEAD_DIM)
    qcos, qsin = jnp.tile(c64, (1, LANES // HEAD_DIM)), jnp.tile(s64, (1, LANES // HEAD_DIM))
    c32, s32 = _rope_tables(seq, IDX_DIM)
    nrot = (IDX_HEADS + 3) * IDX_DIM
    wscale = jnp.full((seq, IDX_HEADS), IDX_HEADS ** -0.5 * IDX_DIM ** -0.5, F32)
    icos = jnp.concatenate([jnp.tile(c32, (1, nrot // IDX_DIM)), wscale,
                            jnp.zeros((seq, IDXW - nrot - IDX_HEADS), F32)], axis=1)
    isin = jnp.concatenate([jnp.tile(s32, (1, nrot // IDX_DIM)), jnp.zeros((seq, IDXW - nrot), F32)], axis=1)

    x2 = x.reshape(n, D_MODEL)
    pool, q, k, v, idx, kcat = _inproj(x2, norm1_g[None, :], wmain, whi, wlo, qg, kg, qcos, qsin, icos, isin, seq)
    b_out = _dsa(q, idx, kcat, k, v, batch, seq)
    x1 = _mixout(pool, b_out, x2, pool_w.astype(BF16), pool_scale[None, :],
                 w_out[:POOL_WIDTH].astype(BF16), w_out[POOL_WIDTH:].astype(BF16), batch, seq)
    return _ffn(x1, norm2_g[None, :], w_up.astype(BF16), conv_w, conv_b[None, :], w_down.astype(BF16), batch, seq)


def kernel(x, norm1_g, w_in, q_norm_g, k_norm_g, pool_w, pool_scale, w_out, norm2_g, w_up, conv_w, conv_b, w_down):
    for l in range(norm1_g.shape[0]):
        x = _layer(x, norm1_g[l], w_in[l], q_norm_g[l], k_norm_g[l], pool_w[l], pool_scale[l], w_out[l],
                   norm2_g[l], w_up[l], conv_w[l], conv_b[l], w_down[l])
    return x
```

```python
import functools

import jax
import jax.numpy as jnp
from jax import lax
from jax.experimental import pallas as pl
from jax.experimental.pallas import tpu as pltpu

F32 = jnp.float32
BF16 = jnp.bfloat16

D_MODEL = 1024
POOL_WIDTH = 512
POOL_WINDOWS = (2, 4, 8, 16)
POOL_GROUP = 128
HEAD_DIM = 64
N_Q_HEADS = 8
N_KV_HEADS = 2
Q_PER_KV = 4
IDX_HEADS = 8
IDX_DIM = 32
TOPK_MAX = 256
ROPE_THETA = 10000.0
D_FF = 2816
CONV_WIDTH = 3
EPS = 1e-6
NEG = -1e30

LANES = 128
SUBLANES = 8
POOL_HALO = 16
QPAD = N_Q_HEADS * LANES
IDXW = 3 * LANES
WI_LANE = 3 * IDX_DIM
MAIN_W = POOL_WIDTH + (N_Q_HEADS + 2 * N_KV_HEADS) * HEAD_DIM

INPROJ_ROWS = 512
DSA_ROWS = 256
MIX_ROWS = 512
FFN_ROWS = 512
FFN_HALO = SUBLANES

NT_DIMS = (((1,), (1,)), ((), ()))


def _dot(a, b):
    return jnp.dot(a, b, preferred_element_type=F32)


def _dot_nt(a, b):
    return lax.dot_general(a, b, NT_DIMS, preferred_element_type=F32)


def _split_bf16(x):
    hi = x.astype(BF16).astype(F32)
    return hi, x - hi


def _swap_halves(xc, half):
    lane = lax.broadcasted_iota(jnp.int32, xc.shape, 1)
    first = (lane % (2 * half)) < half
    return jnp.where(first, pltpu.roll(xc, LANES - half, 1), pltpu.roll(xc, half, 1))


def _inproj_kernel(x_ref, g1_ref, wmain_ref, whi_ref, wlo_ref, qg_ref, kg_ref,
                   qcos_ref, qsin_ref, icos_ref, isin_ref,
                   pool_ref, q_ref, k_ref, v_ref, idx_ref, kcat_ref):
    x = x_ref[...]
    ms = jnp.mean(x * x, axis=-1, keepdims=True)
    h = x * lax.rsqrt(ms + EPS) * g1_ref[...]
    h_hi = h.astype(BF16)
    h_lo = (h - h_hi.astype(F32)).astype(BF16)

    proj = _dot(h_hi, wmain_ref[...])
    pool_ref[...] = proj[:, :POOL_WIDTH]

    qcos = qcos_ref[...]
    qsin = qsin_ref[...]
    lane = lax.broadcasted_iota(jnp.int32, (x.shape[0], LANES), 1)

    first = lane < HEAD_DIM

    def norm_rope(xc, gain):
        sq = xc * xc
        s0 = jnp.sum(jnp.where(first, sq, 0.0), axis=-1, keepdims=True)
        s1 = jnp.sum(jnp.where(first, 0.0, sq), axis=-1, keepdims=True)
        ss = jnp.where(first, s0, s1) * (1.0 / HEAD_DIM)
        xn = xc * lax.rsqrt(ss + EPS) * gain
        return xn * qcos + _swap_halves(xn, HEAD_DIM // 2) * qsin

    n_qcols = N_Q_HEADS * HEAD_DIM // LANES
    for c in range(n_qcols):
        c0 = POOL_WIDTH + c * LANES
        qr = norm_rope(proj[:, c0:c0 + LANES], qg_ref[...]) * (HEAD_DIM ** -0.5)
        qs = pltpu.roll(qr, HEAD_DIM, 1)
        if (2 * c) // Q_PER_KV == 0:
            pair = (jnp.where(first, qr, 0.0), jnp.where(first, qs, 0.0))
        else:
            pair = (jnp.where(first, 0.0, qs), jnp.where(first, 0.0, qr))
        for i, qh in enumerate(pair):
            q_ref[:, (2 * c + i) * LANES:(2 * c + i + 1) * LANES] = qh.astype(BF16)

    k0 = POOL_WIDTH + n_qcols * LANES
    k_ref[...] = norm_rope(proj[:, k0:k0 + LANES], kg_ref[...]).astype(BF16)
    v_ref[...] = proj[:, k0 + LANES:].astype(BF16)

    whi = whi_ref[...]
    idxp = _dot(h_hi, whi) + _dot(h_lo, whi) + _dot(h_hi, wlo_ref[...])
    for c in range(IDXW // LANES):
        sl = slice(c * LANES, (c + 1) * LANES)
        ic = idxp[:, sl]
        tail = ic * icos_ref[:, sl] + _swap_halves(ic, IDX_DIM // 2) * isin_ref[:, sl]
        idx_ref[:, sl] = tail

    t_hi, t_lo = _split_bf16(tail)
    mid = jnp.logical_and(lane >= IDX_DIM, lane < 2 * IDX_DIM)
    cat = jnp.where(mid, t_lo, t_hi)
    kcat_ref[...] = jnp.where(lane < WI_LANE, cat, 0.0).astype(BF16)


def _inproj(x2, g1, wmain, whi, wlo, qg, kg, qcos, qsin, icos, isin, seq):
    n = x2.shape[0]
    rows = INPROJ_ROWS
    tper = seq // rows
    row = lambda w: pl.BlockSpec((rows, w), lambda i: (i, 0))
    const = lambda a: pl.BlockSpec(a.shape, lambda i: (0, 0))
    table = lambda w: pl.BlockSpec((rows, w), lambda i: (i % tper, 0))
    return pl.pallas_call(
        _inproj_kernel,
        grid=(n // rows,),
        in_specs=[row(D_MODEL), const(g1), const(wmain), const(whi), const(wlo), const(qg), const(kg),
                  table(LANES), table(LANES), table(IDXW), table(IDXW)],
        out_specs=[row(POOL_WIDTH), row(QPAD), row(LANES), row(LANES), row(IDXW), row(LANES)],
        out_shape=[jax.ShapeDtypeStruct((n, POOL_WIDTH), F32),
                   jax.ShapeDtypeStruct((n, QPAD), BF16),
                   jax.ShapeDtypeStruct((n, LANES), BF16),
                   jax.ShapeDtypeStruct((n, LANES), BF16),
                   jax.ShapeDtypeStruct((n, IDXW), F32),
                   jax.ShapeDtypeStruct((n, LANES), BF16)],
        compiler_params=pltpu.CompilerParams(dimension_semantics=("arbitrary",),
                                             vmem_limit_bytes=48 * 1024 * 1024),
        name="inproj",
    )(x2, g1, wmain, whi, wlo, qg, kg, qcos, qsin, icos, isin)


def _key_to_float(key):
    bits = key ^ ((key >> 31) & jnp.int32(0x7FFFFFFF))
    return lax.bitcast_convert_type(bits, F32)


def _fold_groups(x, op, ways=4):
    n = x.shape[0] // SUBLANES
    accs = [x[g * SUBLANES:(g + 1) * SUBLANES, :] for g in range(ways)]
    for g in range(ways, n):
        accs[g % ways] = op(accs[g % ways], x[g * SUBLANES:(g + 1) * SUBLANES, :])
    while len(accs) > 1:
        accs = [op(accs[i], accs[i + 1]) for i in range(0, len(accs), 2)]
    return accs[0]


def _fold_sublanes(x, op):
    for shift in (4, 2, 1):
        x = op(x, pltpu.roll(x, shift, 0))
    return x


def _dsa_kernel(q_ref, idxq_ref, kcat_ref, k_ref, vt_ref, out_ref, sc_ref, qcat_ref, acc_ref, p_ref, lg_ref,
                *, seq, topk):
    rows = DSA_ROWS
    ck = DSA_ROWS
    ngrp = ck // SUBLANES
    qb = pl.program_id(1)
    nch = qb + 1
    lane = lax.broadcasted_iota(jnp.int32, (rows, LANES), 1)
    qidx = qb * rows + lax.broadcasted_iota(jnp.int32, (ck, rows), 1)
    krow = lax.broadcasted_iota(jnp.int32, (ck, rows), 0)

    for c in range(IDX_HEADS * IDX_DIM // LANES):
        hi, lo = _split_bf16(idxq_ref[:, c * LANES:(c + 1) * LANES])
        for r in range(LANES // IDX_DIM):
            s0 = (LANES - IDX_DIM * r) % LANES
            a = pltpu.roll(hi, s0, 1) if s0 else hi
            b = pltpu.roll(hi, (s0 + IDX_DIM) % LANES, 1)
            d = pltpu.roll(lo, (s0 + 2 * IDX_DIM) % LANES, 1)
            cat = jnp.where(lane < IDX_DIM, a,
                            jnp.where(lane < 2 * IDX_DIM, b, jnp.where(lane < 3 * IDX_DIM, d, 0.0)))
            qcat_ref[c * (LANES // IDX_DIM) + r] = cat.astype(BF16)
    wt = idxq_ref[:, 2 * LANES:].T[WI_LANE:WI_LANE + IDX_HEADS, :]

    def score_body(j, carry):
        kc = kcat_ref[pl.ds(pl.multiple_of(j * ck, ck), ck), :]
        acc = jnp.zeros((ck, rows), F32)
        for hd in range(IDX_HEADS):
            y = _dot_nt(kc, qcat_ref[hd])
            acc = acc + wt[hd:hd + 1, :] * jnp.maximum(y, 0.0)
        sc_ref[j] = jnp.where(j * ck + krow <= qidx, acc, NEG)
        return carry

    lax.fori_loop(0, nch, score_body, 0)

    n_virtual = (seq - nch * ck).astype(F32)

    def count(pred, thr):
        def body(j, accs):
            accs = list(accs)
            for g in range(ngrp):
                hit = pred(sc_ref[j, g * SUBLANES:(g + 1) * SUBLANES, :], thr)
                accs[g % len(accs)] = accs[g % len(accs)] + jnp.where(hit, 1.0, 0.0)
            return tuple(accs)
        zero = jnp.zeros((SUBLANES, rows), F32)
        accs = lax.fori_loop(0, nch, body, (zero, zero, zero, zero))
        tot = _fold_sublanes((accs[0] + accs[1]) + (accs[2] + accs[3]), jnp.add)
        return tot + jnp.where(pred(jnp.float32(NEG), thr), n_virtual, 0.0)

    ge = lambda a, b: a >= b
    gt = lambda a, b: a > b
    kf = jnp.float32(topk)

    c0 = count(ge, jnp.zeros((SUBLANES, rows), F32))
    key = jnp.where(c0 >= kf, jnp.int32(0), jnp.int32(-2 ** 31))

    def bit_body(i, key):
        cand = key | jnp.left_shift(jnp.int32(1), 30 - i)
        cnt = count(ge, _key_to_float(cand))
        return jnp.where(cnt >= kf, cand, key)

    key = lax.fori_loop(0, 31, bit_body, key)
    thr8 = _key_to_float(key)
    thr = thr8[0:1, :]
    need = (kf - count(gt, thr8))[0:1, :]

    ki = lax.broadcasted_iota(jnp.int32, (ck, ck), 0)
    kj = lax.broadcasted_iota(jnp.int32, (ck, ck), 1)
    lower = jnp.where(kj < ki, 1.0, 0.0).astype(BF16)
    ones8 = jnp.ones((SUBLANES, ck), BF16)

    def select_body(j, run):
        sc = sc_ref[j]
        eq = sc == thr
        eqb = jnp.where(eq, 1.0, 0.0).astype(BF16)
        prior = run[0:1, :] + _dot(lower, eqb)
        tie = jnp.where(prior < need, 0.0, NEG)
        bias = jnp.where(sc > thr, 0.0, jnp.where(eq, tie, NEG))
        sc_ref[j] = jnp.where(j * ck + krow <= qidx, bias, NEG)
        return run + _dot(ones8, eqb)

    lax.fori_loop(0, nch, select_body, jnp.zeros((SUBLANES, rows), F32))

    def max_body(j, ms):
        kc = k_ref[pl.ds(pl.multiple_of(j * ck, ck), ck), :]
        bias = sc_ref[j]
        out = []
        for hd in range(N_Q_HEADS):
            s = _dot_nt(kc, q_ref[:, hd * LANES:(hd + 1) * LANES]) + bias
            lg_ref[hd, j] = s
            out.append(jnp.maximum(ms[hd], _fold_groups(s, jnp.maximum)))
        return tuple(out)

    ms = lax.fori_loop(0, nch, max_body,
                       tuple(jnp.full((SUBLANES, rows), -jnp.inf, F32) for _ in range(N_Q_HEADS)))
    ms = [_fold_sublanes(m, jnp.maximum)[0:1, :] for m in ms]

    acc_ref[...] = jnp.zeros(acc_ref.shape, F32)

    def pv_body(j, ls):
        out = []
        for hd in range(N_Q_HEADS):
            p = jnp.exp(lg_ref[hd, j] - ms[hd])
            out.append(ls[hd] + _fold_groups(p, jnp.add))
            p_ref[hd] = p.astype(BF16)
        vt = vt_ref[j]
        for hd in range(N_Q_HEADS):
            acc_ref[hd] += _dot(vt, p_ref[hd])
        return tuple(out)

    ls = lax.fori_loop(0, nch, pv_body, tuple(jnp.zeros((SUBLANES, rows), F32) for _ in range(N_Q_HEADS)))

    outs = []
    for hd in range(N_Q_HEADS):
        l = _fold_sublanes(ls[hd], jnp.add)
        r0 = HEAD_DIM * (hd // Q_PER_KV)
        outs.append(acc_ref[hd, r0:r0 + HEAD_DIM, :] / l[0:1, :])
    out_ref[...] = jnp.concatenate(outs, axis=0).T.astype(BF16)


def _dsa(q, idx, kcat, k, v, batch, seq):
    rows = DSA_ROWS
    nq = seq // rows
    topk = min(TOPK_MAX, seq // 4)
    assert topk <= rows and seq % rows == 0
    q3 = q.reshape(batch, seq, QPAD)
    idx3 = idx.reshape(batch, seq, IDXW)
    vt = v.reshape(batch, nq, rows, LANES).transpose(0, 1, 3, 2)
    full = pl.BlockSpec((None, seq, LANES), lambda b, i: (b, 0, 0))
    kern = functools.partial(_dsa_kernel, seq=seq, topk=topk)
    return pl.pallas_call(
        kern,
        grid=(batch, nq),
        in_specs=[pl.BlockSpec((None, rows, QPAD), lambda b, i: (b, i, 0)),
                  pl.BlockSpec((None, rows, IDXW), lambda b, i: (b, i, 0)),
                  full, full,
                  pl.BlockSpec((None, nq, LANES, rows), lambda b, i: (b, 0, 0, 0))],
        out_specs=pl.BlockSpec((None, rows, N_Q_HEADS * HEAD_DIM), lambda b, i: (b, i, 0)),
        out_shape=jax.ShapeDtypeStruct((batch, seq, N_Q_HEADS * HEAD_DIM), BF16),
        scratch_shapes=[pltpu.VMEM((nq, rows, rows), F32),
                        pltpu.VMEM((IDX_HEADS, rows, LANES), BF16),
                        pltpu.VMEM((N_Q_HEADS, LANES, rows), F32),
                        pltpu.VMEM((N_Q_HEADS, rows, rows), BF16),
                        pltpu.VMEM((N_Q_HEADS, nq, rows, rows), F32)],
        compiler_params=pltpu.CompilerParams(dimension_semantics=("arbitrary", "arbitrary"),
                                             vmem_limit_bytes=48 * 1024 * 1024),
        name="dsa",
    )(q3, idx3, kcat.reshape(batch, seq, LANES), k.reshape(batch, seq, LANES), vt)


def _mixout_kernel(pool_ref, prev_ref, b_ref, x_ref, pw_ref, ps_ref, woa_ref, wob_ref, out_ref):
    rows = MIX_ROWS
    ti = pl.program_id(1)
    prev = jnp.where(ti == 0, 0.0, prev_ref[...])
    ext = jnp.concatenate([prev, pool_ref[...]], axis=0)
    t = ti * rows + lax.broadcasted_iota(jnp.int32, (rows, 1), 0)
    parts = []
    for g, w in enumerate(POOL_WINDOWS):
        sl = slice(g * POOL_GROUP, (g + 1) * POOL_GROUP)
        a = ext[:, sl]
        span = 1
        while span < w:
            a = a + pltpu.roll(a, span, 0)
            span *= 2
        cnt = jnp.minimum(t + 1, w).astype(F32)
        p = a[POOL_HALO:] / cnt - ext[POOL_HALO:, sl]
        y = _dot(p.astype(BF16), pw_ref[g])
        parts.append(y * ps_ref[:, sl])
    a_out = jnp.concatenate(parts, axis=1).astype(BF16)
    out_ref[...] = x_ref[...] + _dot(a_out, woa_ref[...]) + _dot(b_ref[...], wob_ref[...])


def _mixout(pool, b_out, x2, pw, ps, woa, wob, batch, seq):
    rows = MIX_ROWS
    per = seq // rows
    hb = rows // POOL_HALO
    tile = lambda w: pl.BlockSpec((None, rows, w), lambda b, i: (b, i, 0))
    const = lambda a: pl.BlockSpec(a.shape, lambda b, i: (0,) * a.ndim)
    return pl.pallas_call(
        _mixout_kernel,
        grid=(batch, per),
        in_specs=[tile(POOL_WIDTH),
                  pl.BlockSpec((None, POOL_HALO, POOL_WIDTH), lambda b, i: (b, jnp.maximum(i * hb - 1, 0), 0)),
                  tile(POOL_WIDTH), tile(D_MODEL), const(pw), const(ps), const(woa), const(wob)],
        out_specs=tile(D_MODEL),
        out_shape=jax.ShapeDtypeStruct((batch, seq, D_MODEL), F32),
        compiler_params=pltpu.CompilerParams(dimension_semantics=("arbitrary", "arbitrary"),
                                             vmem_limit_bytes=48 * 1024 * 1024),
        name="mixout",
    )(pool.reshape(batch, seq, POOL_WIDTH), pool.reshape(batch, seq, POOL_WIDTH), b_out,
      x2.reshape(batch, seq, D_MODEL), pw, ps, woa, wob)


def _ffn_kernel(x_ref, prev_ref, g2_ref, wup_ref, cw_ref, cb_ref, wdn_ref, out_ref, act_ref):
    rows = FFN_ROWS
    ti = pl.program_id(1)
    g2 = g2_ref[...]

    def norm(v):
        return v * lax.rsqrt(jnp.mean(v * v, axis=-1, keepdims=True) + EPS) * g2

    x = x_ref[...]
    hp = jnp.where(ti == 0, 0.0, norm(prev_ref[...]))
    h_ext = jnp.concatenate([hp, norm(x)], axis=0).astype(BF16)

    chunk = 2 * LANES
    for c in range(D_FF // chunk):
        halves = []
        for off in (0, D_FF):
            sl = slice(off + c * chunk, off + (c + 1) * chunk)
            u = _dot(h_ext, wup_ref[:, sl])
            u1 = pltpu.roll(u, 1, 0)
            u2 = pltpu.roll(u, 2, 0)
            cv = cb_ref[:, sl] + u2[FFN_HALO:] * cw_ref[0:1, sl]
            cv = cv + u1[FFN_HALO:] * cw_ref[1:2, sl]
            cv = cv + u[FFN_HALO:] * cw_ref[2:3, sl]
            halves.append(cv)
        gate, val = halves
        act = gate * (1.0 / (1.0 + jnp.exp(-gate))) * val
        act_ref[:, c * chunk:(c + 1) * chunk] = act.astype(BF16)
    out_ref[...] = x + _dot(act_ref[...], wdn_ref[...])


def _ffn(x1, g2, wup, cw, cb, wdn, batch, seq):
    rows = FFN_ROWS
    per = seq // rows
    hb = rows // FFN_HALO
    tile = pl.BlockSpec((None, rows, D_MODEL), lambda b, i: (b, i, 0))
    const = lambda a: pl.BlockSpec(a.shape, lambda b, i: (0,) * a.ndim, pipeline_mode=pl.Buffered(1))
    return pl.pallas_call(
        _ffn_kernel,
        grid=(batch, per),
        in_specs=[tile,
                  pl.BlockSpec((None, FFN_HALO, D_MODEL), lambda b, i: (b, jnp.maximum(i * hb - 1, 0), 0)),
                  const(g2), const(wup), const(cw), const(cb), const(wdn)],
        out_specs=tile,
        out_shape=jax.ShapeDtypeStruct((batch, seq, D_MODEL), F32),
        scratch_shapes=[pltpu.VMEM((rows, D_FF), BF16)],
        compiler_params=pltpu.CompilerParams(dimension_semantics=("arbitrary", "arbitrary"),
                                             vmem_limit_bytes=56 * 1024 * 1024),
        name="ffn",
    )(x1, x1, g2, wup, cw, cb, wdn)


def _rope_tables(seq, dim):
    half = dim // 2
    inv = jnp.exp(-jnp.log(jnp.float32(ROPE_THETA)) * jnp.arange(half, dtype=F32) / half)
    ang = jnp.arange(seq).astype(F32)[:, None] * inv[None, :]
    cos, sin = jnp.cos(ang), jnp.sin(ang)
    return jnp.concatenate([cos, cos], axis=1), jnp.concatenate([-sin, sin], axis=1)


def _layer(x, norm1_g, w_in, q_norm_g, k_norm_g, pool_w, pool_scale, w_out, norm2_g, w_up, conv_w, conv_b, w_down):
    batch, seq, _ = x.shape
    n = batch * seq
    wmain = w_in[:, :MAIN_W].astype(BF16)
    w_qi, w_ki, w_wi = jnp.split(w_in[:, MAIN_W:], [IDX_HEADS * IDX_DIM, (IDX_HEADS + 1) * IDX_DIM], axis=1)
    qg = jnp.concatenate([q_norm_g, q_norm_g])[None, :]
    kg = jnp.concatenate([k_norm_g, k_norm_g])[None, :]
    w_idx = jnp.concatenate([w_qi, w_ki, w_ki, w_ki, w_wi,
                             jnp.zeros((D_MODEL, IDXW - WI_LANE - IDX_HEADS - IDX_HEADS * IDX_DIM), F32)], axis=1)
    whi = w_idx.astype(BF16)
    wlo = (w_idx - whi.astype(F32)).astype(BF16)

    c64, s64 = _rope_tables(seq, HEAD_DIM)
    qcos, qsin = jnp.tile(c64, (1, LANES // HEAD_DIM)), jnp.tile(s64, (1, LANES // HEAD_DIM))
    c32, s32 = _rope_tables(seq, IDX_DIM)
    nrot = (IDX_HEADS + 3) * IDX_DIM
    wscale = jnp.full((seq, IDX_HEADS), IDX_HEADS ** -0.5 * IDX_DIM ** -0.5, F32)
    icos = jnp.concatenate([jnp.tile(c32, (1, nrot // IDX_DIM)), wscale,
                            jnp.zeros((seq, IDXW - nrot - IDX_HEADS), F32)], axis=1)
    isin = jnp.concatenate([jnp.tile(s32, (1, nrot // IDX_DIM)), jnp.zeros((seq, IDXW - nrot), F32)], axis=1)

    x2 = x.reshape(n, D_MODEL)
    pool, q, k, v, idx, kcat = _inproj(x2, norm1_g[None, :], wmain, whi, wlo, qg, kg, qcos, qsin, icos, isin, seq)
    b_out = _dsa(q, idx, kcat, k, v, batch, seq)
    x1 = _mixout(pool, b_out, x2, pool_w.astype(BF16), pool_scale[None, :],
                 w_out[:POOL_WIDTH].astype(BF16), w_out[POOL_WIDTH:].astype(BF16), batch, seq)
    return _ffn(x1, norm2_g[None, :], w_up.astype(BF16), conv_w, conv_b[None, :], w_down.astype(BF16), batch, seq)


def kernel(x, norm1_g, w_in, q_norm_g, k_norm_g, pool_w, pool_scale, w_out, norm2_g, w_up, conv_w, conv_b, w_down):
    for l in range(norm1_g.shape[0]):
        x = _layer(x, norm1_g[l], w_in[l], q_norm_g[l], k_norm_g[l], pool_w[l], pool_scale[l], w_out[l],
                   norm2_g[l], w_up[l], conv_w[l], conv_b[l], w_down[l])
    return x
```

```python
import functools

import jax
import jax.numpy as jnp
from jax import lax
from jax.experimental import pallas as pl
from jax.experimental.pallas import tpu as pltpu

F32 = jnp.float32
BF16 = jnp.bfloat16

D_MODEL = 1024
POOL_WIDTH = 512
POOL_WINDOWS = (2, 4, 8, 16)
POOL_GROUP = 128
HEAD_DIM = 64
N_Q_HEADS = 8
N_KV_HEADS = 2
Q_PER_KV = 4
IDX_HEADS = 8
IDX_DIM = 32
TOPK_MAX = 256
ROPE_THETA = 10000.0
D_FF = 2816
CONV_WIDTH = 3
EPS = 1e-6
NEG = -1e30

LANES = 128
SUBLANES = 8
POOL_HALO = 16
QPAD = N_Q_HEADS * LANES
IDXW = 3 * LANES
WI_LANE = 3 * IDX_DIM
MAIN_W = POOL_WIDTH + (N_Q_HEADS + 2 * N_KV_HEADS) * HEAD_DIM

INPROJ_ROWS = 512
DSA_ROWS = 256
MIX_ROWS = 512
FFN_ROWS = 512
FFN_HALO = SUBLANES

NT_DIMS = (((1,), (1,)), ((), ()))


def _dot(a, b):
    return jnp.dot(a, b, preferred_element_type=F32)


def _dot_nt(a, b):
    return lax.dot_general(a, b, NT_DIMS, preferred_element_type=F32)


def _split_bf16(x):
    hi = x.astype(BF16).astype(F32)
    return hi, x - hi


def _swap_halves(xc, half):
    lane = lax.broadcasted_iota(jnp.int32, xc.shape, 1)
    first = (lane % (2 * half)) < half
    return jnp.where(first, pltpu.roll(xc, LANES - half, 1), pltpu.roll(xc, half, 1))


def _inproj_kernel(x_ref, g1_ref, wmain_ref, widx_ref, qg_ref, kg_ref,
                   qcos_ref, qsin_ref, icos_ref, isin_ref,
                   pool_ref, q_ref, k_ref, v_ref, idx_ref, kcat_ref):
    x = x_ref[...]
    ms = jnp.mean(x * x, axis=-1, keepdims=True)
    h = x * lax.rsqrt(ms + EPS) * g1_ref[...]
    h_hi = h.astype(BF16)

    proj = _dot(h_hi, wmain_ref[...])
    pool_ref[...] = proj[:, :POOL_WIDTH]

    qcos = qcos_ref[...]
    qsin = qsin_ref[...]
    lane = lax.broadcasted_iota(jnp.int32, (x.shape[0], LANES), 1)

    first = lane < HEAD_DIM

    def norm_rope(xc, gain):
        sq = xc * xc
        s0 = jnp.sum(jnp.where(first, sq, 0.0), axis=-1, keepdims=True)
        s1 = jnp.sum(jnp.where(first, 0.0, sq), axis=-1, keepdims=True)
        ss = jnp.where(first, s0, s1) * (1.0 / HEAD_DIM)
        xn = xc * lax.rsqrt(ss + EPS) * gain
        return xn * qcos + _swap_halves(xn, HEAD_DIM // 2) * qsin

    n_qcols = N_Q_HEADS * HEAD_DIM // LANES
    for c in range(n_qcols):
        c0 = POOL_WIDTH + c * LANES
        qr = norm_rope(proj[:, c0:c0 + LANES], qg_ref[...]) * (HEAD_DIM ** -0.5)
        qs = pltpu.roll(qr, HEAD_DIM, 1)
        if (2 * c) // Q_PER_KV == 0:
            pair = (jnp.where(first, qr, 0.0), jnp.where(first, qs, 0.0))
        else:
            pair = (jnp.where(first, 0.0, qs), jnp.where(first, 0.0, qr))
        for i, qh in enumerate(pair):
            q_ref[:, (2 * c + i) * LANES:(2 * c + i + 1) * LANES] = qh.astype(BF16)

    k0 = POOL_WIDTH + n_qcols * LANES
    k_ref[...] = norm_rope(proj[:, k0:k0 + LANES], kg_ref[...]).astype(BF16)
    v_ref[...] = proj[:, k0 + LANES:].astype(BF16)

    idxp = _dot(h_hi, widx_ref[...])
    for c in range(IDXW // LANES):
        sl = slice(c * LANES, (c + 1) * LANES)
        ic = idxp[:, sl]
        tail = ic * icos_ref[:, sl] + _swap_halves(ic, IDX_DIM // 2) * isin_ref[:, sl]
        idx_ref[:, sl] = tail

    t_hi, t_lo = _split_bf16(tail)
    mid = jnp.logical_and(lane >= IDX_DIM, lane < 2 * IDX_DIM)
    cat = jnp.where(mid, t_lo, t_hi)
    kcat_ref[...] = jnp.where(lane < WI_LANE, cat, 0.0).astype(BF16)


def _inproj(x2, g1, wmain, widx, qg, kg, qcos, qsin, icos, isin, seq):
    n = x2.shape[0]
    rows = INPROJ_ROWS
    tper = seq // rows
    row = lambda w: pl.BlockSpec((rows, w), lambda i: (i, 0))
    const = lambda a: pl.BlockSpec(a.shape, lambda i: (0, 0))
    table = lambda w: pl.BlockSpec((rows, w), lambda i: (i % tper, 0))
    return pl.pallas_call(
        _inproj_kernel,
        grid=(n // rows,),
        in_specs=[row(D_MODEL), const(g1), const(wmain), const(widx), const(qg), const(kg),
                  table(LANES), table(LANES), table(IDXW), table(IDXW)],
        out_specs=[row(POOL_WIDTH), row(QPAD), row(LANES), row(LANES), row(IDXW), row(LANES)],
        out_shape=[jax.ShapeDtypeStruct((n, POOL_WIDTH), F32),
                   jax.ShapeDtypeStruct((n, QPAD), BF16),
                   jax.ShapeDtypeStruct((n, LANES), BF16),
                   jax.ShapeDtypeStruct((n, LANES), BF16),
                   jax.ShapeDtypeStruct((n, IDXW), F32),
                   jax.ShapeDtypeStruct((n, LANES), BF16)],
        compiler_params=pltpu.CompilerParams(dimension_semantics=("arbitrary",),
                                             vmem_limit_bytes=48 * 1024 * 1024),
        name="inproj",
    )(x2, g1, wmain, widx, qg, kg, qcos, qsin, icos, isin)


def _key_to_float(key):
    bits = key ^ ((key >> 31) & jnp.int32(0x7FFFFFFF))
    return lax.bitcast_convert_type(bits, F32)


def _fold_groups(x, op, ways=4):
    n = x.shape[0] // SUBLANES
    accs = [x[g * SUBLANES:(g + 1) * SUBLANES, :] for g in range(ways)]
    for g in range(ways, n):
        accs[g % ways] = op(accs[g % ways], x[g * SUBLANES:(g + 1) * SUBLANES, :])
    while len(accs) > 1:
        accs = [op(accs[i], accs[i + 1]) for i in range(0, len(accs), 2)]
    return accs[0]


def _fold_sublanes(x, op):
    for shift in (4, 2, 1):
        x = op(x, pltpu.roll(x, shift, 0))
    return x


def _dsa_kernel(q_ref, idxq_ref, kcat_ref, k_ref, vt_ref, out_ref, sc_ref, qcat_ref, acc_ref, p_ref, lg_ref,
                *, seq, topk):
    rows = DSA_ROWS
    ck = DSA_ROWS
    ngrp = ck // SUBLANES
    qb = pl.program_id(1)
    nch = qb + 1
    lane = lax.broadcasted_iota(jnp.int32, (rows, LANES), 1)
    qidx = qb * rows + lax.broadcasted_iota(jnp.int32, (ck, rows), 1)
    krow = lax.broadcasted_iota(jnp.int32, (ck, rows), 0)

    for c in range(IDX_HEADS * IDX_DIM // LANES):
        hi, lo = _split_bf16(idxq_ref[:, c * LANES:(c + 1) * LANES])
        for r in range(LANES // IDX_DIM):
            s0 = (LANES - IDX_DIM * r) % LANES
            a = pltpu.roll(hi, s0, 1) if s0 else hi
            b = pltpu.roll(hi, (s0 + IDX_DIM) % LANES, 1)
            d = pltpu.roll(lo, (s0 + 2 * IDX_DIM) % LANES, 1)
            cat = jnp.where(lane < IDX_DIM, a,
                            jnp.where(lane < 2 * IDX_DIM, b, jnp.where(lane < 3 * IDX_DIM, d, 0.0)))
            qcat_ref[c * (LANES // IDX_DIM) + r] = cat.astype(BF16)
    wt = idxq_ref[:, 2 * LANES:].T[WI_LANE:WI_LANE + IDX_HEADS, :]

    def score_body(j, carry):
        kc = kcat_ref[pl.ds(pl.multiple_of(j * ck, ck), ck), :]
        acc = jnp.zeros((ck, rows), F32)
        for hd in range(IDX_HEADS):
            y = _dot_nt(kc, qcat_ref[hd])
            acc = acc + wt[hd:hd + 1, :] * jnp.maximum(y, 0.0)
        sc_ref[j] = jnp.where(j * ck + krow <= qidx, acc, NEG)
        return carry

    lax.fori_loop(0, nch, score_body, 0)

    n_virtual = (seq - nch * ck).astype(F32)

    def count(pred, thr):
        def body(j, accs):
            accs = list(accs)
            for g in range(ngrp):
                hit = pred(sc_ref[j, g * SUBLANES:(g + 1) * SUBLANES, :], thr)
                accs[g % len(accs)] = accs[g % len(accs)] + jnp.where(hit, 1.0, 0.0)
            return tuple(accs)
        zero = jnp.zeros((SUBLANES, rows), F32)
        accs = lax.fori_loop(0, nch, body, (zero, zero, zero, zero))
        tot = _fold_sublanes((accs[0] + accs[1]) + (accs[2] + accs[3]), jnp.add)
        return tot + jnp.where(pred(jnp.float32(NEG), thr), n_virtual, 0.0)

    ge = lambda a, b: a >= b
    gt = lambda a, b: a > b
    kf = jnp.float32(topk)

    c0 = count(ge, jnp.zeros((SUBLANES, rows), F32))
    key = jnp.where(c0 >= kf, jnp.int32(0), jnp.int32(-2 ** 31))

    def bit_body(i, key):
        cand = key | jnp.left_shift(jnp.int32(1), 30 - i)
        cnt = count(ge, _key_to_float(cand))
        return jnp.where(cnt >= kf, cand, key)

    key = lax.fori_loop(0, 31, bit_body, key)
    thr8 = _key_to_float(key)
    thr = thr8[0:1, :]
    need = (kf - count(gt, thr8))[0:1, :]

    ki = lax.broadcasted_iota(jnp.int32, (ck, ck), 0)
    kj = lax.broadcasted_iota(jnp.int32, (ck, ck), 1)
    lower = jnp.where(kj < ki, 1.0, 0.0).astype(BF16)
    ones8 = jnp.ones((SUBLANES, ck), BF16)

    def select_body(j, run):
        sc = sc_ref[j]
        eq = sc == thr
        eqb = jnp.where(eq, 1.0, 0.0).astype(BF16)
        prior = run[0:1, :] + _dot(lower, eqb)
        tie = jnp.where(prior < need, 0.0, NEG)
        bias = jnp.where(sc > thr, 0.0, jnp.where(eq, tie, NEG))
        sc_ref[j] = jnp.where(j * ck + krow <= qidx, bias, NEG)
        return run + _dot(ones8, eqb)

    lax.fori_loop(0, nch, select_body, jnp.zeros((SUBLANES, rows), F32))

    def max_body(j, ms):
        kc = k_ref[pl.ds(pl.multiple_of(j * ck, ck), ck), :]
        bias = sc_ref[j]
        out = []
        for hd in range(N_Q_HEADS):
            s = _dot_nt(kc, q_ref[:, hd * LANES:(hd + 1) * LANES]) + bias
            lg_ref[hd, j] = s
            out.append(jnp.maximum(ms[hd], _fold_groups(s, jnp.maximum)))
        return tuple(out)

    ms = lax.fori_loop(0, nch, max_body,
                       tuple(jnp.full((SUBLANES, rows), -jnp.inf, F32) for _ in range(N_Q_HEADS)))
    ms = [_fold_sublanes(m, jnp.maximum)[0:1, :] for m in ms]

    acc_ref[...] = jnp.zeros(acc_ref.shape, F32)

    def pv_body(j, ls):
        out = []
        for hd in range(N_Q_HEADS):
            p = jnp.exp(lg_ref[hd, j] - ms[hd])
            out.append(ls[hd] + _fold_groups(p, jnp.add))
            p_ref[hd] = p.astype(BF16)
        vt = vt_ref[j]
        for hd in range(N_Q_HEADS):
            acc_ref[hd] += _dot(vt, p_ref[hd])
        return tuple(out)

    ls = lax.fori_loop(0, nch, pv_body, tuple(jnp.zeros((SUBLANES, rows), F32) for _ in range(N_Q_HEADS)))

    outs = []
    for hd in range(N_Q_HEADS):
        l = _fold_sublanes(ls[hd], jnp.add)
        r0 = HEAD_DIM * (hd // Q_PER_KV)
        outs.append(acc_ref[hd, r0:r0 + HEAD_DIM, :] / l[0:1, :])
    out_ref[...] = jnp.concatenate(outs, axis=0).T.astype(BF16)


def _dsa(q, idx, kcat, k, v, batch, seq):
    rows = DSA_ROWS
    nq = seq // rows
    topk = min(TOPK_MAX, seq // 4)
    assert topk <= rows and seq % rows == 0
    q3 = q.reshape(batch, seq, QPAD)
    idx3 = idx.reshape(batch, seq, IDXW)
    vt = v.reshape(batch, nq, rows, LANES).transpose(0, 1, 3, 2)
    full = pl.BlockSpec((None, seq, LANES), lambda b, i: (b, 0, 0))
    kern = functools.partial(_dsa_kernel, seq=seq, topk=topk)
    return pl.pallas_call(
        kern,
        grid=(batch, nq),
        in_specs=[pl.BlockSpec((None, rows, QPAD), lambda b, i: (b, i, 0)),
                  pl.BlockSpec((None, rows, IDXW), lambda b, i: (b, i, 0)),
                  full, full,
                  pl.BlockSpec((None, nq, LANES, rows), lambda b, i: (b, 0, 0, 0))],
        out_specs=pl.BlockSpec((None, rows, N_Q_HEADS * HEAD_DIM), lambda b, i: (b, i, 0)),
        out_shape=jax.ShapeDtypeStruct((batch, seq, N_Q_HEADS * HEAD_DIM), BF16),
        scratch_shapes=[pltpu.VMEM((nq, rows, rows), F32),
                        pltpu.VMEM((IDX_HEADS, rows, LANES), BF16),
                        pltpu.VMEM((N_Q_HEADS, LANES, rows), F32),
                        pltpu.VMEM((N_Q_HEADS, rows, rows), BF16),
                        pltpu.VMEM((N_Q_HEADS, nq, rows, rows), F32)],
        compiler_params=pltpu.CompilerParams(dimension_semantics=("arbitrary", "arbitrary"),
                                             vmem_limit_bytes=48 * 1024 * 1024),
        name="dsa",
    )(q3, idx3, kcat.reshape(batch, seq, LANES), k.reshape(batch, seq, LANES), vt)


def _mixout_kernel(pool_ref, prev_ref, b_ref, x_ref, pw_ref, ps_ref, woa_ref, wob_ref, out_ref):
    rows = MIX_ROWS
    ti = pl.program_id(1)
    prev = jnp.where(ti == 0, 0.0, prev_ref[...])
    ext = jnp.concatenate([prev, pool_ref[...]], axis=0)
    t = ti * rows + lax.broadcasted_iota(jnp.int32, (rows, 1), 0)
    parts = []
    for g, w in enumerate(POOL_WINDOWS):
        sl = slice(g * POOL_GROUP, (g + 1) * POOL_GROUP)
        a = ext[:, sl]
        span = 1
        while span < w:
            a = a + pltpu.roll(a, span, 0)
            span *= 2
        cnt = jnp.minimum(t + 1, w).astype(F32)
        p = a[POOL_HALO:] / cnt - ext[POOL_HALO:, sl]
        y = _dot(p.astype(BF16), pw_ref[g])
        parts.append(y * ps_ref[:, sl])
    a_out = jnp.concatenate(parts, axis=1).astype(BF16)
    out_ref[...] = x_ref[...] + _dot(a_out, woa_ref[...]) + _dot(b_ref[...], wob_ref[...])


def _mixout(pool, b_out, x2, pw, ps, woa, wob, batch, seq):
    rows = MIX_ROWS
    per = seq // rows
    hb = rows // POOL_HALO
    tile = lambda w: pl.BlockSpec((None, rows, w), lambda b, i: (b, i, 0))
    const = lambda a: pl.BlockSpec(a.shape, lambda b, i: (0,) * a.ndim)
    return pl.pallas_call(
        _mixout_kernel,
        grid=(batch, per),
        in_specs=[tile(POOL_WIDTH),
                  pl.BlockSpec((None, POOL_HALO, POOL_WIDTH), lambda b, i: (b, jnp.maximum(i * hb - 1, 0), 0)),
                  tile(POOL_WIDTH), tile(D_MODEL), const(pw), const(ps), const(woa), const(wob)],
        out_specs=tile(D_MODEL),
        out_shape=jax.ShapeDtypeStruct((batch, seq, D_MODEL), F32),
        compiler_params=pltpu.CompilerParams(dimension_semantics=("arbitrary", "arbitrary"),
                                             vmem_limit_bytes=48 * 1024 * 1024),
        name="mixout",
    )(pool.reshape(batch, seq, POOL_WIDTH), pool.reshape(batch, seq, POOL_WIDTH), b_out,
      x2.reshape(batch, seq, D_MODEL), pw, ps, woa, wob)


def _ffn_kernel(x_ref, prev_ref, g2_ref, wup_ref, cw_ref, cb_ref, wdn_ref, out_ref, act_ref):
    rows = FFN_ROWS
    ti = pl.program_id(1)
    g2 = g2_ref[...]

    def norm(v):
        return v * lax.rsqrt(jnp.mean(v * v, axis=-1, keepdims=True) + EPS) * g2

    x = x_ref[...]
    hp = jnp.where(ti == 0, 0.0, norm(prev_ref[...]))
    h_ext = jnp.concatenate([hp, norm(x)], axis=0).astype(BF16)

    chunk = 2 * LANES
    for c in range(D_FF // chunk):
        halves = []
        for off in (0, D_FF):
            sl = slice(off + c * chunk, off + (c + 1) * chunk)
            u = _dot(h_ext, wup_ref[:, sl])
            u1 = pltpu.roll(u, 1, 0)
            u2 = pltpu.roll(u, 2, 0)
            cv = cb_ref[:, sl] + u2[FFN_HALO:] * cw_ref[0:1, sl]
            cv = cv + u1[FFN_HALO:] * cw_ref[1:2, sl]
            cv = cv + u[FFN_HALO:] * cw_ref[2:3, sl]
            halves.append(cv)
        gate, val = halves
        act = gate * (1.0 / (1.0 + jnp.exp(-gate))) * val
        act_ref[:, c * chunk:(c + 1) * chunk] = act.astype(BF16)
    out_ref[...] = x + _dot(act_ref[...], wdn_ref[...])


def _ffn(x1, g2, wup, cw, cb, wdn, batch, seq):
    rows = FFN_ROWS
    per = seq // rows
    hb = rows // FFN_HALO
    tile = pl.BlockSpec((None, rows, D_MODEL), lambda b, i: (b, i, 0))
    const = lambda a: pl.BlockSpec(a.shape, lambda b, i: (0,) * a.ndim, pipeline_mode=pl.Buffered(1))
    return pl.pallas_call(
        _ffn_kernel,
        grid=(batch, per),
        in_specs=[tile,
                  pl.BlockSpec((None, FFN_HALO, D_MODEL), lambda b, i: (b, jnp.maximum(i * hb - 1, 0), 0)),
                  const(g2), const(wup), const(cw), const(cb), const(wdn)],
        out_specs=tile,
        out_shape=jax.ShapeDtypeStruct((batch, seq, D_MODEL), F32),
        scratch_shapes=[pltpu.VMEM((rows, D_FF), BF16)],
        compiler_params=pltpu.CompilerParams(dimension_semantics=("arbitrary", "arbitrary"),
                                             vmem_limit_bytes=56 * 1024 * 1024),
        name="ffn",
    )(x1, x1, g2, wup, cw, cb, wdn)


def _rope_tables(seq, dim):
    half = dim // 2
    inv = jnp.exp(-jnp.log(jnp.float32(ROPE_THETA)) * jnp.arange(half, dtype=F32) / half)
    ang = jnp.arange(seq).astype(F32)[:, None] * inv[None, :]
    cos, sin = jnp.cos(ang), jnp.sin(ang)
    return jnp.concatenate([cos, cos], axis=1), jnp.concatenate([-sin, sin], axis=1)


def _layer(x, norm1_g, w_in, q_norm_g, k_norm_g, pool_w, pool_scale, w_out, norm2_g, w_up, conv_w, conv_b, w_down):
    batch, seq, _ = x.shape
    n = batch * seq
    wmain = w_in[:, :MAIN_W].astype(BF16)
    w_qi, w_ki, w_wi = jnp.split(w_in[:, MAIN_W:], [IDX_HEADS * IDX_DIM, (IDX_HEADS + 1) * IDX_DIM], axis=1)
    qg = jnp.concatenate([q_norm_g, q_norm_g])[None, :]
    kg = jnp.concatenate([k_norm_g, k_norm_g])[None, :]
    w_idx = jnp.concatenate([w_qi, w_ki, w_ki, w_ki, w_wi,
                             jnp.zeros((D_MODEL, IDXW - WI_LANE - IDX_HEADS - IDX_HEADS * IDX_DIM), F32)], axis=1)

    c64, s64 = _rope_tables(seq, HEAD_DIM)
    qcos, qsin = jnp.tile(c64, (1, LANES // HEAD_DIM)), jnp.tile(s64, (1, LANES // HEAD_DIM))
    c32, s32 = _rope_tables(seq, IDX_DIM)
    nrot = (IDX_HEADS + 3) * IDX_DIM
    wscale = jnp.full((seq, IDX_HEADS), IDX_HEADS ** -0.5 * IDX_DIM ** -0.5, F32)
    icos = jnp.concatenate([jnp.tile(c32, (1, nrot // IDX_DIM)), wscale,
                            jnp.zeros((seq, IDXW - nrot - IDX_HEADS), F32)], axis=1)
    isin = jnp.concatenate([jnp.tile(s32, (1, nrot // IDX_DIM)), jnp.zeros((seq, IDXW - nrot), F32)], axis=1)

    x2 = x.reshape(n, D_MODEL)
    pool, q, k, v, idx, kcat = _inproj(x2, norm1_g[None, :], wmain, w_idx.astype(BF16), qg, kg, qcos, qsin, icos, isin, seq)
    b_out = _dsa(q, idx, kcat, k, v, batch, seq)
    x1 = _mixout(pool, b_out, x2, pool_w.astype(BF16), pool_scale[None, :],
                 w_out[:POOL_WIDTH].astype(BF16), w_out[POOL_WIDTH:].astype(BF16), batch, seq)
    return _ffn(x1, norm2_g[None, :], w_up.astype(BF16), conv_w, conv_b[None, :], w_down.astype(BF16), batch, seq)


def kernel(x, norm1_g, w_in, q_norm_g, k_norm_g, pool_w, pool_scale, w_out, norm2_g, w_up, conv_w, conv_b, w_down):
    for l in range(norm1_g.shape[0]):
        x = _layer(x, norm1_g[l], w_in[l], q_norm_g[l], k_norm_g[l], pool_w[l], pool_scale[l], w_out[l],
                   norm2_g[l], w_up[l], conv_w[l], conv_b[l], w_down[l])
    return x
```

```python
import functools

import jax
import jax.numpy as jnp
from jax import lax
from jax.experimental import pallas as pl
from jax.experimental.pallas import tpu as pltpu

F32 = jnp.float32
BF16 = jnp.bfloat16

D_MODEL = 1024
POOL_WIDTH = 512
POOL_WINDOWS = (2, 4, 8, 16)
POOL_GROUP = 128
HEAD_DIM = 64
N_Q_HEADS = 8
N_KV_HEADS = 2
Q_PER_KV = 4
IDX_HEADS = 8
IDX_DIM = 32
TOPK_MAX = 256
ROPE_THETA = 10000.0
D_FF = 2816
CONV_WIDTH = 3
EPS = 1e-6
NEG = -1e30

LANES = 128
SUBLANES = 8
POOL_HALO = 16
QPAD = N_Q_HEADS * LANES
IDXW = 3 * LANES
IDX_PROJ_W = IDXW + LANES
MAIN_W = POOL_WIDTH + (N_Q_HEADS + 2 * N_KV_HEADS) * HEAD_DIM

INPROJ_ROWS = 512
DSA_ROWS = 256
MIX_ROWS = 512
FFN_ROWS = 512
FFN_HALO = SUBLANES

NT_DIMS = (((1,), (1,)), ((), ()))


def _dot(a, b):
    return jnp.dot(a, b, preferred_element_type=F32)


def _dot_nt(a, b):
    return lax.dot_general(a, b, NT_DIMS, preferred_element_type=F32)


def _swap_halves(xc, half):
    lane = lax.broadcasted_iota(jnp.int32, xc.shape, 1)
    first = (lane % (2 * half)) < half
    return jnp.where(first, pltpu.roll(xc, LANES - half, 1), pltpu.roll(xc, half, 1))


def _inproj_kernel(x_ref, g1_ref, wmain_ref, widx_ref, qg_ref, kg_ref,
                   qcos_ref, qsin_ref, icos_ref, isin_ref,
                   pool_ref, q_ref, k_ref, v_ref, idx_ref, kcat_ref):
    x = x_ref[...]
    ms = jnp.mean(x * x, axis=-1, keepdims=True)
    h = x * lax.rsqrt(ms + EPS) * g1_ref[...]
    h_hi = h.astype(BF16)

    qcos = qcos_ref[...]
    qsin = qsin_ref[...]
    lane = lax.broadcasted_iota(jnp.int32, (x.shape[0], LANES), 1)
    first = lane < HEAD_DIM

    def norm_rope(xc, gain):
        sq = xc * xc
        s0 = jnp.sum(jnp.where(first, sq, 0.0), axis=-1, keepdims=True)
        s1 = jnp.sum(jnp.where(first, 0.0, sq), axis=-1, keepdims=True)
        ss = jnp.where(first, s0, s1) * (1.0 / HEAD_DIM)
        xn = xc * lax.rsqrt(ss + EPS) * gain
        return xn * qcos + _swap_halves(xn, HEAD_DIM // 2) * qsin

    n_qcols = N_Q_HEADS * HEAD_DIM // LANES
    k0 = POOL_WIDTH + n_qcols * LANES
    qk = _dot(h_hi, wmain_ref[:, POOL_WIDTH:k0 + LANES])
    idxp = _dot(h_hi, widx_ref[...])

    for c in range(n_qcols):
        qr = norm_rope(qk[:, c * LANES:(c + 1) * LANES], qg_ref[...]) * (HEAD_DIM ** -0.5)
        qs = pltpu.roll(qr, HEAD_DIM, 1)
        if (2 * c) // Q_PER_KV == 0:
            pair = (jnp.where(first, qr, 0.0), jnp.where(first, qs, 0.0))
        else:
            pair = (jnp.where(first, 0.0, qs), jnp.where(first, 0.0, qr))
        for i, qh in enumerate(pair):
            q_ref[:, (2 * c + i) * LANES:(2 * c + i + 1) * LANES] = qh.astype(BF16)
    k_ref[...] = norm_rope(qk[:, n_qcols * LANES:], kg_ref[...]).astype(BF16)

    for c in range(IDX_PROJ_W // LANES):
        sl = slice(c * LANES, (c + 1) * LANES)
        ic = idxp[:, sl]
        roped = ic * icos_ref[:, sl] + _swap_halves(ic, IDX_DIM // 2) * isin_ref[:, sl]
        if c < IDXW // LANES:
            idx_ref[:, sl] = roped
        else:
            kcat_ref[...] = roped.astype(BF16)

    pool_ref[...] = _dot(h_hi, wmain_ref[:, :POOL_WIDTH])
    v_ref[...] = _dot(h_hi, wmain_ref[:, k0 + LANES:]).astype(BF16)


def _inproj(x2, g1, wmain, widx, qg, kg, qcos, qsin, icos, isin, seq):
    n = x2.shape[0]
    rows = INPROJ_ROWS
    tper = seq // rows
    row = lambda w: pl.BlockSpec((rows, w), lambda i: (i, 0))
    const = lambda a: pl.BlockSpec(a.shape, lambda i: (0, 0))
    table = lambda w: pl.BlockSpec((rows, w), lambda i: (i % tper, 0))
    return pl.pallas_call(
        _inproj_kernel,
        grid=(n // rows,),
        in_specs=[row(D_MODEL), const(g1), const(wmain), const(widx), const(qg), const(kg),
                  table(LANES), table(LANES), table(IDX_PROJ_W), table(IDX_PROJ_W)],
        out_specs=[row(POOL_WIDTH), row(QPAD), row(LANES), row(LANES), row(IDXW), row(LANES)],
        out_shape=[jax.ShapeDtypeStruct((n, POOL_WIDTH), F32),
                   jax.ShapeDtypeStruct((n, QPAD), BF16),
                   jax.ShapeDtypeStruct((n, LANES), BF16),
                   jax.ShapeDtypeStruct((n, LANES), BF16),
                   jax.ShapeDtypeStruct((n, IDXW), F32),
                   jax.ShapeDtypeStruct((n, LANES), BF16)],
        compiler_params=pltpu.CompilerParams(dimension_semantics=("arbitrary",),
                                             vmem_limit_bytes=48 * 1024 * 1024),
        name="inproj",
    )(x2, g1, wmain, widx, qg, kg, qcos, qsin, icos, isin)


def _key_to_float(key):
    bits = key ^ ((key >> 31) & jnp.int32(0x7FFFFFFF))
    return lax.bitcast_convert_type(bits, F32)


def _fold_groups(x, op, ways=4):
    n = x.shape[0] // SUBLANES
    accs = [x[g * SUBLANES:(g + 1) * SUBLANES, :] for g in range(ways)]
    for g in range(ways, n):
        accs[g % ways] = op(accs[g % ways], x[g * SUBLANES:(g + 1) * SUBLANES, :])
    while len(accs) > 1:
        accs = [op(accs[i], accs[i + 1]) for i in range(0, len(accs), 2)]
    return accs[0]


def _fold_sublanes(x, op):
    for shift in (4, 2, 1):
        x = op(x, pltpu.roll(x, shift, 0))
    return x


def _dsa_kernel(q_ref, idxq_ref, kcat_ref, k_ref, vt_ref, out_ref, sc_ref, qcat_ref, acc_ref, p_ref, lg_ref,
                *, seq, topk):
    rows = DSA_ROWS
    ck = DSA_ROWS
    ngrp = ck // SUBLANES
    qb = pl.program_id(1)
    nch = qb + 1
    lane = lax.broadcasted_iota(jnp.int32, (rows, LANES), 1)
    qidx = qb * rows + lax.broadcasted_iota(jnp.int32, (ck, rows), 1)
    krow = lax.broadcasted_iota(jnp.int32, (ck, rows), 0)

    slots = LANES // IDX_DIM
    for c in range(IDX_HEADS // slots):
        col = idxq_ref[:, c * LANES:(c + 1) * LANES]
        for r in range(slots):
            live = jnp.logical_and(lane >= r * IDX_DIM, lane < (r + 1) * IDX_DIM)
            qcat_ref[c * slots + r] = jnp.where(live, col, 0.0).astype(BF16)
    wt = idxq_ref[:, 2 * LANES:].T[0:IDX_HEADS, :]

    def score_body(j, carry):
        kc = kcat_ref[pl.ds(pl.multiple_of(j * ck, ck), ck), :]
        acc = jnp.zeros((ck, rows), F32)
        for hd in range(IDX_HEADS):
            y = _dot_nt(kc, qcat_ref[hd])
            acc = acc + wt[hd:hd + 1, :] * jnp.maximum(y, 0.0)
        sc_ref[j] = jnp.where(j * ck + krow <= qidx, acc, NEG)
        return carry

    lax.fori_loop(0, nch, score_body, 0)

    n_virtual = (seq - nch * ck).astype(F32)

    def count(pred, thr):
        def body(j, accs):
            accs = list(accs)
            for g in range(ngrp):
                hit = pred(sc_ref[j, g * SUBLANES:(g + 1) * SUBLANES, :], thr)
                accs[g % len(accs)] = accs[g % len(accs)] + jnp.where(hit, 1.0, 0.0)
            return tuple(accs)
        zero = jnp.zeros((SUBLANES, rows), F32)
        accs = lax.fori_loop(0, nch, body, (zero, zero, zero, zero))
        tot = _fold_sublanes((accs[0] + accs[1]) + (accs[2] + accs[3]), jnp.add)
        return tot + jnp.where(pred(jnp.float32(NEG), thr), n_virtual, 0.0)

    ge = lambda a, b: a >= b
    gt = lambda a, b: a > b
    kf = jnp.float32(topk)

    c0 = count(ge, jnp.zeros((SUBLANES, rows), F32))
    key = jnp.where(c0 >= kf, jnp.int32(0), jnp.int32(-2 ** 31))

    def bit_body(i, key):
        cand = key | jnp.left_shift(jnp.int32(1), 30 - i)
        cnt = count(ge, _key_to_float(cand))
        return jnp.where(cnt >= kf, cand, key)

    key = lax.fori_loop(0, 31, bit_body, key)
    thr8 = _key_to_float(key)
    thr = thr8[0:1, :]
    need = (kf - count(gt, thr8))[0:1, :]

    ki = lax.broadcasted_iota(jnp.int32, (ck, ck), 0)
    kj = lax.broadcasted_iota(jnp.int32, (ck, ck), 1)
    lower = jnp.where(kj < ki, 1.0, 0.0).astype(BF16)
    ones8 = jnp.ones((SUBLANES, ck), BF16)

    def select_body(j, run):
        sc = sc_ref[j]
        eq = sc == thr
        eqb = jnp.where(eq, 1.0, 0.0).astype(BF16)
        prior = run[0:1, :] + _dot(lower, eqb)
        tie = jnp.where(prior < need, 0.0, NEG)
        bias = jnp.where(sc > thr, 0.0, jnp.where(eq, tie, NEG))
        sc_ref[j] = jnp.where(j * ck + krow <= qidx, bias, NEG)
        return run + _dot(ones8, eqb)

    lax.fori_loop(0, nch, select_body, jnp.zeros((SUBLANES, rows), F32))

    def max_body(j, ms):
        kc = k_ref[pl.ds(pl.multiple_of(j * ck, ck), ck), :]
        bias = sc_ref[j]
        out = []
        for hd in range(N_Q_HEADS):
            s = _dot_nt(kc, q_ref[:, hd * LANES:(hd + 1) * LANES]) + bias
            lg_ref[hd, j] = s
            out.append(jnp.maximum(ms[hd], _fold_groups(s, jnp.maximum)))
        return tuple(out)

    ms = lax.fori_loop(0, nch, max_body,
                       tuple(jnp.full((SUBLANES, rows), -jnp.inf, F32) for _ in range(N_Q_HEADS)))
    ms = [_fold_sublanes(m, jnp.maximum)[0:1, :] for m in ms]

    acc_ref[...] = jnp.zeros(acc_ref.shape, F32)

    def pv_body(j, ls):
        out = []
        for hd in range(N_Q_HEADS):
            p = jnp.exp(lg_ref[hd, j] - ms[hd])
            out.append(ls[hd] + _fold_groups(p, jnp.add))
            p_ref[hd] = p.astype(BF16)
        vt = vt_ref[j]
        for hd in range(N_Q_HEADS):
            acc_ref[hd] += _dot(vt, p_ref[hd])
        return tuple(out)

    ls = lax.fori_loop(0, nch, pv_body, tuple(jnp.zeros((SUBLANES, rows), F32) for _ in range(N_Q_HEADS)))

    outs = []
    for hd in range(N_Q_HEADS):
        l = _fold_sublanes(ls[hd], jnp.add)
        r0 = HEAD_DIM * (hd // Q_PER_KV)
        outs.append(acc_ref[hd, r0:r0 + HEAD_DIM, :] / l[0:1, :])
    out_ref[...] = jnp.concatenate(outs, axis=0).T.astype(BF16)


def _dsa(q, idx, kcat, k, v, batch, seq):
    rows = DSA_ROWS
    nq = seq // rows
    topk = min(TOPK_MAX, seq // 4)
    assert topk <= rows and seq % rows == 0
    q3 = q.reshape(batch, seq, QPAD)
    idx3 = idx.reshape(batch, seq, IDXW)
    vt = v.reshape(batch, nq, rows, LANES).transpose(0, 1, 3, 2)
    full = pl.BlockSpec((None, seq, LANES), lambda b, i: (b, 0, 0))
    kern = functools.partial(_dsa_kernel, seq=seq, topk=topk)
    return pl.pallas_call(
        kern,
        grid=(batch, nq),
        in_specs=[pl.BlockSpec((None, rows, QPAD), lambda b, i: (b, i, 0)),
                  pl.BlockSpec((None, rows, IDXW), lambda b, i: (b, i, 0)),
                  full, full,
                  pl.BlockSpec((None, nq, LANES, rows), lambda b, i: (b, 0, 0, 0))],
        out_specs=pl.BlockSpec((None, rows, N_Q_HEADS * HEAD_DIM), lambda b, i: (b, i, 0)),
        out_shape=jax.ShapeDtypeStruct((batch, seq, N_Q_HEADS * HEAD_DIM), BF16),
        scratch_shapes=[pltpu.VMEM((nq, rows, rows), F32),
                        pltpu.VMEM((IDX_HEADS, rows, LANES), BF16),
                        pltpu.VMEM((N_Q_HEADS, LANES, rows), F32),
                        pltpu.VMEM((N_Q_HEADS, rows, rows), BF16),
                        pltpu.VMEM((N_Q_HEADS, nq, rows, rows), F32)],
        compiler_params=pltpu.CompilerParams(dimension_semantics=("arbitrary", "arbitrary"),
                                             vmem_limit_bytes=48 * 1024 * 1024),
        name="dsa",
    )(q3, idx3, kcat.reshape(batch, seq, LANES), k.reshape(batch, seq, LANES), vt)


def _mixout_kernel(pool_ref, prev_ref, b_ref, x_ref, pw_ref, ps_ref, woa_ref, wob_ref, out_ref):
    rows = MIX_ROWS
    ti = pl.program_id(1)
    prev = jnp.where(ti == 0, 0.0, prev_ref[...])
    ext = jnp.concatenate([prev, pool_ref[...]], axis=0)
    t = ti * rows + lax.broadcasted_iota(jnp.int32, (rows, 1), 0)
    parts = []
    for g, w in enumerate(POOL_WINDOWS):
        sl = slice(g * POOL_GROUP, (g + 1) * POOL_GROUP)
        a = ext[:, sl]
        span = 1
        while span < w:
            a = a + pltpu.roll(a, span, 0)
            span *= 2
        cnt = jnp.minimum(t + 1, w).astype(F32)
        p = a[POOL_HALO:] / cnt - ext[POOL_HALO:, sl]
        y = _dot(p.astype(BF16), pw_ref[g])
        parts.append(y * ps_ref[:, sl])
    a_out = jnp.concatenate(parts, axis=1).astype(BF16)
    out_ref[...] = x_ref[...] + _dot(a_out, woa_ref[...]) + _dot(b_ref[...], wob_ref[...])


def _mixout(pool, b_out, x2, pw, ps, woa, wob, batch, seq):
    rows = MIX_ROWS
    per = seq // rows
    hb = rows // POOL_HALO
    tile = lambda w: pl.BlockSpec((None, rows, w), lambda b, i: (b, i, 0))
    const = lambda a: pl.BlockSpec(a.shape, lambda b, i: (0,) * a.ndim)
    return pl.pallas_call(
        _mixout_kernel,
        grid=(batch, per),
        in_specs=[tile(POOL_WIDTH),
                  pl.BlockSpec((None, POOL_HALO, POOL_WIDTH), lambda b, i: (b, jnp.maximum(i * hb - 1, 0), 0)),
                  tile(POOL_WIDTH), tile(D_MODEL), const(pw), const(ps), const(woa), const(wob)],
        out_specs=tile(D_MODEL),
        out_shape=jax.ShapeDtypeStruct((batch, seq, D_MODEL), F32),
        compiler_params=pltpu.CompilerParams(dimension_semantics=("arbitrary", "arbitrary"),
                                             vmem_limit_bytes=48 * 1024 * 1024),
        name="mixout",
    )(pool.reshape(batch, seq, POOL_WIDTH), pool.reshape(batch, seq, POOL_WIDTH), b_out,
      x2.reshape(batch, seq, D_MODEL), pw, ps, woa, wob)


def _ffn_kernel(x_ref, prev_ref, g2_ref, wup_ref, cw_ref, cb_ref, wdn_ref, out_ref, act_ref):
    rows = FFN_ROWS
    ti = pl.program_id(1)
    g2 = g2_ref[...]

    def norm(v):
        return v * lax.rsqrt(jnp.mean(v * v, axis=-1, keepdims=True) + EPS) * g2

    x = x_ref[...]
    hp = jnp.where(ti == 0, 0.0, norm(prev_ref[...]))
    h_ext = jnp.concatenate([hp, norm(x)], axis=0).astype(BF16)

    chunk = 2 * LANES
    for c in range(D_FF // chunk):
        halves = []
        for off in (0, D_FF):
            sl = slice(off + c * chunk, off + (c + 1) * chunk)
            u = _dot(h_ext, wup_ref[:, sl])
            u1 = pltpu.roll(u, 1, 0)
            u2 = pltpu.roll(u, 2, 0)
            cv = cb_ref[:, sl] + u2[FFN_HALO:] * cw_ref[0:1, sl]
            cv = cv + u1[FFN_HALO:] * cw_ref[1:2, sl]
            cv = cv + u[FFN_HALO:] * cw_ref[2:3, sl]
            halves.append(cv)
        gate, val = halves
        act = gate * (1.0 / (1.0 + jnp.exp(-gate))) * val
        act_ref[:, c * chunk:(c + 1) * chunk] = act.astype(BF16)
    out_ref[...] = x + _dot(act_ref[...], wdn_ref[...])


def _ffn(x1, g2, wup, cw, cb, wdn, batch, seq):
    rows = FFN_ROWS
    per = seq // rows
    hb = rows // FFN_HALO
    tile = pl.BlockSpec((None, rows, D_MODEL), lambda b, i: (b, i, 0))
    const = lambda a: pl.BlockSpec(a.shape, lambda b, i: (0,) * a.ndim, pipeline_mode=pl.Buffered(1))
    return pl.pallas_call(
        _ffn_kernel,
        grid=(batch, per),
        in_specs=[tile,
                  pl.BlockSpec((None, FFN_HALO, D_MODEL), lambda b, i: (b, jnp.maximum(i * hb - 1, 0), 0)),
                  const(g2), const(wup), const(cw), const(cb), const(wdn)],
        out_specs=tile,
        out_shape=jax.ShapeDtypeStruct((batch, seq, D_MODEL), F32),
        scratch_shapes=[pltpu.VMEM((rows, D_FF), BF16)],
        compiler_params=pltpu.CompilerParams(dimension_semantics=("arbitrary", "arbitrary"),
                                             vmem_limit_bytes=56 * 1024 * 1024),
        name="ffn",
    )(x1, x1, g2, wup, cw, cb, wdn)


def _rope_tables(seq, dim):
    half = dim // 2
    inv = jnp.exp(-jnp.log(jnp.float32(ROPE_THETA)) * jnp.arange(half, dtype=F32) / half)
    ang = jnp.arange(seq).astype(F32)[:, None] * inv[None, :]
    cos, sin = jnp.cos(ang), jnp.sin(ang)
    return jnp.concatenate([cos, cos], axis=1), jnp.concatenate([-sin, sin], axis=1)


def _layer(x, norm1_g, w_in, q_norm_g, k_norm_g, pool_w, pool_scale, w_out, norm2_g, w_up, conv_w, conv_b, w_down):
    batch, seq, _ = x.shape
    n = batch * seq
    wmain = w_in[:, :MAIN_W].astype(BF16)
    w_qi, w_ki, w_wi = jnp.split(w_in[:, MAIN_W:], [IDX_HEADS * IDX_DIM, (IDX_HEADS + 1) * IDX_DIM], axis=1)
    qg = jnp.concatenate([q_norm_g, q_norm_g])[None, :]
    kg = jnp.concatenate([k_norm_g, k_norm_g])[None, :]
    slots = LANES // IDX_DIM
    w_idx = jnp.concatenate([w_qi, w_wi, jnp.zeros((D_MODEL, LANES - IDX_HEADS), F32)] + [w_ki] * slots, axis=1)

    c64, s64 = _rope_tables(seq, HEAD_DIM)
    qcos, qsin = jnp.tile(c64, (1, LANES // HEAD_DIM)), jnp.tile(s64, (1, LANES // HEAD_DIM))
    c32, s32 = _rope_tables(seq, IDX_DIM)
    wscale = jnp.full((seq, IDX_HEADS), IDX_HEADS ** -0.5 * IDX_DIM ** -0.5, F32)
    gap = jnp.zeros((seq, LANES - IDX_HEADS), F32)
    icos = jnp.concatenate([jnp.tile(c32, (1, IDX_HEADS)), wscale, gap, jnp.tile(c32, (1, slots))], axis=1)
    isin = jnp.concatenate([jnp.tile(s32, (1, IDX_HEADS)), jnp.zeros((seq, LANES), F32),
                            jnp.tile(s32, (1, slots))], axis=1)

    x2 = x.reshape(n, D_MODEL)
    pool, q, k, v, idx, kcat = _inproj(x2, norm1_g[None, :], wmain, w_idx.astype(BF16), qg, kg, qcos, qsin, icos, isin, seq)
    b_out = _dsa(q, idx, kcat, k, v, batch, seq)
    x1 = _mixout(pool, b_out, x2, pool_w.astype(BF16), pool_scale[None, :],
                 w_out[:POOL_WIDTH].astype(BF16), w_out[POOL_WIDTH:].astype(BF16), batch, seq)
    return _ffn(x1, norm2_g[None, :], w_up.astype(BF16), conv_w, conv_b[None, :], w_down.astype(BF16), batch, seq)


def kernel(x, norm1_g, w_in, q_norm_g, k_norm_g, pool_w, pool_scale, w_out, norm2_g, w_up, conv_w, conv_b, w_down):
    for l in range(norm1_g.shape[0]):
        x = _layer(x, norm1_g[l], w_in[l], q_norm_g[l], k_norm_g[l], pool_w[l], pool_scale[l], w_out[l],
                   norm2_g[l], w_up[l], conv_w[l], conv_b[l], w_down[l])
    return x
```

```python
import functools

import jax
import jax.numpy as jnp
from jax import lax
from jax.experimental import pallas as pl
from jax.experimental.pallas import tpu as pltpu

F32 = jnp.float32
BF16 = jnp.bfloat16

D_MODEL = 1024
POOL_WIDTH = 512
POOL_WINDOWS = (2, 4, 8, 16)
POOL_GROUP = 128
HEAD_DIM = 64
N_Q_HEADS = 8
N_KV_HEADS = 2
Q_PER_KV = 4
IDX_HEADS = 8
IDX_DIM = 32
TOPK_MAX = 256
ROPE_THETA = 10000.0
D_FF = 2816
CONV_WIDTH = 3
EPS = 1e-6
NEG = -1e30

LANES = 128
SUBLANES = 8
POOL_HALO = 16
QPAD = N_Q_HEADS * LANES
IDXW = 3 * LANES
IDX_PROJ_W = IDXW + LANES
MAIN_W = POOL_WIDTH + (N_Q_HEADS + 2 * N_KV_HEADS) * HEAD_DIM

INPROJ_ROWS = 512
DSA_ROWS = 256
MIX_ROWS = 512
FFN_ROWS = 512
FFN_HALO = SUBLANES

NT_DIMS = (((1,), (1,)), ((), ()))


def _dot(a, b):
    return jnp.dot(a, b, preferred_element_type=F32)


def _dot_nt(a, b):
    return lax.dot_general(a, b, NT_DIMS, preferred_element_type=F32)


def _swap_halves(xc, half):
    lane = lax.broadcasted_iota(jnp.int32, xc.shape, 1)
    first = (lane % (2 * half)) < half
    return jnp.where(first, pltpu.roll(xc, LANES - half, 1), pltpu.roll(xc, half, 1))


def _inproj_kernel(x_ref, g1_ref, wmain_ref, widx_ref, qg_ref, kg_ref,
                   qcos_ref, qsin_ref, icos_ref, isin_ref,
                   pool_ref, q_ref, k_ref, v_ref, idx_ref, kcat_ref):
    x = x_ref[...]
    ms = jnp.mean(x * x, axis=-1, keepdims=True)
    h = x * lax.rsqrt(ms + EPS) * g1_ref[...]
    h_hi = h.astype(BF16)

    qcos = qcos_ref[...]
    qsin = qsin_ref[...]
    lane = lax.broadcasted_iota(jnp.int32, (x.shape[0], LANES), 1)
    first = lane < HEAD_DIM

    def norm_rope(xc, gain):
        sq = xc * xc
        s0 = jnp.sum(jnp.where(first, sq, 0.0), axis=-1, keepdims=True)
        s1 = jnp.sum(jnp.where(first, 0.0, sq), axis=-1, keepdims=True)
        ss = jnp.where(first, s0, s1) * (1.0 / HEAD_DIM)
        xn = xc * lax.rsqrt(ss + EPS) * gain
        return xn * qcos + _swap_halves(xn, HEAD_DIM // 2) * qsin

    n_qcols = N_Q_HEADS * HEAD_DIM // LANES
    k0 = POOL_WIDTH + n_qcols * LANES
    qk = _dot(h_hi, wmain_ref[:, POOL_WIDTH:k0 + LANES])
    idxp = _dot(h_hi, widx_ref[...])

    for c in range(n_qcols):
        qr = norm_rope(qk[:, c * LANES:(c + 1) * LANES], qg_ref[...]) * (HEAD_DIM ** -0.5)
        qs = pltpu.roll(qr, HEAD_DIM, 1)
        if (2 * c) // Q_PER_KV == 0:
            pair = (jnp.where(first, qr, 0.0), jnp.where(first, qs, 0.0))
        else:
            pair = (jnp.where(first, 0.0, qs), jnp.where(first, 0.0, qr))
        for i, qh in enumerate(pair):
            q_ref[:, (2 * c + i) * LANES:(2 * c + i + 1) * LANES] = qh.astype(BF16)
    k_ref[...] = norm_rope(qk[:, n_qcols * LANES:], kg_ref[...]).astype(BF16)

    for c in range(IDX_PROJ_W // LANES):
        sl = slice(c * LANES, (c + 1) * LANES)
        ic = idxp[:, sl]
        roped = ic * icos_ref[:, sl] + _swap_halves(ic, IDX_DIM // 2) * isin_ref[:, sl]
        if c < IDXW // LANES:
            idx_ref[:, sl] = roped
        else:
            kcat_ref[...] = roped.astype(BF16)

    pool_ref[...] = _dot(h_hi, wmain_ref[:, :POOL_WIDTH])
    v_ref[...] = _dot(h_hi, wmain_ref[:, k0 + LANES:]).astype(BF16)


def _inproj(x2, g1, wmain, widx, qg, kg, qcos, qsin, icos, isin, seq):
    n = x2.shape[0]
    rows = INPROJ_ROWS
    tper = seq // rows
    row = lambda w: pl.BlockSpec((rows, w), lambda i: (i, 0))
    const = lambda a: pl.BlockSpec(a.shape, lambda i: (0, 0))
    table = lambda w: pl.BlockSpec((rows, w), lambda i: (i % tper, 0))
    return pl.pallas_call(
        _inproj_kernel,
        grid=(n // rows,),
        in_specs=[row(D_MODEL), const(g1), const(wmain), const(widx), const(qg), const(kg),
                  table(LANES), table(LANES), table(IDX_PROJ_W), table(IDX_PROJ_W)],
        out_specs=[row(POOL_WIDTH), row(QPAD), row(LANES), row(LANES), row(IDXW), row(LANES)],
        out_shape=[jax.ShapeDtypeStruct((n, POOL_WIDTH), F32),
                   jax.ShapeDtypeStruct((n, QPAD), BF16),
                   jax.ShapeDtypeStruct((n, LANES), BF16),
                   jax.ShapeDtypeStruct((n, LANES), BF16),
                   jax.ShapeDtypeStruct((n, IDXW), F32),
                   jax.ShapeDtypeStruct((n, LANES), BF16)],
        compiler_params=pltpu.CompilerParams(dimension_semantics=("arbitrary",),
                                             vmem_limit_bytes=48 * 1024 * 1024),
        name="inproj",
    )(x2, g1, wmain, widx, qg, kg, qcos, qsin, icos, isin)


def _key_to_float(key):
    bits = key ^ ((key >> 31) & jnp.int32(0x7FFFFFFF))
    return lax.bitcast_convert_type(bits, F32)


def _fold_groups(x, op, ways=4):
    n = x.shape[0] // SUBLANES
    accs = [x[g * SUBLANES:(g + 1) * SUBLANES, :] for g in range(ways)]
    for g in range(ways, n):
        accs[g % ways] = op(accs[g % ways], x[g * SUBLANES:(g + 1) * SUBLANES, :])
    while len(accs) > 1:
        accs = [op(accs[i], accs[i + 1]) for i in range(0, len(accs), 2)]
    return accs[0]


def _fold_sublanes(x, op):
    for shift in (4, 2, 1):
        x = op(x, pltpu.roll(x, shift, 0))
    return x


def _dsa_kernel(q_ref, idxq_ref, kcat_ref, k_ref, vt_ref, out_ref, sc_ref, qcat_ref, acc_ref, p_ref, lg_ref,
                *, seq, topk):
    rows = DSA_ROWS
    ck = DSA_ROWS
    ngrp = ck // SUBLANES
    qb = pl.program_id(1)
    nch = qb + 1
    lane = lax.broadcasted_iota(jnp.int32, (rows, LANES), 1)
    qidx = qb * rows + lax.broadcasted_iota(jnp.int32, (ck, rows), 1)
    krow = lax.broadcasted_iota(jnp.int32, (ck, rows), 0)

    slots = LANES // IDX_DIM
    for c in range(IDX_HEADS // slots):
        col = idxq_ref[:, c * LANES:(c + 1) * LANES]
        for r in range(slots):
            live = jnp.logical_and(lane >= r * IDX_DIM, lane < (r + 1) * IDX_DIM)
            qcat_ref[c * slots + r] = jnp.where(live, col, 0.0).astype(BF16)
    wt = idxq_ref[:, 2 * LANES:].T[0:IDX_HEADS, :]

    def score_body(j, carry):
        kc = kcat_ref[pl.ds(pl.multiple_of(j * ck, ck), ck), :]
        acc = jnp.zeros((ck, rows), F32)
        for hd in range(IDX_HEADS):
            y = _dot_nt(kc, qcat_ref[hd])
            acc = acc + wt[hd:hd + 1, :] * jnp.maximum(y, 0.0)
        sc_ref[j] = jnp.where(j * ck + krow <= qidx, acc, NEG)
        return carry

    lax.fori_loop(0, nch, score_body, 0)

    n_virtual = (seq - nch * ck).astype(F32)

    def count(pred, thr):
        def body(j, accs):
            accs = list(accs)
            for g in range(ngrp):
                hit = pred(sc_ref[j, g * SUBLANES:(g + 1) * SUBLANES, :], thr)
                accs[g % len(accs)] = accs[g % len(accs)] + jnp.where(hit, 1.0, 0.0)
            return tuple(accs)
        zero = jnp.zeros((SUBLANES, rows), F32)
        accs = lax.fori_loop(0, nch, body, (zero, zero, zero, zero))
        tot = _fold_sublanes((accs[0] + accs[1]) + (accs[2] + accs[3]), jnp.add)
        return tot + jnp.where(pred(jnp.float32(NEG), thr), n_virtual, 0.0)

    ge = lambda a, b: a >= b
    gt = lambda a, b: a > b
    kf = jnp.float32(topk)

    c0 = count(ge, jnp.zeros((SUBLANES, rows), F32))
    key = jnp.where(c0 >= kf, jnp.int32(0), jnp.int32(-2 ** 31))

    def bit_body(i, key):
        cand = key | jnp.left_shift(jnp.int32(1), 30 - i)
        cnt = count(ge, _key_to_float(cand))
        return jnp.where(cnt >= kf, cand, key)

    key = lax.fori_loop(0, 31, bit_body, key)
    thr8 = _key_to_float(key)
    thr = thr8[0:1, :]
    need = (kf - count(gt, thr8))[0:1, :]

    ki = lax.broadcasted_iota(jnp.int32, (ck, ck), 0)
    kj = lax.broadcasted_iota(jnp.int32, (ck, ck), 1)
    lower = jnp.where(kj < ki, 1.0, 0.0).astype(BF16)
    ones8 = jnp.ones((SUBLANES, ck), BF16)

    def select_body(j, carry):
        run, ms = carry
        sc = sc_ref[j]
        eq = sc == thr
        eqb = jnp.where(eq, 1.0, 0.0).astype(BF16)
        prior = run[0:1, :] + _dot(lower, eqb)
        tie = jnp.where(prior < need, 0.0, NEG)
        bias = jnp.where(sc > thr, 0.0, jnp.where(eq, tie, NEG))
        sc_ref[j] = jnp.where(j * ck + krow <= qidx, bias, NEG)
        kc = k_ref[pl.ds(pl.multiple_of(j * ck, ck), ck), :]
        out = []
        for hd in range(N_Q_HEADS):
            s = _dot_nt(kc, q_ref[:, hd * LANES:(hd + 1) * LANES]) + sc_ref[j]
            lg_ref[hd, j] = s
            out.append(jnp.maximum(ms[hd], _fold_groups(s, jnp.maximum)))
        return run + _dot(ones8, eqb), tuple(out)

    _, ms = lax.fori_loop(0, nch, select_body,
                          (jnp.zeros((SUBLANES, rows), F32),
                           tuple(jnp.full((SUBLANES, rows), -jnp.inf, F32) for _ in range(N_Q_HEADS))))
    ms = [_fold_sublanes(m, jnp.maximum)[0:1, :] for m in ms]

    acc_ref[...] = jnp.zeros(acc_ref.shape, F32)

    def pv_body(j, ls):
        out = []
        for hd in range(N_Q_HEADS):
            p = jnp.exp(lg_ref[hd, j] - ms[hd])
            out.append(ls[hd] + _fold_groups(p, jnp.add))
            p_ref[hd] = p.astype(BF16)
        vt = vt_ref[j]
        for hd in range(N_Q_HEADS):
            acc_ref[hd] += _dot(vt, p_ref[hd])
        return tuple(out)

    ls = lax.fori_loop(0, nch, pv_body, tuple(jnp.zeros((SUBLANES, rows), F32) for _ in range(N_Q_HEADS)))

    outs = []
    for hd in range(N_Q_HEADS):
        l = _fold_sublanes(ls[hd], jnp.add)
        r0 = HEAD_DIM * (hd // Q_PER_KV)
        outs.append(acc_ref[hd, r0:r0 + HEAD_DIM, :] / l[0:1, :])
    out_ref[...] = jnp.concatenate(outs, axis=0).T.astype(BF16)


def _dsa(q, idx, kcat, k, v, batch, seq):
    rows = DSA_ROWS
    nq = seq // rows
    topk = min(TOPK_MAX, seq // 4)
    assert topk <= rows and seq % rows == 0
    q3 = q.reshape(batch, seq, QPAD)
    idx3 = idx.reshape(batch, seq, IDXW)
    vt = v.reshape(batch, nq, rows, LANES).transpose(0, 1, 3, 2)
    full = pl.BlockSpec((None, seq, LANES), lambda b, i: (b, 0, 0))
    kern = functools.partial(_dsa_kernel, seq=seq, topk=topk)
    return pl.pallas_call(
        kern,
        grid=(batch, nq),
        in_specs=[pl.BlockSpec((None, rows, QPAD), lambda b, i: (b, i, 0)),
                  pl.BlockSpec((None, rows, IDXW), lambda b, i: (b, i, 0)),
                  full, full,
                  pl.BlockSpec((None, nq, LANES, rows), lambda b, i: (b, 0, 0, 0))],
        out_specs=pl.BlockSpec((None, rows, N_Q_HEADS * HEAD_DIM), lambda b, i: (b, i, 0)),
        out_shape=jax.ShapeDtypeStruct((batch, seq, N_Q_HEADS * HEAD_DIM), BF16),
        scratch_shapes=[pltpu.VMEM((nq, rows, rows), F32),
                        pltpu.VMEM((IDX_HEADS, rows, LANES), BF16),
                        pltpu.VMEM((N_Q_HEADS, LANES, rows), F32),
                        pltpu.VMEM((N_Q_HEADS, rows, rows), BF16),
                        pltpu.VMEM((N_Q_HEADS, nq, rows, rows), F32)],
        compiler_params=pltpu.CompilerParams(dimension_semantics=("arbitrary", "arbitrary"),
                                             vmem_limit_bytes=48 * 1024 * 1024),
        name="dsa",
    )(q3, idx3, kcat.reshape(batch, seq, LANES), k.reshape(batch, seq, LANES), vt)


def _mixout_kernel(pool_ref, prev_ref, b_ref, x_ref, pw_ref, ps_ref, woa_ref, wob_ref, out_ref):
    rows = MIX_ROWS
    ti = pl.program_id(1)
    prev = jnp.where(ti == 0, 0.0, prev_ref[...])
    ext = jnp.concatenate([prev, pool_ref[...]], axis=0)
    t = ti * rows + lax.broadcasted_iota(jnp.int32, (rows, 1), 0)
    parts = []
    for g, w in enumerate(POOL_WINDOWS):
        sl = slice(g * POOL_GROUP, (g + 1) * POOL_GROUP)
        a = ext[:, sl]
        span = 1
        while span < w:
            a = a + pltpu.roll(a, span, 0)
            span *= 2
        cnt = jnp.minimum(t + 1, w).astype(F32)
        p = a[POOL_HALO:] / cnt - ext[POOL_HALO:, sl]
        y = _dot(p.astype(BF16), pw_ref[g])
        parts.append(y * ps_ref[:, sl])
    a_out = jnp.concatenate(parts, axis=1).astype(BF16)
    out_ref[...] = x_ref[...] + _dot(a_out, woa_ref[...]) + _dot(b_ref[...], wob_ref[...])


def _mixout(pool, b_out, x2, pw, ps, woa, wob, batch, seq):
    rows = MIX_ROWS
    per = seq // rows
    hb = rows // POOL_HALO
    tile = lambda w: pl.BlockSpec((None, rows, w), lambda b, i: (b, i, 0))
    const = lambda a: pl.BlockSpec(a.shape, lambda b, i: (0,) * a.ndim)
    return pl.pallas_call(
        _mixout_kernel,
        grid=(batch, per),
        in_specs=[tile(POOL_WIDTH),
                  pl.BlockSpec((None, POOL_HALO, POOL_WIDTH), lambda b, i: (b, jnp.maximum(i * hb - 1, 0), 0)),
                  tile(POOL_WIDTH), tile(D_MODEL), const(pw), const(ps), const(woa), const(wob)],
        out_specs=tile(D_MODEL),
        out_shape=jax.ShapeDtypeStruct((batch, seq, D_MODEL), F32),
        compiler_params=pltpu.CompilerParams(dimension_semantics=("arbitrary", "arbitrary"),
                                             vmem_limit_bytes=48 * 1024 * 1024),
        name="mixout",
    )(pool.reshape(batch, seq, POOL_WIDTH), pool.reshape(batch, seq, POOL_WIDTH), b_out,
      x2.reshape(batch, seq, D_MODEL), pw, ps, woa, wob)


def _ffn_kernel(x_ref, prev_ref, g2_ref, wup_ref, cw_ref, cb_ref, wdn_ref, out_ref, act_ref):
    rows = FFN_ROWS
    ti = pl.program_id(1)
    g2 = g2_ref[...]

    def norm(v):
        return v * lax.rsqrt(jnp.mean(v * v, axis=-1, keepdims=True) + EPS) * g2

    x = x_ref[...]
    hp = jnp.where(ti == 0, 0.0, norm(prev_ref[...]))
    h_ext = jnp.concatenate([hp, norm(x)], axis=0).astype(BF16)

    chunk = 2 * LANES
    for c in range(D_FF // chunk):
        halves = []
        for off in (0, D_FF):
            sl = slice(off + c * chunk, off + (c + 1) * chunk)
            u = _dot(h_ext, wup_ref[:, sl])
            u1 = pltpu.roll(u, 1, 0)
            u2 = pltpu.roll(u, 2, 0)
            cv = cb_ref[:, sl] + u2[FFN_HALO:] * cw_ref[0:1, sl]
            cv = cv + u1[FFN_HALO:] * cw_ref[1:2, sl]
            cv = cv + u[FFN_HALO:] * cw_ref[2:3, sl]
            halves.append(cv)
        gate, val = halves
        act = gate * (1.0 / (1.0 + jnp.exp(-gate))) * val
        act_ref[:, c * chunk:(c + 1) * chunk] = act.astype(BF16)
    out_ref[...] = x + _dot(act_ref[...], wdn_ref[...])


def _ffn(x1, g2, wup, cw, cb, wdn, batch, seq):
    rows = FFN_ROWS
    per = seq // rows
    hb = rows // FFN_HALO
    tile = pl.BlockSpec((None, rows, D_MODEL), lambda b, i: (b, i, 0))
    const = lambda a: pl.BlockSpec(a.shape, lambda b, i: (0,) * a.ndim, pipeline_mode=pl.Buffered(1))
    return pl.pallas_call(
        _ffn_kernel,
        grid=(batch, per),
        in_specs=[tile,
                  pl.BlockSpec((None, FFN_HALO, D_MODEL), lambda b, i: (b, jnp.maximum(i * hb - 1, 0), 0)),
                  const(g2), const(wup), const(cw), const(cb), const(wdn)],
        out_specs=tile,
        out_shape=jax.ShapeDtypeStruct((batch, seq, D_MODEL), F32),
        scratch_shapes=[pltpu.VMEM((rows, D_FF), BF16)],
        compiler_params=pltpu.CompilerParams(dimension_semantics=("arbitrary", "arbitrary"),
                                             vmem_limit_bytes=56 * 1024 * 1024),
        name="ffn",
    )(x1, x1, g2, wup, cw, cb, wdn)


def _rope_tables(seq, dim):
    half = dim // 2
    inv = jnp.exp(-jnp.log(jnp.float32(ROPE_THETA)) * jnp.arange(half, dtype=F32) / half)
    ang = jnp.arange(seq).astype(F32)[:, None] * inv[None, :]
    cos, sin = jnp.cos(ang), jnp.sin(ang)
    return jnp.concatenate([cos, cos], axis=1), jnp.concatenate([-sin, sin], axis=1)


def _layer(x, norm1_g, w_in, q_norm_g, k_norm_g, pool_w, pool_scale, w_out, norm2_g, w_up, conv_w, conv_b, w_down):
    batch, seq, _ = x.shape
    n = batch * seq
    wmain = w_in[:, :MAIN_W].astype(BF16)
    w_qi, w_ki, w_wi = jnp.split(w_in[:, MAIN_W:], [IDX_HEADS * IDX_DIM, (IDX_HEADS + 1) * IDX_DIM], axis=1)
    qg = jnp.concatenate([q_norm_g, q_norm_g])[None, :]
    kg = jnp.concatenate([k_norm_g, k_norm_g])[None, :]
    slots = LANES // IDX_DIM
    w_idx = jnp.concatenate([w_qi, w_wi, jnp.zeros((D_MODEL, LANES - IDX_HEADS), F32)] + [w_ki] * slots, axis=1)

    c64, s64 = _rope_tables(seq, HEAD_DIM)
    qcos, qsin = jnp.tile(c64, (1, LANES // HEAD_DIM)), jnp.tile(s64, (1, LANES // HEAD_DIM))
    c32, s32 = _rope_tables(seq, IDX_DIM)
    wscale = jnp.full((seq, IDX_HEADS), IDX_HEADS ** -0.5 * IDX_DIM ** -0.5, F32)
    gap = jnp.zeros((seq, LANES - IDX_HEADS), F32)
    icos = jnp.concatenate([jnp.tile(c32, (1, IDX_HEADS)), wscale, gap, jnp.tile(c32, (1, slots))], axis=1)
    isin = jnp.concatenate([jnp.tile(s32, (1, IDX_HEADS)), jnp.zeros((seq, LANES), F32),
                            jnp.tile(s32, (1, slots))], axis=1)

    x2 = x.reshape(n, D_MODEL)
    pool, q, k, v, idx, kcat = _inproj(x2, norm1_g[None, :], wmain, w_idx.astype(BF16), qg, kg, qcos, qsin, icos, isin, seq)
    b_out = _dsa(q, idx, kcat, k, v, batch, seq)
    x1 = _mixout(pool, b_out, x2, pool_w.astype(BF16), pool_scale[None, :],
                 w_out[:POOL_WIDTH].astype(BF16), w_out[POOL_WIDTH:].astype(BF16), batch, seq)
    return _ffn(x1, norm2_g[None, :], w_up.astype(BF16), conv_w, conv_b[None, :], w_down.astype(BF16), batch, seq)


def kernel(x, norm1_g, w_in, q_norm_g, k_norm_g, pool_w, pool_scale, w_out, norm2_g, w_up, conv_w, conv_b, w_down):
    for l in range(norm1_g.shape[0]):
        x = _layer(x, norm1_g[l], w_in[l], q_norm_g[l], k_norm_g[l], pool_w[l], pool_scale[l], w_out[l],
                   norm2_g[l], w_up[l], conv_w[l], conv_b[l], w_down[l])
    return x
```

```python
import functools

import jax
import jax.numpy as jnp
from jax import lax
from jax.experimental import pallas as pl
from jax.experimental.pallas import tpu as pltpu

F32 = jnp.float32
BF16 = jnp.bfloat16

D_MODEL = 1024
POOL_WIDTH = 512
POOL_WINDOWS = (2, 4, 8, 16)
POOL_GROUP = 128
HEAD_DIM = 64
N_Q_HEADS = 8
N_KV_HEADS = 2
Q_PER_KV = 4
IDX_HEADS = 8
IDX_DIM = 32
TOPK_MAX = 256
ROPE_THETA = 10000.0
D_FF = 2816
CONV_WIDTH = 3
EPS = 1e-6
NEG = -1e30

LANES = 128
SUBLANES = 8
QPAD = N_Q_HEADS * LANES
IDXW = 3 * LANES
IDX_PROJ_W = IDXW + LANES
MAIN_W = POOL_WIDTH + (N_Q_HEADS + 2 * N_KV_HEADS) * HEAD_DIM

INPROJ_ROWS = 512
DSA_ROWS = 256
FFN_ROWS = 512
MIX_HALO = 16
POOL_BACK = 32

Q_SCALE = HEAD_DIM ** -0.5 * 1.4426950408889634

NT_DIMS = (((1,), (1,)), ((), ()))


def _dot(a, b):
    return jnp.dot(a, b, preferred_element_type=F32)


def _dot_nt(a, b):
    return lax.dot_general(a, b, NT_DIMS, preferred_element_type=F32)


def _swap_halves(xc, half):
    lane = lax.broadcasted_iota(jnp.int32, xc.shape, 1)
    first = (lane % (2 * half)) < half
    return jnp.where(first, pltpu.roll(xc, LANES - half, 1), pltpu.roll(xc, half, 1))


def _inproj_kernel(x_ref, g1_ref, wmain_ref, widx_ref, qg_ref, kg_ref,
                   qcos_ref, qsin_ref, icos_ref, isin_ref,
                   pool_ref, q_ref, k_ref, v_ref, idx_ref, kcat_ref):
    x = x_ref[...]
    ms = jnp.mean(x * x, axis=-1, keepdims=True)
    h = x * lax.rsqrt(ms + EPS) * g1_ref[...]
    h_hi = h.astype(BF16)

    qcos = qcos_ref[...]
    qsin = qsin_ref[...]
    lane = lax.broadcasted_iota(jnp.int32, (x.shape[0], LANES), 1)
    first = lane < HEAD_DIM

    def norm_rope(xc, gain):
        sq = xc * xc
        s0 = jnp.sum(jnp.where(first, sq, 0.0), axis=-1, keepdims=True)
        s1 = jnp.sum(jnp.where(first, 0.0, sq), axis=-1, keepdims=True)
        ss = jnp.where(first, s0, s1) * (1.0 / HEAD_DIM)
        xn = xc * lax.rsqrt(ss + EPS) * gain
        return xn * qcos + _swap_halves(xn, HEAD_DIM // 2) * qsin

    n_qcols = N_Q_HEADS * HEAD_DIM // LANES
    k0 = POOL_WIDTH + n_qcols * LANES
    qk = _dot(h_hi, wmain_ref[:, POOL_WIDTH:k0 + LANES])
    idxp = _dot(h_hi, widx_ref[...])

    for c in range(n_qcols):
        qr = norm_rope(qk[:, c * LANES:(c + 1) * LANES], qg_ref[...]) * Q_SCALE
        qs = pltpu.roll(qr, HEAD_DIM, 1)
        if (2 * c) // Q_PER_KV == 0:
            pair = (jnp.where(first, qr, 0.0), jnp.where(first, qs, 0.0))
        else:
            pair = (jnp.where(first, 0.0, qs), jnp.where(first, 0.0, qr))
        for i, qh in enumerate(pair):
            q_ref[:, (2 * c + i) * LANES:(2 * c + i + 1) * LANES] = qh.astype(BF16)
    k_ref[...] = norm_rope(qk[:, n_qcols * LANES:], kg_ref[...]).astype(BF16)

    for c in range(IDX_PROJ_W // LANES):
        sl = slice(c * LANES, (c + 1) * LANES)
        ic = idxp[:, sl]
        roped = ic * icos_ref[:, sl] + _swap_halves(ic, IDX_DIM // 2) * isin_ref[:, sl]
        if c < IDXW // LANES:
            idx_ref[:, sl] = roped
        else:
            kcat_ref[...] = roped.astype(BF16)

    pool_ref[...] = _dot(h_hi, wmain_ref[:, :POOL_WIDTH])
    v_ref[...] = _dot(h_hi, wmain_ref[:, k0 + LANES:]).astype(BF16)


def _inproj(x2, g1, wmain, widx, qg, kg, qcos, qsin, icos, isin, seq):
    n = x2.shape[0]
    rows = INPROJ_ROWS
    tper = seq // rows
    row = lambda w: pl.BlockSpec((rows, w), lambda i: (i, 0))
    const = lambda a: pl.BlockSpec(a.shape, lambda i: (0, 0))
    table = lambda w: pl.BlockSpec((rows, w), lambda i: (i % tper, 0))
    return pl.pallas_call(
        _inproj_kernel,
        grid=(n // rows,),
        in_specs=[row(D_MODEL), const(g1), const(wmain), const(widx), const(qg), const(kg),
                  table(LANES), table(LANES), table(IDX_PROJ_W), table(IDX_PROJ_W)],
        out_specs=[row(POOL_WIDTH), row(QPAD), row(LANES), row(LANES), row(IDXW), row(LANES)],
        out_shape=[jax.ShapeDtypeStruct((n, POOL_WIDTH), F32),
                   jax.ShapeDtypeStruct((n, QPAD), BF16),
                   jax.ShapeDtypeStruct((n, LANES), BF16),
                   jax.ShapeDtypeStruct((n, LANES), BF16),
                   jax.ShapeDtypeStruct((n, IDXW), F32),
                   jax.ShapeDtypeStruct((n, LANES), BF16)],
        compiler_params=pltpu.CompilerParams(dimension_semantics=("arbitrary",),
                                             vmem_limit_bytes=48 * 1024 * 1024),
        name="inproj",
    )(x2, g1, wmain, widx, qg, kg, qcos, qsin, icos, isin)


def _key_to_float(key):
    bits = key ^ ((key >> 31) & jnp.int32(0x7FFFFFFF))
    return lax.bitcast_convert_type(bits, F32)


def _fold_groups(x, op, ways=4):
    n = x.shape[0] // SUBLANES
    accs = [x[g * SUBLANES:(g + 1) * SUBLANES, :] for g in range(ways)]
    for g in range(ways, n):
        accs[g % ways] = op(accs[g % ways], x[g * SUBLANES:(g + 1) * SUBLANES, :])
    while len(accs) > 1:
        accs = [op(accs[i], accs[i + 1]) for i in range(0, len(accs), 2)]
    return accs[0]


def _fold_sublanes(x, op):
    for shift in (4, 2, 1):
        x = op(x, pltpu.roll(x, shift, 0))
    return x


def _dsa_kernel(q_ref, idxq_ref, kcat_ref, k_ref, vt_ref, out_ref, sc_ref, qcat_ref, acc_ref, p_ref, lg_ref,
                *, seq, topk):
    rows = DSA_ROWS
    ck = DSA_ROWS
    ngrp = ck // SUBLANES
    qb = pl.program_id(1)
    nch = qb + 1
    lane = lax.broadcasted_iota(jnp.int32, (rows, LANES), 1)
    qidx = qb * rows + lax.broadcasted_iota(jnp.int32, (ck, rows), 1)
    krow = lax.broadcasted_iota(jnp.int32, (ck, rows), 0)

    slots = LANES // IDX_DIM
    for c in range(IDX_HEADS // slots):
        col = idxq_ref[:, c * LANES:(c + 1) * LANES]
        for r in range(slots):
            live = jnp.logical_and(lane >= r * IDX_DIM, lane < (r + 1) * IDX_DIM)
            qcat_ref[c * slots + r] = jnp.where(live, col, 0.0).astype(BF16)
    wt = idxq_ref[:, 2 * LANES:].T[0:IDX_HEADS, :]

    def score_body(j, nonneg):
        kc = kcat_ref[pl.ds(pl.multiple_of(j * ck, ck), ck), :]
        acc = jnp.zeros((ck, rows), F32)
        for hd in range(IDX_HEADS):
            y = _dot_nt(kc, qcat_ref[hd])
            acc = acc + wt[hd:hd + 1, :] * jnp.maximum(y, 0.0)
        sc = jnp.where(j * ck + krow <= qidx, acc, NEG)
        sc_ref[j] = sc
        return nonneg + _fold_groups(jnp.where(sc >= 0.0, 1.0, 0.0), jnp.add)

    nonneg = lax.fori_loop(0, nch, score_body, jnp.zeros((SUBLANES, rows), F32))

    n_virtual = (seq - nch * ck).astype(F32)

    def count(pred, thr):
        def body(j, accs):
            accs = list(accs)
            for g in range(ngrp):
                hit = pred(sc_ref[j, g * SUBLANES:(g + 1) * SUBLANES, :], thr)
                accs[g % len(accs)] = accs[g % len(accs)] + jnp.where(hit, 1.0, 0.0)
            return tuple(accs)
        zero = jnp.zeros((SUBLANES, rows), F32)
        accs = lax.fori_loop(0, nch, body, (zero, zero, zero, zero))
        tot = _fold_sublanes((accs[0] + accs[1]) + (accs[2] + accs[3]), jnp.add)
        return tot + jnp.where(pred(jnp.float32(NEG), thr), n_virtual, 0.0)

    ge = lambda a, b: a >= b
    gt = lambda a, b: a > b
    kf = jnp.float32(topk)

    key = jnp.where(_fold_sublanes(nonneg, jnp.add) >= kf, jnp.int32(0), jnp.int32(-2 ** 31))

    def bit_body(i, key):
        cand = key | jnp.left_shift(jnp.int32(1), 30 - i)
        cnt = count(ge, _key_to_float(cand))
        return jnp.where(cnt >= kf, cand, key)

    key = lax.fori_loop(0, 31, bit_body, key)
    thr8 = _key_to_float(key)
    thr = thr8[0:1, :]
    need = (kf - count(gt, thr8))[0:1, :]

    ki = lax.broadcasted_iota(jnp.int32, (ck, ck), 0)
    kj = lax.broadcasted_iota(jnp.int32, (ck, ck), 1)
    lower = jnp.where(kj < ki, 1.0, 0.0).astype(BF16)
    ones8 = jnp.ones((SUBLANES, ck), BF16)

    def select_body(j, carry):
        run, ms = carry
        sc = sc_ref[j]
        eq = sc == thr
        eqb = jnp.where(eq, 1.0, 0.0).astype(BF16)
        prior = run[0:1, :] + _dot(lower, eqb)
        tie = jnp.where(prior < need, 0.0, NEG)
        bias = jnp.where(sc > thr, 0.0, jnp.where(eq, tie, NEG))
        sc_ref[j] = jnp.where(j * ck + krow <= qidx, bias, NEG)
        kc = k_ref[pl.ds(pl.multiple_of(j * ck, ck), ck), :]
        out = []
        for hd in range(N_Q_HEADS):
            s = _dot_nt(kc, q_ref[:, hd * LANES:(hd + 1) * LANES]) + sc_ref[j]
            lg_ref[hd, j] = s
            out.append(jnp.maximum(ms[hd], _fold_groups(s, jnp.maximum)))
        return run + _dot(ones8, eqb), tuple(out)

    _, ms = lax.fori_loop(0, nch, select_body,
                          (jnp.zeros((SUBLANES, rows), F32),
                           tuple(jnp.full((SUBLANES, rows), -jnp.inf, F32) for _ in range(N_Q_HEADS))))
    ms = [_fold_sublanes(m, jnp.maximum)[0:1, :] for m in ms]

    acc_ref[...] = jnp.zeros(acc_ref.shape, F32)

    def pv_body(j, ls):
        out = []
        for hd in range(N_Q_HEADS):
            p = jnp.exp2(lg_ref[hd, j] - ms[hd])
            out.append(ls[hd] + _fold_groups(p, jnp.add))
            p_ref[hd] = p.astype(BF16)
        vt = vt_ref[j]
        for hd in range(N_Q_HEADS):
            acc_ref[hd] += _dot(vt, p_ref[hd])
        return tuple(out)

    ls = lax.fori_loop(0, nch, pv_body, tuple(jnp.zeros((SUBLANES, rows), F32) for _ in range(N_Q_HEADS)))

    outs = []
    for hd in range(N_Q_HEADS):
        l = _fold_sublanes(ls[hd], jnp.add)
        r0 = HEAD_DIM * (hd // Q_PER_KV)
        outs.append(acc_ref[hd, r0:r0 + HEAD_DIM, :] / l[0:1, :])
    out_ref[...] = jnp.concatenate(outs, axis=0).T.astype(BF16)


def _dsa(q, idx, kcat, k, v, batch, seq):
    rows = DSA_ROWS
    nq = seq // rows
    topk = min(TOPK_MAX, seq // 4)
    assert topk <= rows and seq % rows == 0
    q3 = q.reshape(batch, seq, QPAD)
    idx3 = idx.reshape(batch, seq, IDXW)
    vt = v.reshape(batch, nq, rows, LANES).transpose(0, 1, 3, 2)
    full = pl.BlockSpec((None, seq, LANES), lambda b, i: (b, 0, 0))
    kern = functools.partial(_dsa_kernel, seq=seq, topk=topk)
    return pl.pallas_call(
        kern,
        grid=(batch, nq),
        in_specs=[pl.BlockSpec((None, rows, QPAD), lambda b, i: (b, i, 0)),
                  pl.BlockSpec((None, rows, IDXW), lambda b, i: (b, i, 0)),
                  full, full,
                  pl.BlockSpec((None, nq, LANES, rows), lambda b, i: (b, 0, 0, 0))],
        out_specs=pl.BlockSpec((None, rows, N_Q_HEADS * HEAD_DIM), lambda b, i: (b, i, 0)),
        out_shape=jax.ShapeDtypeStruct((batch, seq, N_Q_HEADS * HEAD_DIM), BF16),
        scratch_shapes=[pltpu.VMEM((nq, rows, rows), F32),
                        pltpu.VMEM((IDX_HEADS, rows, LANES), BF16),
                        pltpu.VMEM((N_Q_HEADS, LANES, rows), F32),
                        pltpu.VMEM((N_Q_HEADS, rows, rows), BF16),
                        pltpu.VMEM((N_Q_HEADS, nq, rows, rows), F32)],
        compiler_params=pltpu.CompilerParams(dimension_semantics=("arbitrary", "arbitrary"),
                                             vmem_limit_bytes=48 * 1024 * 1024),
        name="dsa",
    )(q3, idx3, kcat.reshape(batch, seq, LANES), k.reshape(batch, seq, LANES), vt)


def _mixffn_kernel(x_ref, xp_ref, pool_ref, pp_ref, b_ref, bp_ref, pw_ref, ps_ref, woa_ref, wob_ref,
                   g2_ref, wup_ref, cw_ref, cb_ref, wdn_ref, out_ref, act_ref):
    rows = FFN_ROWS
    ext = MIX_HALO + rows
    ti = pl.program_id(1)
    row = lax.broadcasted_iota(jnp.int32, (ext, 1), 0)
    t = ti * rows - MIX_HALO + row

    pool_ext = jnp.concatenate([jnp.where(ti == 0, 0.0, pp_ref[...]), pool_ref[...]], axis=0)
    lead = POOL_BACK - MIX_HALO
    parts = []
    for g, w in enumerate(POOL_WINDOWS):
        sl = slice(g * POOL_GROUP, (g + 1) * POOL_GROUP)
        a = pool_ext[:, sl]
        span = 1
        while span < w:
            a = a + pltpu.roll(a, span, 0)
            span *= 2
        cnt = jnp.clip(t + 1, 1, w).astype(F32)
        p = a[lead:] / cnt - pool_ext[lead:, sl]
        parts.append(_dot(p.astype(BF16), pw_ref[g]) * ps_ref[:, sl])
    a_out = jnp.concatenate(parts, axis=1).astype(BF16)

    x_ext = jnp.concatenate([xp_ref[...], x_ref[...]], axis=0)
    b_ext = jnp.concatenate([bp_ref[...], b_ref[...]], axis=0)
    x1 = x_ext + _dot(a_out, woa_ref[...]) + _dot(b_ext, wob_ref[...])

    h = x1 * lax.rsqrt(jnp.mean(x1 * x1, axis=-1, keepdims=True) + EPS) * g2_ref[...]
    h_ext = jnp.where(jnp.logical_and(ti == 0, row < MIX_HALO), 0.0, h).astype(BF16)

    chunk = 2 * LANES
    for c in range(D_FF // chunk):
        halves = []
        for off in (0, D_FF):
            sl = slice(off + c * chunk, off + (c + 1) * chunk)
            u = _dot(h_ext, wup_ref[:, sl])
            u1 = pltpu.roll(u, 1, 0)
            u2 = pltpu.roll(u, 2, 0)
            cv = cb_ref[:, sl] + u2[MIX_HALO:] * cw_ref[0:1, sl]
            cv = cv + u1[MIX_HALO:] * cw_ref[1:2, sl]
            cv = cv + u[MIX_HALO:] * cw_ref[2:3, sl]
            halves.append(cv)
        gate, val = halves
        act = gate * (1.0 / (1.0 + jnp.exp(-gate))) * val
        act_ref[:, c * chunk:(c + 1) * chunk] = act.astype(BF16)
    out_ref[...] = x1[MIX_HALO:] + _dot(act_ref[...], wdn_ref[...])


def _mixffn(x3, pool3, b_out, pw, ps, woa, wob, g2, wup, cw, cb, wdn):
    batch, seq, _ = x3.shape
    rows = FFN_ROWS
    tile = lambda w: pl.BlockSpec((None, rows, w), lambda b, i: (b, i, 0))
    back = lambda h, w: pl.BlockSpec((None, h, w), lambda b, i: (b, jnp.maximum(i * (rows // h) - 1, 0), 0))
    const = lambda a: pl.BlockSpec(a.shape, lambda b, i: (0,) * a.ndim, pipeline_mode=pl.Buffered(1))
    return pl.pallas_call(
        _mixffn_kernel,
        grid=(batch, seq // rows),
        in_specs=[tile(D_MODEL), back(MIX_HALO, D_MODEL), tile(POOL_WIDTH), back(POOL_BACK, POOL_WIDTH),
                  tile(POOL_WIDTH), back(MIX_HALO, POOL_WIDTH),
                  const(pw), const(ps), const(woa), const(wob), const(g2), const(wup), const(cw), const(cb),
                  const(wdn)],
        out_specs=tile(D_MODEL),
        out_shape=jax.ShapeDtypeStruct((batch, seq, D_MODEL), F32),
        scratch_shapes=[pltpu.VMEM((rows, D_FF), BF16)],
        compiler_params=pltpu.CompilerParams(dimension_semantics=("arbitrary", "arbitrary"),
                                             vmem_limit_bytes=56 * 1024 * 1024),
        name="mixffn",
    )(x3, x3, pool3, pool3, b_out, b_out, pw, ps, woa, wob, g2, wup, cw, cb, wdn)


def _rope_tables(seq, dim):
    half = dim // 2
    inv = jnp.exp(-jnp.log(jnp.float32(ROPE_THETA)) * jnp.arange(half, dtype=F32) / half)
    ang = jnp.arange(seq).astype(F32)[:, None] * inv[None, :]
    cos, sin = jnp.cos(ang), jnp.sin(ang)
    return jnp.concatenate([cos, cos], axis=1), jnp.concatenate([-sin, sin], axis=1)


def _layer(x, norm1_g, w_in, q_norm_g, k_norm_g, pool_w, pool_scale, w_out, norm2_g, w_up, conv_w, conv_b, w_down):
    batch, seq, _ = x.shape
    n = batch * seq
    wmain = w_in[:, :MAIN_W].astype(BF16)
    w_qi, w_ki, w_wi = jnp.split(w_in[:, MAIN_W:], [IDX_HEADS * IDX_DIM, (IDX_HEADS + 1) * IDX_DIM], axis=1)
    qg = jnp.concatenate([q_norm_g, q_norm_g])[None, :]
    kg = jnp.concatenate([k_norm_g, k_norm_g])[None, :]
    slots = LANES // IDX_DIM
    w_idx = jnp.concatenate([w_qi, w_wi, jnp.zeros((D_MODEL, LANES - IDX_HEADS), F32)] + [w_ki] * slots, axis=1)

    c64, s64 = _rope_tables(seq, HEAD_DIM)
    qcos, qsin = jnp.tile(c64, (1, LANES // HEAD_DIM)), jnp.tile(s64, (1, LANES // HEAD_DIM))
    c32, s32 = _rope_tables(seq, IDX_DIM)
    wscale = jnp.full((seq, IDX_HEADS), IDX_HEADS ** -0.5 * IDX_DIM ** -0.5, F32)
    gap = jnp.zeros((seq, LANES - IDX_HEADS), F32)
    icos = jnp.concatenate([jnp.tile(c32, (1, IDX_HEADS)), wscale, gap, jnp.tile(c32, (1, slots))], axis=1)
    isin = jnp.concatenate([jnp.tile(s32, (1, IDX_HEADS)), jnp.zeros((seq, LANES), F32),
                            jnp.tile(s32, (1, slots))], axis=1)

    x2 = x.reshape(n, D_MODEL)
    pool, q, k, v, idx, kcat = _inproj(x2, norm1_g[None, :], wmain, w_idx.astype(BF16), qg, kg, qcos, qsin, icos, isin, seq)
    b_out = _dsa(q, idx, kcat, k, v, batch, seq)
    return _mixffn(x, pool.reshape(batch, seq, POOL_WIDTH), b_out, pool_w.astype(BF16), pool_scale[None, :],
                   w_out[:POOL_WIDTH].astype(BF16), w_out[POOL_WIDTH:].astype(BF16), norm2_g[None, :],
                   w_up.astype(BF16), conv_w, conv_b[None, :], w_down.astype(BF16))


def kernel(x, norm1_g, w_in, q_norm_g, k_norm_g, pool_w, pool_scale, w_out, norm2_g, w_up, conv_w, conv_b, w_down):
    for l in range(norm1_g.shape[0]):
        x = _layer(x, norm1_g[l], w_in[l], q_norm_g[l], k_norm_g[l], pool_w[l], pool_scale[l], w_out[l],
                   norm2_g[l], w_up[l], conv_w[l], conv_b[l], w_down[l])
    return x
```

```python
import functools

import jax
import jax.numpy as jnp
from jax import lax
from jax.experimental import pallas as pl
from jax.experimental.pallas import tpu as pltpu

F32 = jnp.float32
BF16 = jnp.bfloat16

D_MODEL = 1024
POOL_WIDTH = 512
POOL_WINDOWS = (2, 4, 8, 16)
POOL_GROUP = 128
HEAD_DIM = 64
N_Q_HEADS = 8
N_KV_HEADS = 2
Q_PER_KV = 4
IDX_HEADS = 8
IDX_DIM = 32
TOPK_MAX = 256
ROPE_THETA = 10000.0
D_FF = 2816
CONV_WIDTH = 3
EPS = 1e-6
NEG = -1e30

LANES = 128
SUBLANES = 8
QPAD = N_Q_HEADS * LANES
IDXW = 3 * LANES
IDX_PROJ_W = IDXW + LANES
MAIN_W = POOL_WIDTH + (N_Q_HEADS + 2 * N_KV_HEADS) * HEAD_DIM

INPROJ_ROWS = 512
DSA_ROWS = 256
FFN_ROWS = 512
MIX_HALO = 16
POOL_BACK = 32

Q_SCALE = HEAD_DIM ** -0.5 * 1.4426950408889634

NT_DIMS = (((1,), (1,)), ((), ()))


def _dot(a, b):
    return jnp.dot(a, b, preferred_element_type=F32)


def _dot_nt(a, b):
    return lax.dot_general(a, b, NT_DIMS, preferred_element_type=F32)


def _rot_partner(xc):
    return pltpu.roll(xc, LANES // 2, 1)


def _split_halves(w, heads, dim):
    per = LANES // dim
    w = w.reshape(w.shape[0], heads // per, per, 2, dim // 2)
    return w.transpose(0, 1, 3, 2, 4).reshape(w.shape[0], heads * dim)


def _inproj_kernel(x_ref, g1_ref, wmain_ref, widx_ref, wvt_ref, qg_ref, kg_ref,
                   qcos_ref, qsin_ref, icos_ref, isin_ref,
                   pool_ref, q_ref, k_ref, vt_ref, idx_ref, kcat_ref):
    x = x_ref[...]
    ms = jnp.mean(x * x, axis=-1, keepdims=True)
    h = x * lax.rsqrt(ms + EPS) * g1_ref[...]
    h_hi = h.astype(BF16)

    qcos = qcos_ref[...]
    qsin = qsin_ref[...]
    lane = lax.broadcasted_iota(jnp.int32, (x.shape[0], LANES), 1)
    head_a = (lane % HEAD_DIM) < HEAD_DIM // 2

    def norm_rope(xc, gain):
        sq = xc * xc
        sa = jnp.sum(jnp.where(head_a, sq, 0.0), axis=-1, keepdims=True)
        sb = jnp.sum(jnp.where(head_a, 0.0, sq), axis=-1, keepdims=True)
        ss = jnp.where(head_a, sa, sb) * (1.0 / HEAD_DIM)
        xn = xc * lax.rsqrt(ss + EPS) * gain
        return xn * qcos + _rot_partner(xn) * qsin

    n_qcols = N_Q_HEADS * HEAD_DIM // LANES
    qk = _dot(h_hi, wmain_ref[:, POOL_WIDTH:])
    idxp = _dot(h_hi, widx_ref[...])

    half = HEAD_DIM // 2
    for c in range(n_qcols):
        qr = norm_rope(qk[:, c * LANES:(c + 1) * LANES], qg_ref[...]) * Q_SCALE
        if (2 * c) // Q_PER_KV == 0:
            pair = (jnp.where(head_a, qr, 0.0), jnp.where(head_a, pltpu.roll(qr, LANES - half, 1), 0.0))
        else:
            pair = (jnp.where(head_a, 0.0, pltpu.roll(qr, half, 1)), jnp.where(head_a, 0.0, qr))
        for i, qh in enumerate(pair):
            q_ref[:, (2 * c + i) * LANES:(2 * c + i + 1) * LANES] = qh.astype(BF16)
    k_ref[...] = norm_rope(qk[:, n_qcols * LANES:], kg_ref[...]).astype(BF16)

    for c in range(IDX_PROJ_W // LANES):
        sl = slice(c * LANES, (c + 1) * LANES)
        ic = idxp[:, sl]
        roped = ic * icos_ref[:, sl] + _rot_partner(ic) * isin_ref[:, sl]
        if c < IDXW // LANES:
            idx_ref[:, sl] = roped
        else:
            kcat_ref[...] = roped.astype(BF16)

    pool_ref[...] = _dot(h_hi, wmain_ref[:, :POOL_WIDTH])
    vt = _dot_nt(wvt_ref[...], h_hi).astype(BF16)
    for i in range(vt_ref.shape[0]):
        vt_ref[i] = vt[:, i * DSA_ROWS:(i + 1) * DSA_ROWS]


def _inproj(x2, g1, wmain, widx, wvt, qg, kg, qcos, qsin, icos, isin, seq):
    n = x2.shape[0]
    rows = INPROJ_ROWS
    tper = seq // rows
    vblk = rows // DSA_ROWS
    row = lambda w: pl.BlockSpec((rows, w), lambda i: (i, 0))
    const = lambda a: pl.BlockSpec(a.shape, lambda i: (0, 0))
    table = lambda w: pl.BlockSpec((rows, w), lambda i: (i % tper, 0))
    return pl.pallas_call(
        _inproj_kernel,
        grid=(n // rows,),
        in_specs=[row(D_MODEL), const(g1), const(wmain), const(widx), const(wvt), const(qg), const(kg),
                  table(LANES), table(LANES), table(IDX_PROJ_W), table(IDX_PROJ_W)],
        out_specs=[row(POOL_WIDTH), row(QPAD), row(LANES),
                   pl.BlockSpec((vblk, LANES, DSA_ROWS), lambda i: (i, 0, 0)), row(IDXW), row(LANES)],
        out_shape=[jax.ShapeDtypeStruct((n, POOL_WIDTH), F32),
                   jax.ShapeDtypeStruct((n, QPAD), BF16),
                   jax.ShapeDtypeStruct((n, LANES), BF16),
                   jax.ShapeDtypeStruct((n // DSA_ROWS, LANES, DSA_ROWS), BF16),
                   jax.ShapeDtypeStruct((n, IDXW), F32),
                   jax.ShapeDtypeStruct((n, LANES), BF16)],
        compiler_params=pltpu.CompilerParams(dimension_semantics=("arbitrary",),
                                             vmem_limit_bytes=48 * 1024 * 1024),
        name="inproj",
    )(x2, g1, wmain, widx, wvt, qg, kg, qcos, qsin, icos, isin)


def _key_to_float(key):
    bits = key ^ ((key >> 31) & jnp.int32(0x7FFFFFFF))
    return lax.bitcast_convert_type(bits, F32)


def _fold_groups(x, op, ways=4):
    n = x.shape[0] // SUBLANES
    accs = [x[g * SUBLANES:(g + 1) * SUBLANES, :] for g in range(ways)]
    for g in range(ways, n):
        accs[g % ways] = op(accs[g % ways], x[g * SUBLANES:(g + 1) * SUBLANES, :])
    while len(accs) > 1:
        accs = [op(accs[i], accs[i + 1]) for i in range(0, len(accs), 2)]
    return accs[0]


def _fold_sublanes(x, op):
    for shift in (4, 2, 1):
        x = op(x, pltpu.roll(x, shift, 0))
    return x


def _dsa_kernel(q_ref, idxq_ref, kcat_ref, k_ref, vt_ref, out_ref, sc_ref, qcat_ref, acc_ref, p_ref, lg_ref,
                *, seq, topk):
    rows = DSA_ROWS
    ck = DSA_ROWS
    ngrp = ck // SUBLANES
    qb = pl.program_id(1)
    nch = qb + 1
    lane = lax.broadcasted_iota(jnp.int32, (rows, LANES), 1)
    qidx = qb * rows + lax.broadcasted_iota(jnp.int32, (ck, rows), 1)
    krow = lax.broadcasted_iota(jnp.int32, (ck, rows), 0)

    slots = LANES // IDX_DIM
    slot = (lane % (LANES // 2)) // (IDX_DIM // 2)
    for c in range(IDX_HEADS // slots):
        col = idxq_ref[:, c * LANES:(c + 1) * LANES]
        for r in range(slots):
            qcat_ref[c * slots + r] = jnp.where(slot == r, col, 0.0).astype(BF16)
    wt = idxq_ref[:, 2 * LANES:].T[0:IDX_HEADS, :]

    def score_body(j, nonneg):
        kc = kcat_ref[pl.ds(pl.multiple_of(j * ck, ck), ck), :]
        acc = jnp.zeros((ck, rows), F32)
        for hd in range(IDX_HEADS):
            y = _dot_nt(kc, qcat_ref[hd])
            acc = acc + wt[hd:hd + 1, :] * jnp.maximum(y, 0.0)
        sc = jnp.where(j * ck + krow <= qidx, acc, NEG)
        sc_ref[j] = sc
        return nonneg + _fold_groups(jnp.where(sc >= 0.0, 1.0, 0.0), jnp.add)

    nonneg = lax.fori_loop(0, nch, score_body, jnp.zeros((SUBLANES, rows), F32))

    n_virtual = (seq - nch * ck).astype(F32)

    def count(pred, thr):
        def body(j, accs):
            accs = list(accs)
            for g in range(ngrp):
                hit = pred(sc_ref[j, g * SUBLANES:(g + 1) * SUBLANES, :], thr)
                accs[g % len(accs)] = accs[g % len(accs)] + jnp.where(hit, 1.0, 0.0)
            return tuple(accs)
        zero = jnp.zeros((SUBLANES, rows), F32)
        accs = lax.fori_loop(0, nch, body, (zero, zero, zero, zero))
        tot = _fold_sublanes((accs[0] + accs[1]) + (accs[2] + accs[3]), jnp.add)
        return tot + jnp.where(pred(jnp.float32(NEG), thr), n_virtual, 0.0)

    ge = lambda a, b: a >= b
    gt = lambda a, b: a > b
    kf = jnp.float32(topk)

    key = jnp.where(_fold_sublanes(nonneg, jnp.add) >= kf, jnp.int32(0), jnp.int32(-2 ** 31))

    def bit_body(i, key):
        cand = key | jnp.left_shift(jnp.int32(1), 30 - i)
        cnt = count(ge, _key_to_float(cand))
        return jnp.where(cnt >= kf, cand, key)

    key = lax.fori_loop(0, 31, bit_body, key)
    thr8 = _key_to_float(key)
    thr = thr8[0:1, :]
    need = (kf - count(gt, thr8))[0:1, :]

    ki = lax.broadcasted_iota(jnp.int32, (ck, ck), 0)
    kj = lax.broadcasted_iota(jnp.int32, (ck, ck), 1)
    lower = jnp.where(kj < ki, 1.0, 0.0).astype(BF16)
    ones8 = jnp.ones((SUBLANES, ck), BF16)

    def select_body(j, carry):
        run, ms = carry
        sc = sc_ref[j]
        eq = sc == thr
        eqb = jnp.where(eq, 1.0, 0.0).astype(BF16)
        prior = run[0:1, :] + _dot(lower, eqb)
        tie = jnp.where(prior < need, 0.0, NEG)
        bias = jnp.where(sc > thr, 0.0, jnp.where(eq, tie, NEG))
        sc_ref[j] = jnp.where(j * ck + krow <= qidx, bias, NEG)
        kc = k_ref[pl.ds(pl.multiple_of(j * ck, ck), ck), :]
        out = []
        for hd in range(N_Q_HEADS):
            s = _dot_nt(kc, q_ref[:, hd * LANES:(hd + 1) * LANES]) + sc_ref[j]
            lg_ref[hd, j] = s
            out.append(jnp.maximum(ms[hd], _fold_groups(s, jnp.maximum)))
        return run + _dot(ones8, eqb), tuple(out)

    _, ms = lax.fori_loop(0, nch, select_body,
                          (jnp.zeros((SUBLANES, rows), F32),
                           tuple(jnp.full((SUBLANES, rows), -jnp.inf, F32) for _ in range(N_Q_HEADS))))
    ms = [_fold_sublanes(m, jnp.maximum)[0:1, :] for m in ms]

    acc_ref[...] = jnp.zeros(acc_ref.shape, F32)

    def pv_body(j, ls):
        out = []
        for hd in range(N_Q_HEADS):
            p = jnp.exp2(lg_ref[hd, j] - ms[hd])
            out.append(ls[hd] + _fold_groups(p, jnp.add))
            p_ref[hd] = p.astype(BF16)
        vt = vt_ref[j]
        for hd in range(N_Q_HEADS):
            acc_ref[hd] += _dot(vt, p_ref[hd])
        return tuple(out)

    ls = lax.fori_loop(0, nch, pv_body, tuple(jnp.zeros((SUBLANES, rows), F32) for _ in range(N_Q_HEADS)))

    outs = []
    for hd in range(N_Q_HEADS):
        l = _fold_sublanes(ls[hd], jnp.add)
        r0 = HEAD_DIM * (hd // Q_PER_KV)
        outs.append(acc_ref[hd, r0:r0 + HEAD_DIM, :] / l[0:1, :])
    out_ref[...] = jnp.concatenate(outs, axis=0).T.astype(BF16)


def _dsa(q, idx, kcat, k, vt, batch, seq):
    rows = DSA_ROWS
    nq = seq // rows
    topk = min(TOPK_MAX, seq // 4)
    assert topk <= rows and seq % rows == 0
    q3 = q.reshape(batch, seq, QPAD)
    idx3 = idx.reshape(batch, seq, IDXW)
    vt = vt.reshape(batch, nq, LANES, rows)
    full = pl.BlockSpec((None, seq, LANES), lambda b, i: (b, 0, 0))
    kern = functools.partial(_dsa_kernel, seq=seq, topk=topk)
    return pl.pallas_call(
        kern,
        grid=(batch, nq),
        in_specs=[pl.BlockSpec((None, rows, QPAD), lambda b, i: (b, i, 0)),
                  pl.BlockSpec((None, rows, IDXW), lambda b, i: (b, i, 0)),
                  full, full,
                  pl.BlockSpec((None, nq, LANES, rows), lambda b, i: (b, 0, 0, 0))],
        out_specs=pl.BlockSpec((None, rows, N_Q_HEADS * HEAD_DIM), lambda b, i: (b, i, 0)),
        out_shape=jax.ShapeDtypeStruct((batch, seq, N_Q_HEADS * HEAD_DIM), BF16),
        scratch_shapes=[pltpu.VMEM((nq, rows, rows), F32),
                        pltpu.VMEM((IDX_HEADS, rows, LANES), BF16),
                        pltpu.VMEM((N_Q_HEADS, LANES, rows), F32),
                        pltpu.VMEM((N_Q_HEADS, rows, rows), BF16),
                        pltpu.VMEM((N_Q_HEADS, nq, rows, rows), F32)],
        compiler_params=pltpu.CompilerParams(dimension_semantics=("arbitrary", "arbitrary"),
                                             vmem_limit_bytes=48 * 1024 * 1024),
        name="dsa",
    )(q3, idx3, kcat.reshape(batch, seq, LANES), k.reshape(batch, seq, LANES), vt)


def _mixffn_kernel(x_ref, xp_ref, pool_ref, pp_ref, b_ref, bp_ref, pw_ref, ps_ref, woa_ref, wob_ref,
                   g2_ref, wup_ref, cw_ref, cb_ref, wdn_ref, out_ref, act_ref):
    rows = FFN_ROWS
    ext = MIX_HALO + rows
    ti = pl.program_id(1)
    row = lax.broadcasted_iota(jnp.int32, (ext, 1), 0)
    t = ti * rows - MIX_HALO + row

    pool_ext = jnp.concatenate([jnp.where(ti == 0, 0.0, pp_ref[...]), pool_ref[...]], axis=0)
    lead = POOL_BACK - MIX_HALO
    parts = []
    for g, w in enumerate(POOL_WINDOWS):
        sl = slice(g * POOL_GROUP, (g + 1) * POOL_GROUP)
        a = pool_ext[:, sl]
        span = 1
        while span < w:
            a = a + pltpu.roll(a, span, 0)
            span *= 2
        cnt = jnp.clip(t + 1, 1, w).astype(F32)
        p = a[lead:] / cnt - pool_ext[lead:, sl]
        parts.append(_dot(p.astype(BF16), pw_ref[g]) * ps_ref[:, sl])
    a_out = jnp.concatenate(parts, axis=1).astype(BF16)

    x_ext = jnp.concatenate([xp_ref[...], x_ref[...]], axis=0)
    b_ext = jnp.concatenate([bp_ref[...], b_ref[...]], axis=0)
    x1 = x_ext + _dot(a_out, woa_ref[...]) + _dot(b_ext, wob_ref[...])

    h = x1 * lax.rsqrt(jnp.mean(x1 * x1, axis=-1, keepdims=True) + EPS) * g2_ref[...]
    h_ext = jnp.where(jnp.logical_and(ti == 0, row < MIX_HALO), 0.0, h).astype(BF16)

    chunk = 2 * LANES
    for c in range(D_FF // chunk):
        halves = []
        for off in (0, D_FF):
            sl = slice(off + c * chunk, off + (c + 1) * chunk)
            u = _dot(h_ext, wup_ref[:, sl])
            u1 = pltpu.roll(u, 1, 0)
            u2 = pltpu.roll(u, 2, 0)
            cv = cb_ref[:, sl] + u2[MIX_HALO:] * cw_ref[0:1, sl]
            cv = cv + u1[MIX_HALO:] * cw_ref[1:2, sl]
            cv = cv + u[MIX_HALO:] * cw_ref[2:3, sl]
            halves.append(cv)
        gate, val = halves
        act = gate * (1.0 / (1.0 + jnp.exp(-gate))) * val
        act_ref[:, c * chunk:(c + 1) * chunk] = act.astype(BF16)
    out_ref[...] = x1[MIX_HALO:] + _dot(act_ref[...], wdn_ref[...])


def _mixffn(x3, pool3, b_out, pw, ps, woa, wob, g2, wup, cw, cb, wdn):
    batch, seq, _ = x3.shape
    rows = FFN_ROWS
    tile = lambda w: pl.BlockSpec((None, rows, w), lambda b, i: (b, i, 0))
    back = lambda h, w: pl.BlockSpec((None, h, w), lambda b, i: (b, jnp.maximum(i * (rows // h) - 1, 0), 0))
    const = lambda a: pl.BlockSpec(a.shape, lambda b, i: (0,) * a.ndim, pipeline_mode=pl.Buffered(1))
    return pl.pallas_call(
        _mixffn_kernel,
        grid=(batch, seq // rows),
        in_specs=[tile(D_MODEL), back(MIX_HALO, D_MODEL), tile(POOL_WIDTH), back(POOL_BACK, POOL_WIDTH),
                  tile(POOL_WIDTH), back(MIX_HALO, POOL_WIDTH),
                  const(pw), const(ps), const(woa), const(wob), const(g2), const(wup), const(cw), const(cb),
                  const(wdn)],
        out_specs=tile(D_MODEL),
        out_shape=jax.ShapeDtypeStruct((batch, seq, D_MODEL), F32),
        scratch_shapes=[pltpu.VMEM((rows, D_FF), BF16)],
        compiler_params=pltpu.CompilerParams(dimension_semantics=("arbitrary", "arbitrary"),
                                             vmem_limit_bytes=56 * 1024 * 1024),
        name="mixffn",
    )(x3, x3, pool3, pool3, b_out, b_out, pw, ps, woa, wob, g2, wup, cw, cb, wdn)


def _layer(x, norm1_g, w_in, q_norm_g, k_norm_g, pool_w, pool_scale, w_out, norm2_g, w_up, conv_w, conv_b, w_down):
    batch, seq, _ = x.shape
    n = batch * seq
    qk_w = N_Q_HEADS * HEAD_DIM + N_KV_HEADS * HEAD_DIM
    w_pool, w_qk, w_v, w_qi, w_ki, w_wi = jnp.split(
        w_in, [POOL_WIDTH, POOL_WIDTH + qk_w, MAIN_W, MAIN_W + IDX_HEADS * IDX_DIM,
               MAIN_W + (IDX_HEADS + 1) * IDX_DIM], axis=1)
    wmain = jnp.concatenate([w_pool, _split_halves(w_qk, N_Q_HEADS + N_KV_HEADS, HEAD_DIM)], axis=1).astype(BF16)
    wvt = w_v.T.astype(BF16)
    slots = LANES // IDX_DIM
    hi = IDX_DIM // 2
    w_idx = jnp.concatenate([_split_halves(w_qi, IDX_HEADS, IDX_DIM), w_wi,
                             jnp.zeros((D_MODEL, LANES - IDX_HEADS), F32)]
                            + [w_ki[:, :hi]] * slots + [w_ki[:, hi:]] * slots, axis=1)

    def gains(g):
        return jnp.concatenate([g[:HEAD_DIM // 2]] * 2 + [g[HEAD_DIM // 2:]] * 2)[None, :]

    def tables(dim):
        half = dim // 2
        inv = jnp.exp(-jnp.log(jnp.float32(ROPE_THETA)) * jnp.arange(half, dtype=F32) / half)
        ang = jnp.arange(seq).astype(F32)[:, None] * inv[None, :]
        reps = LANES // dim
        cos, sin = jnp.tile(jnp.cos(ang), (1, reps)), jnp.tile(jnp.sin(ang), (1, reps))
        return jnp.concatenate([cos, cos], axis=1), jnp.concatenate([-sin, sin], axis=1)

    qcos, qsin = tables(HEAD_DIM)
    c32, s32 = tables(IDX_DIM)
    wscale = jnp.full((seq, IDX_HEADS), IDX_HEADS ** -0.5 * IDX_DIM ** -0.5, F32)
    gap = jnp.zeros((seq, LANES - IDX_HEADS), F32)
    ncol = IDX_HEADS // slots
    icos = jnp.concatenate([c32] * ncol + [wscale, gap, c32], axis=1)
    isin = jnp.concatenate([s32] * ncol + [jnp.zeros((seq, LANES), F32), s32], axis=1)

    x2 = x.reshape(n, D_MODEL)
    pool, q, k, vt, idx, kcat = _inproj(x2, norm1_g[None, :], wmain, w_idx.astype(BF16), wvt, gains(q_norm_g),
                                        gains(k_norm_g), qcos, qsin, icos, isin, seq)
    b_out = _dsa(q, idx, kcat, k, vt, batch, seq)
    return _mixffn(x, pool.reshape(batch, seq, POOL_WIDTH), b_out, pool_w.astype(BF16), pool_scale[None, :],
                   w_out[:POOL_WIDTH].astype(BF16), w_out[POOL_WIDTH:].astype(BF16), norm2_g[None, :],
                   w_up.astype(BF16), conv_w, conv_b[None, :], w_down.astype(BF16))


def kernel(x, norm1_g, w_in, q_norm_g, k_norm_g, pool_w, pool_scale, w_out, norm2_g, w_up, conv_w, conv_b, w_down):
    for l in range(norm1_g.shape[0]):
        x = _layer(x, norm1_g[l], w_in[l], q_norm_g[l], k_norm_g[l], pool_w[l], pool_scale[l], w_out[l],
                   norm2_g[l], w_up[l], conv_w[l], conv_b[l], w_down[l])
    return x
```

```python
import functools

import jax
import jax.numpy as jnp
from jax import lax
from jax.experimental import pallas as pl
from jax.experimental.pallas import tpu as pltpu

F32 = jnp.float32
BF16 = jnp.bfloat16

D_MODEL = 1024
POOL_WIDTH = 512
POOL_WINDOWS = (2, 4, 8, 16)
POOL_GROUP = 128
HEAD_DIM = 64
N_Q_HEADS = 8
N_KV_HEADS = 2
Q_PER_KV = 4
IDX_HEADS = 8
IDX_DIM = 32
TOPK_MAX = 256
ROPE_THETA = 10000.0
D_FF = 2816
CONV_WIDTH = 3
EPS = 1e-6
NEG = -1e30

LANES = 128
SUBLANES = 8
QPAD = N_Q_HEADS * LANES
IDXW = 3 * LANES
IDX_PROJ_W = IDXW + LANES
MAIN_W = POOL_WIDTH + (N_Q_HEADS + 2 * N_KV_HEADS) * HEAD_DIM

INPROJ_ROWS = 512
DSA_ROWS = 256
FFN_ROWS = 512
MIX_HALO = 16
POOL_BACK = 32

Q_SCALE = HEAD_DIM ** -0.5 * 1.4426950408889634

NT_DIMS = (((1,), (1,)), ((), ()))


def _dot(a, b):
    return jnp.dot(a, b, preferred_element_type=F32)


def _dot_nt(a, b):
    return lax.dot_general(a, b, NT_DIMS, preferred_element_type=F32)


def _swap_halves(xc, half):
    lane = lax.broadcasted_iota(jnp.int32, xc.shape, 1)
    first = (lane % (2 * half)) < half
    return jnp.where(first, pltpu.roll(xc, LANES - half, 1), pltpu.roll(xc, half, 1))


def _inproj_kernel(x_ref, g1_ref, wmain_ref, widx_ref, wvt_ref, qg_ref, kg_ref,
                   qcos_ref, qsin_ref, icos_ref, isin_ref,
                   pool_ref, q_ref, k_ref, vt_ref, idx_ref, kcat_ref):
    x = x_ref[...]
    ms = jnp.mean(x * x, axis=-1, keepdims=True)
    h = x * lax.rsqrt(ms + EPS) * g1_ref[...]
    h_hi = h.astype(BF16)

    qcos = qcos_ref[...]
    qsin = qsin_ref[...]
    lane = lax.broadcasted_iota(jnp.int32, (x.shape[0], LANES), 1)
    first = lane < HEAD_DIM

    def norm_rope(xc, gain):
        sq = xc * xc
        s0 = jnp.sum(jnp.where(first, sq, 0.0), axis=-1, keepdims=True)
        s1 = jnp.sum(jnp.where(first, 0.0, sq), axis=-1, keepdims=True)
        ss = jnp.where(first, s0, s1) * (1.0 / HEAD_DIM)
        xn = xc * lax.rsqrt(ss + EPS) * gain
        return xn * qcos + _swap_halves(xn, HEAD_DIM // 2) * qsin

    n_qcols = N_Q_HEADS * HEAD_DIM // LANES
    qk = _dot(h_hi, wmain_ref[:, POOL_WIDTH:])
    idxp = _dot(h_hi, widx_ref[...])

    for c in range(n_qcols):
        qr = norm_rope(qk[:, c * LANES:(c + 1) * LANES], qg_ref[...]) * Q_SCALE
        qs = pltpu.roll(qr, HEAD_DIM, 1)
        if (2 * c) // Q_PER_KV == 0:
            pair = (jnp.where(first, qr, 0.0), jnp.where(first, qs, 0.0))
        else:
            pair = (jnp.where(first, 0.0, qs), jnp.where(first, 0.0, qr))
        for i, qh in enumerate(pair):
            q_ref[:, (2 * c + i) * LANES:(2 * c + i + 1) * LANES] = qh.astype(BF16)
    k_ref[...] = norm_rope(qk[:, n_qcols * LANES:], kg_ref[...]).astype(BF16)

    for c in range(IDX_PROJ_W // LANES):
        sl = slice(c * LANES, (c + 1) * LANES)
        ic = idxp[:, sl]
        roped = ic * icos_ref[:, sl] + _swap_halves(ic, IDX_DIM // 2) * isin_ref[:, sl]
        if c < IDXW // LANES:
            idx_ref[:, sl] = roped
        else:
            kcat_ref[...] = roped.astype(BF16)

    pool_ref[...] = _dot(h_hi, wmain_ref[:, :POOL_WIDTH])
    vt = _dot_nt(wvt_ref[...], h_hi).astype(BF16)
    for i in range(vt_ref.shape[0]):
        vt_ref[i] = vt[:, i * DSA_ROWS:(i + 1) * DSA_ROWS]


def _inproj(x2, g1, wmain, widx, wvt, qg, kg, qcos, qsin, icos, isin, seq):
    n = x2.shape[0]
    rows = INPROJ_ROWS
    tper = seq // rows
    vblk = rows // DSA_ROWS
    row = lambda w: pl.BlockSpec((rows, w), lambda i: (i, 0))
    const = lambda a: pl.BlockSpec(a.shape, lambda i: (0, 0))
    table = lambda w: pl.BlockSpec((rows, w), lambda i: (i % tper, 0))
    return pl.pallas_call(
        _inproj_kernel,
        grid=(n // rows,),
        in_specs=[row(D_MODEL), const(g1), const(wmain), const(widx), const(wvt), const(qg), const(kg),
                  table(LANES), table(LANES), table(IDX_PROJ_W), table(IDX_PROJ_W)],
        out_specs=[row(POOL_WIDTH), row(QPAD), row(LANES),
                   pl.BlockSpec((vblk, LANES, DSA_ROWS), lambda i: (i, 0, 0)), row(IDXW), row(LANES)],
        out_shape=[jax.ShapeDtypeStruct((n, POOL_WIDTH), F32),
                   jax.ShapeDtypeStruct((n, QPAD), BF16),
                   jax.ShapeDtypeStruct((n, LANES), BF16),
                   jax.ShapeDtypeStruct((n // DSA_ROWS, LANES, DSA_ROWS), BF16),
                   jax.ShapeDtypeStruct((n, IDXW), F32),
                   jax.ShapeDtypeStruct((n, LANES), BF16)],
        compiler_params=pltpu.CompilerParams(dimension_semantics=("arbitrary",),
                                             vmem_limit_bytes=48 * 1024 * 1024),
        name="inproj",
    )(x2, g1, wmain, widx, wvt, qg, kg, qcos, qsin, icos, isin)


def _key_to_float(key):
    bits = key ^ ((key >> 31) & jnp.int32(0x7FFFFFFF))
    return lax.bitcast_convert_type(bits, F32)


def _fold_groups(x, op, ways=4):
    n = x.shape[0] // SUBLANES
    accs = [x[g * SUBLANES:(g + 1) * SUBLANES, :] for g in range(ways)]
    for g in range(ways, n):
        accs[g % ways] = op(accs[g % ways], x[g * SUBLANES:(g + 1) * SUBLANES, :])
    while len(accs) > 1:
        accs = [op(accs[i], accs[i + 1]) for i in range(0, len(accs), 2)]
    return accs[0]


def _fold_sublanes(x, op):
    for shift in (4, 2, 1):
        x = op(x, pltpu.roll(x, shift, 0))
    return x


def _dsa_kernel(q_ref, idxq_ref, kcat_ref, k_ref, vt_ref, out_ref, sc_ref, qcat_ref, acc_ref, p_ref, lg_ref,
                *, seq, topk):
    rows = DSA_ROWS
    ck = DSA_ROWS
    ngrp = ck // SUBLANES
    qb = pl.program_id(1)
    nch = qb + 1
    lane = lax.broadcasted_iota(jnp.int32, (rows, LANES), 1)
    qidx = qb * rows + lax.broadcasted_iota(jnp.int32, (ck, rows), 1)
    krow = lax.broadcasted_iota(jnp.int32, (ck, rows), 0)

    slots = LANES // IDX_DIM
    for c in range(IDX_HEADS // slots):
        col = idxq_ref[:, c * LANES:(c + 1) * LANES]
        for r in range(slots):
            live = jnp.logical_and(lane >= r * IDX_DIM, lane < (r + 1) * IDX_DIM)
            qcat_ref[c * slots + r] = jnp.where(live, col, 0.0).astype(BF16)
    wt = idxq_ref[:, 2 * LANES:].T[0:IDX_HEADS, :]

    def score_body(j, nonneg):
        kc = kcat_ref[pl.ds(pl.multiple_of(j * ck, ck), ck), :]
        acc = jnp.zeros((ck, rows), F32)
        for hd in range(IDX_HEADS):
            y = _dot_nt(kc, qcat_ref[hd])
            acc = acc + wt[hd:hd + 1, :] * jnp.maximum(y, 0.0)
        sc = jnp.where(j * ck + krow <= qidx, acc, NEG)
        sc_ref[j] = sc
        return nonneg + _fold_groups(jnp.where(sc >= 0.0, 1.0, 0.0), jnp.add)

    nonneg = lax.fori_loop(0, nch, score_body, jnp.zeros((SUBLANES, rows), F32))

    n_virtual = (seq - nch * ck).astype(F32)

    def count(pred, thr):
        def body(j, accs):
            accs = list(accs)
            for g in range(ngrp):
                hit = pred(sc_ref[j, g * SUBLANES:(g + 1) * SUBLANES, :], thr)
                accs[g % len(accs)] = accs[g % len(accs)] + jnp.where(hit, 1.0, 0.0)
            return tuple(accs)
        zero = jnp.zeros((SUBLANES, rows), F32)
        accs = lax.fori_loop(0, nch, body, (zero, zero, zero, zero))
        tot = _fold_sublanes((accs[0] + accs[1]) + (accs[2] + accs[3]), jnp.add)
        return tot + jnp.where(pred(jnp.float32(NEG), thr), n_virtual, 0.0)

    ge = lambda a, b: a >= b
    gt = lambda a, b: a > b
    kf = jnp.float32(topk)

    key = jnp.where(_fold_sublanes(nonneg, jnp.add) >= kf, jnp.int32(0), jnp.int32(-2 ** 31))

    def bit_body(i, key):
        cand = key | jnp.left_shift(jnp.int32(1), 30 - i)
        cnt = count(ge, _key_to_float(cand))
        return jnp.where(cnt >= kf, cand, key)

    key = lax.fori_loop(0, 31, bit_body, key)
    thr8 = _key_to_float(key)
    thr = thr8[0:1, :]
    need = (kf - count(gt, thr8))[0:1, :]

    ki = lax.broadcasted_iota(jnp.int32, (ck, ck), 0)
    kj = lax.broadcasted_iota(jnp.int32, (ck, ck), 1)
    lower = jnp.where(kj < ki, 1.0, 0.0).astype(BF16)
    ones8 = jnp.ones((SUBLANES, ck), BF16)

    def select_body(j, carry):
        run, ms = carry
        sc = sc_ref[j]
        eq = sc == thr
        eqb = jnp.where(eq, 1.0, 0.0).astype(BF16)
        prior = run[0:1, :] + _dot(lower, eqb)
        tie = jnp.where(prior < need, 0.0, NEG)
        bias = jnp.where(sc > thr, 0.0, jnp.where(eq, tie, NEG))
        sc_ref[j] = jnp.where(j * ck + krow <= qidx, bias, NEG)
        kc = k_ref[pl.ds(pl.multiple_of(j * ck, ck), ck), :]
        out = []
        for hd in range(N_Q_HEADS):
            s = _dot_nt(kc, q_ref[:, hd * LANES:(hd + 1) * LANES]) + sc_ref[j]
            lg_ref[hd, j] = s
            out.append(jnp.maximum(ms[hd], _fold_groups(s, jnp.maximum)))
        return run + _dot(ones8, eqb), tuple(out)

    _, ms = lax.fori_loop(0, nch, select_body,
                          (jnp.zeros((SUBLANES, rows), F32),
                           tuple(jnp.full((SUBLANES, rows), -jnp.inf, F32) for _ in range(N_Q_HEADS))))
    ms = [_fold_sublanes(m, jnp.maximum)[0:1, :] for m in ms]

    acc_ref[...] = jnp.zeros(acc_ref.shape, F32)

    def pv_body(j, ls):
        out = []
        for hd in range(N_Q_HEADS):
            p = jnp.exp2(lg_ref[hd, j] - ms[hd])
            out.append(ls[hd] + _fold_groups(p, jnp.add))
            p_ref[hd] = p.astype(BF16)
        vt = vt_ref[j]
        for hd in range(N_Q_HEADS):
            acc_ref[hd] += _dot(vt, p_ref[hd])
        return tuple(out)

    ls = lax.fori_loop(0, nch, pv_body, tuple(jnp.zeros((SUBLANES, rows), F32) for _ in range(N_Q_HEADS)))

    outs = []
    for hd in range(N_Q_HEADS):
        l = _fold_sublanes(ls[hd], jnp.add)
        r0 = HEAD_DIM * (hd // Q_PER_KV)
        outs.append(acc_ref[hd, r0:r0 + HEAD_DIM, :] / l[0:1, :])
    out_ref[...] = jnp.concatenate(outs, axis=0).T.astype(BF16)


def _dsa(q, idx, kcat, k, vt, batch, seq):
    rows = DSA_ROWS
    nq = seq // rows
    topk = min(TOPK_MAX, seq // 4)
    assert topk <= rows and seq % rows == 0
    q3 = q.reshape(batch, seq, QPAD)
    idx3 = idx.reshape(batch, seq, IDXW)
    vt = vt.reshape(batch, nq, LANES, rows)
    full = pl.BlockSpec((None, seq, LANES), lambda b, i: (b, 0, 0))
    kern = functools.partial(_dsa_kernel, seq=seq, topk=topk)
    return pl.pallas_call(
        kern,
        grid=(batch, nq),
        in_specs=[pl.BlockSpec((None, rows, QPAD), lambda b, i: (b, i, 0)),
                  pl.BlockSpec((None, rows, IDXW), lambda b, i: (b, i, 0)),
                  full, full,
                  pl.BlockSpec((None, nq, LANES, rows), lambda b, i: (b, 0, 0, 0))],
        out_specs=pl.BlockSpec((None, rows, N_Q_HEADS * HEAD_DIM), lambda b, i: (b, i, 0)),
        out_shape=jax.ShapeDtypeStruct((batch, seq, N_Q_HEADS * HEAD_DIM), BF16),
        scratch_shapes=[pltpu.VMEM((nq, rows, rows), F32),
                        pltpu.VMEM((IDX_HEADS, rows, LANES), BF16),
                        pltpu.VMEM((N_Q_HEADS, LANES, rows), F32),
                        pltpu.VMEM((N_Q_HEADS, rows, rows), BF16),
                        pltpu.VMEM((N_Q_HEADS, nq, rows, rows), F32)],
        compiler_params=pltpu.CompilerParams(dimension_semantics=("arbitrary", "arbitrary"),
                                             vmem_limit_bytes=48 * 1024 * 1024),
        name="dsa",
    )(q3, idx3, kcat.reshape(batch, seq, LANES), k.reshape(batch, seq, LANES), vt)


def _mixffn_kernel(x_ref, xp_ref, pool_ref, pp_ref, b_ref, bp_ref, pw_ref, ps_ref, woa_ref, wob_ref,
                   g2_ref, wup_ref, cw_ref, cb_ref, wdn_ref, out_ref, act_ref):
    rows = FFN_ROWS
    ext = MIX_HALO + rows
    ti = pl.program_id(1)
    row = lax.broadcasted_iota(jnp.int32, (ext, 1), 0)
    t = ti * rows - MIX_HALO + row

    pool_ext = jnp.concatenate([jnp.where(ti == 0, 0.0, pp_ref[...]), pool_ref[...]], axis=0)
    lead = POOL_BACK - MIX_HALO
    parts = []
    for g, w in enumerate(POOL_WINDOWS):
        sl = slice(g * POOL_GROUP, (g + 1) * POOL_GROUP)
        a = pool_ext[:, sl]
        span = 1
        while span < w:
            a = a + pltpu.roll(a, span, 0)
            span *= 2
        cnt = jnp.clip(t + 1, 1, w).astype(F32)
        p = a[lead:] / cnt - pool_ext[lead:, sl]
        parts.append(_dot(p.astype(BF16), pw_ref[g]) * ps_ref[:, sl])
    a_out = jnp.concatenate(parts, axis=1).astype(BF16)

    x_ext = jnp.concatenate([xp_ref[...], x_ref[...]], axis=0)
    b_ext = jnp.concatenate([bp_ref[...], b_ref[...]], axis=0)
    x1 = x_ext + _dot(a_out, woa_ref[...]) + _dot(b_ext, wob_ref[...])

    h = x1 * lax.rsqrt(jnp.mean(x1 * x1, axis=-1, keepdims=True) + EPS) * g2_ref[...]
    h_ext = jnp.where(jnp.logical_and(ti == 0, row < MIX_HALO), 0.0, h).astype(BF16)

    chunk = 2 * LANES
    for c in range(D_FF // chunk):
        halves = []
        for off in (0, D_FF):
            sl = slice(off + c * chunk, off + (c + 1) * chunk)
            u = _dot(h_ext, wup_ref[:, sl])
            u1 = pltpu.roll(u, 1, 0)
            u2 = pltpu.roll(u, 2, 0)
            cv = cb_ref[:, sl] + u2[MIX_HALO:] * cw_ref[0:1, sl]
            cv = cv + u1[MIX_HALO:] * cw_ref[1:2, sl]
            cv = cv + u[MIX_HALO:] * cw_ref[2:3, sl]
            halves.append(cv)
        gate, val = halves
        act = gate * (1.0 / (1.0 + jnp.exp(-gate))) * val
        act_ref[:, c * chunk:(c + 1) * chunk] = act.astype(BF16)
    out_ref[...] = x1[MIX_HALO:] + _dot(act_ref[...], wdn_ref[...])


def _mixffn(x3, pool3, b_out, pw, ps, woa, wob, g2, wup, cw, cb, wdn):
    batch, seq, _ = x3.shape
    rows = FFN_ROWS
    tile = lambda w: pl.BlockSpec((None, rows, w), lambda b, i: (b, i, 0))
    back = lambda h, w: pl.BlockSpec((None, h, w), lambda b, i: (b, jnp.maximum(i * (rows // h) - 1, 0), 0))
    const = lambda a: pl.BlockSpec(a.shape, lambda b, i: (0,) * a.ndim, pipeline_mode=pl.Buffered(1))
    return pl.pallas_call(
        _mixffn_kernel,
        grid=(batch, seq // rows),
        in_specs=[tile(D_MODEL), back(MIX_HALO, D_MODEL), tile(POOL_WIDTH), back(POOL_BACK, POOL_WIDTH),
                  tile(POOL_WIDTH), back(MIX_HALO, POOL_WIDTH),
                  const(pw), const(ps), const(woa), const(wob), const(g2), const(wup), const(cw), const(cb),
                  const(wdn)],
        out_specs=tile(D_MODEL),
        out_shape=jax.ShapeDtypeStruct((batch, seq, D_MODEL), F32),
        scratch_shapes=[pltpu.VMEM((rows, D_FF), BF16)],
        compiler_params=pltpu.CompilerParams(dimension_semantics=("arbitrary", "arbitrary"),
                                             vmem_limit_bytes=56 * 1024 * 1024),
        name="mixffn",
    )(x3, x3, pool3, pool3, b_out, b_out, pw, ps, woa, wob, g2, wup, cw, cb, wdn)


def _rope_tables(seq, dim):
    half = dim // 2
    inv = jnp.exp(-jnp.log(jnp.float32(ROPE_THETA)) * jnp.arange(half, dtype=F32) / half)
    ang = jnp.arange(seq).astype(F32)[:, None] * inv[None, :]
    cos, sin = jnp.cos(ang), jnp.sin(ang)
    return jnp.concatenate([cos, cos], axis=1), jnp.concatenate([-sin, sin], axis=1)


def _layer(x, norm1_g, w_in, q_norm_g, k_norm_g, pool_w, pool_scale, w_out, norm2_g, w_up, conv_w, conv_b, w_down):
    batch, seq, _ = x.shape
    n = batch * seq
    v0 = MAIN_W - N_KV_HEADS * HEAD_DIM
    wmain = w_in[:, :v0].astype(BF16)
    wvt = w_in[:, v0:MAIN_W].T.astype(BF16)
    w_qi, w_ki, w_wi = jnp.split(w_in[:, MAIN_W:], [IDX_HEADS * IDX_DIM, (IDX_HEADS + 1) * IDX_DIM], axis=1)
    qg = jnp.concatenate([q_norm_g, q_norm_g])[None, :]
    kg = jnp.concatenate([k_norm_g, k_norm_g])[None, :]
    slots = LANES // IDX_DIM
    w_idx = jnp.concatenate([w_qi, w_wi, jnp.zeros((D_MODEL, LANES - IDX_HEADS), F32)] + [w_ki] * slots, axis=1)

    c64, s64 = _rope_tables(seq, HEAD_DIM)
    qcos, qsin = jnp.tile(c64, (1, LANES // HEAD_DIM)), jnp.tile(s64, (1, LANES // HEAD_DIM))
    c32, s32 = _rope_tables(seq, IDX_DIM)
    wscale = jnp.full((seq, IDX_HEADS), IDX_HEADS ** -0.5 * IDX_DIM ** -0.5, F32)
    gap = jnp.zeros((seq, LANES - IDX_HEADS), F32)
    icos = jnp.concatenate([jnp.tile(c32, (1, IDX_HEADS)), wscale, gap, jnp.tile(c32, (1, slots))], axis=1)
    isin = jnp.concatenate([jnp.tile(s32, (1, IDX_HEADS)), jnp.zeros((seq, LANES), F32),
                            jnp.tile(s32, (1, slots))], axis=1)

    x2 = x.reshape(n, D_MODEL)
    pool, q, k, vt, idx, kcat = _inproj(x2, norm1_g[None, :], wmain, w_idx.astype(BF16), wvt, qg, kg,
                                        qcos, qsin, icos, isin, seq)
    b_out = _dsa(q, idx, kcat, k, vt, batch, seq)
    return _mixffn(x, pool.reshape(batch, seq, POOL_WIDTH), b_out, pool_w.astype(BF16), pool_scale[None, :],
                   w_out[:POOL_WIDTH].astype(BF16), w_out[POOL_WIDTH:].astype(BF16), norm2_g[None, :],
                   w_up.astype(BF16), conv_w, conv_b[None, :], w_down.astype(BF16))


def kernel(x, norm1_g, w_in, q_norm_g, k_norm_g, pool_w, pool_scale, w_out, norm2_g, w_up, conv_w, conv_b, w_down):
    for l in range(norm1_g.shape[0]):
        x = _layer(x, norm1_g[l], w_in[l], q_norm_g[l], k_norm_g[l], pool_w[l], pool_scale[l], w_out[l],
                   norm2_g[l], w_up[l], conv_w[l], conv_b[l], w_down[l])
    return x
```

```python
import functools

import jax
import jax.numpy as jnp
from jax import lax
from jax.experimental import pallas as pl
from jax.experimental.pallas import tpu as pltpu

F32 = jnp.float32
BF16 = jnp.bfloat16

D_MODEL = 1024
POOL_WIDTH = 512
POOL_WINDOWS = (2, 4, 8, 16)
POOL_GROUP = 128
HEAD_DIM = 64
N_Q_HEADS = 8
N_KV_HEADS = 2
Q_PER_KV = 4
IDX_HEADS = 8
IDX_DIM = 32
TOPK_MAX = 256
ROPE_THETA = 10000.0
D_FF = 2816
CONV_WIDTH = 3
EPS = 1e-6
NEG = -1e30

LANES = 128
SUBLANES = 8
QPAD = N_Q_HEADS * LANES
IDXW = 3 * LANES
IDX_PROJ_W = IDXW + LANES
MAIN_W = POOL_WIDTH + (N_Q_HEADS + 2 * N_KV_HEADS) * HEAD_DIM

INPROJ_ROWS = 512
DSA_ROWS = 256
FFN_ROWS = 512
MIX_HALO = 16
POOL_BACK = 32

Q_SCALE = HEAD_DIM ** -0.5 * 1.4426950408889634

NT_DIMS = (((1,), (1,)), ((), ()))


def _dot(a, b):
    return jnp.dot(a, b, preferred_element_type=F32)


def _dot_nt(a, b):
    return lax.dot_general(a, b, NT_DIMS, preferred_element_type=F32)


def _swap_halves(xc, half):
    lane = lax.broadcasted_iota(jnp.int32, xc.shape, 1)
    first = (lane % (2 * half)) < half
    return jnp.where(first, pltpu.roll(xc, LANES - half, 1), pltpu.roll(xc, half, 1))


def _inproj_kernel(x_ref, g1_ref, wmain_ref, widx_ref, wvt_ref, qg_ref, kg_ref,
                   qcos_ref, qsin_ref, icos_ref, isin_ref,
                   pool_ref, q_ref, k_ref, vt_ref, idx_ref, kcat_ref):
    x = x_ref[...]
    ms = jnp.mean(x * x, axis=-1, keepdims=True)
    h = x * lax.rsqrt(ms + EPS) * g1_ref[...]
    h_hi = h.astype(BF16)

    qcos = qcos_ref[...]
    qsin = qsin_ref[...]
    lane = lax.broadcasted_iota(jnp.int32, (x.shape[0], LANES), 1)
    first = lane < HEAD_DIM

    def norm_rope(xc, gain):
        sq = xc * xc
        s0 = jnp.sum(jnp.where(first, sq, 0.0), axis=-1, keepdims=True)
        s1 = jnp.sum(jnp.where(first, 0.0, sq), axis=-1, keepdims=True)
        ss = jnp.where(first, s0, s1) * (1.0 / HEAD_DIM)
        xn = xc * lax.rsqrt(ss + EPS) * gain
        return xn * qcos + _swap_halves(xn, HEAD_DIM // 2) * qsin

    n_qcols = N_Q_HEADS * HEAD_DIM // LANES
    qk = _dot(h_hi, wmain_ref[:, POOL_WIDTH:])
    idxp = _dot(h_hi, widx_ref[...])

    for c in range(n_qcols):
        qr = norm_rope(qk[:, c * LANES:(c + 1) * LANES], qg_ref[...]) * Q_SCALE
        qs = pltpu.roll(qr, HEAD_DIM, 1)
        if (2 * c) // Q_PER_KV == 0:
            pair = (jnp.where(first, qr, 0.0), jnp.where(first, qs, 0.0))
        else:
            pair = (jnp.where(first, 0.0, qs), jnp.where(first, 0.0, qr))
        for i, qh in enumerate(pair):
            q_ref[:, (2 * c + i) * LANES:(2 * c + i + 1) * LANES] = qh.astype(BF16)
    k_ref[...] = norm_rope(qk[:, n_qcols * LANES:], kg_ref[...]).astype(BF16)

    for c in range(IDX_PROJ_W // LANES):
        sl = slice(c * LANES, (c + 1) * LANES)
        ic = idxp[:, sl]
        roped = ic * icos_ref[:, sl] + _swap_halves(ic, IDX_DIM // 2) * isin_ref[:, sl]
        if c < IDXW // LANES:
            idx_ref[:, sl] = roped
        else:
            kcat_ref[...] = roped.astype(BF16)

    pool_ref[...] = _dot(h_hi, wmain_ref[:, :POOL_WIDTH])
    vt = _dot_nt(wvt_ref[...], h_hi).astype(BF16)
    for i in range(vt_ref.shape[0]):
        vt_ref[i] = vt[:, i * DSA_ROWS:(i + 1) * DSA_ROWS]


def _inproj(x2, g1, wmain, widx, wvt, qg, kg, qcos, qsin, icos, isin, seq):
    n = x2.shape[0]
    rows = INPROJ_ROWS
    tper = seq // rows
    vblk = rows // DSA_ROWS
    row = lambda w: pl.BlockSpec((rows, w), lambda i: (i, 0))
    const = lambda a: pl.BlockSpec(a.shape, lambda i: (0, 0))
    table = lambda w: pl.BlockSpec((rows, w), lambda i: (i % tper, 0))
    return pl.pallas_call(
        _inproj_kernel,
        grid=(n // rows,),
        in_specs=[row(D_MODEL), const(g1), const(wmain), const(widx), const(wvt), const(qg), const(kg),
                  table(LANES), table(LANES), table(IDX_PROJ_W), table(IDX_PROJ_W)],
        out_specs=[row(POOL_WIDTH), row(QPAD), row(LANES),
                   pl.BlockSpec((vblk, LANES, DSA_ROWS), lambda i: (i, 0, 0)), row(IDXW), row(LANES)],
        out_shape=[jax.ShapeDtypeStruct((n, POOL_WIDTH), F32),
                   jax.ShapeDtypeStruct((n, QPAD), BF16),
                   jax.ShapeDtypeStruct((n, LANES), BF16),
                   jax.ShapeDtypeStruct((n // DSA_ROWS, LANES, DSA_ROWS), BF16),
                   jax.ShapeDtypeStruct((n, IDXW), F32),
                   jax.ShapeDtypeStruct((n, LANES), BF16)],
        compiler_params=pltpu.CompilerParams(dimension_semantics=("arbitrary",),
                                             vmem_limit_bytes=48 * 1024 * 1024),
        name="inproj",
    )(x2, g1, wmain, widx, wvt, qg, kg, qcos, qsin, icos, isin)


def _key_to_float(key):
    bits = key ^ ((key >> 31) & jnp.int32(0x7FFFFFFF))
    return lax.bitcast_convert_type(bits, F32)


def _fold_groups(x, op, ways=4):
    n = x.shape[0] // SUBLANES
    accs = [x[g * SUBLANES:(g + 1) * SUBLANES, :] for g in range(ways)]
    for g in range(ways, n):
        accs[g % ways] = op(accs[g % ways], x[g * SUBLANES:(g + 1) * SUBLANES, :])
    while len(accs) > 1:
        accs = [op(accs[i], accs[i + 1]) for i in range(0, len(accs), 2)]
    return accs[0]


def _fold_sublanes(x, op):
    for shift in (4, 2, 1):
        x = op(x, pltpu.roll(x, shift, 0))
    return x


def _chunk_loop(n, body, init):
    carry = lax.fori_loop(0, n // 2, lambda i, c: body(2 * i + 1, body(2 * i, c)), init)
    return lax.cond(n % 2 == 1, lambda c: body(n - 1, c), lambda c: c, carry)


def _dsa_kernel(q_ref, idxq_ref, kcat_ref, k_ref, vt_ref, out_ref, sc_ref, qcat_ref, acc_ref, p_ref, lg_ref,
                *, seq, topk):
    rows = DSA_ROWS
    ck = DSA_ROWS
    ngrp = ck // SUBLANES
    qb = pl.program_id(1)
    nch = qb + 1
    lane = lax.broadcasted_iota(jnp.int32, (rows, LANES), 1)
    qidx = qb * rows + lax.broadcasted_iota(jnp.int32, (ck, rows), 1)
    krow = lax.broadcasted_iota(jnp.int32, (ck, rows), 0)

    slots = LANES // IDX_DIM
    for c in range(IDX_HEADS // slots):
        col = idxq_ref[:, c * LANES:(c + 1) * LANES]
        for r in range(slots):
            live = jnp.logical_and(lane >= r * IDX_DIM, lane < (r + 1) * IDX_DIM)
            qcat_ref[c * slots + r] = jnp.where(live, col, 0.0).astype(BF16)
    wt = idxq_ref[:, 2 * LANES:].T[0:IDX_HEADS, :]

    def score_body(j, nonneg):
        kc = kcat_ref[pl.ds(pl.multiple_of(j * ck, ck), ck), :]
        acc = jnp.zeros((ck, rows), F32)
        for hd in range(IDX_HEADS):
            y = _dot_nt(kc, qcat_ref[hd])
            acc = acc + wt[hd:hd + 1, :] * jnp.maximum(y, 0.0)
        sc = jnp.where(j * ck + krow <= qidx, acc, NEG)
        sc_ref[j] = sc
        return nonneg + _fold_groups(jnp.where(sc >= 0.0, 1.0, 0.0), jnp.add)

    nonneg = _chunk_loop(nch, score_body, jnp.zeros((SUBLANES, rows), F32))

    n_virtual = (seq - nch * ck).astype(F32)

    def count(pred, thr):
        def body(j, accs):
            accs = list(accs)
            for g in range(ngrp):
                hit = pred(sc_ref[j, g * SUBLANES:(g + 1) * SUBLANES, :], thr)
                accs[g % len(accs)] = accs[g % len(accs)] + jnp.where(hit, 1.0, 0.0)
            return tuple(accs)
        zero = jnp.zeros((SUBLANES, rows), F32)
        accs = _chunk_loop(nch, body, (zero, zero, zero, zero))
        tot = _fold_sublanes((accs[0] + accs[1]) + (accs[2] + accs[3]), jnp.add)
        return tot + jnp.where(pred(jnp.float32(NEG), thr), n_virtual, 0.0)

    ge = lambda a, b: a >= b
    gt = lambda a, b: a > b
    kf = jnp.float32(topk)

    key = jnp.where(_fold_sublanes(nonneg, jnp.add) >= kf, jnp.int32(0), jnp.int32(-2 ** 31))

    def bit_body(i, key):
        cand = key | jnp.left_shift(jnp.int32(1), 30 - i)
        cnt = count(ge, _key_to_float(cand))
        return jnp.where(cnt >= kf, cand, key)

    key = lax.fori_loop(0, 31, bit_body, key)
    thr8 = _key_to_float(key)
    thr = thr8[0:1, :]
    need = (kf - count(gt, thr8))[0:1, :]

    ki = lax.broadcasted_iota(jnp.int32, (ck, ck), 0)
    kj = lax.broadcasted_iota(jnp.int32, (ck, ck), 1)
    lower = jnp.where(kj < ki, 1.0, 0.0).astype(BF16)
    ones8 = jnp.ones((SUBLANES, ck), BF16)

    def select_body(j, carry):
        run, ms = carry
        sc = sc_ref[j]
        eq = sc == thr
        eqb = jnp.where(eq, 1.0, 0.0).astype(BF16)
        prior = run[0:1, :] + _dot(lower, eqb)
        tie = jnp.where(prior < need, 0.0, NEG)
        bias = jnp.where(sc > thr, 0.0, jnp.where(eq, tie, NEG))
        sc_ref[j] = jnp.where(j * ck + krow <= qidx, bias, NEG)
        kc = k_ref[pl.ds(pl.multiple_of(j * ck, ck), ck), :]
        out = []
        for hd in range(N_Q_HEADS):
            s = _dot_nt(kc, q_ref[:, hd * LANES:(hd + 1) * LANES]) + sc_ref[j]
            lg_ref[hd, j] = s
            out.append(jnp.maximum(ms[hd], _fold_groups(s, jnp.maximum)))
        return run + _dot(ones8, eqb), tuple(out)

    _, ms = _chunk_loop(nch, select_body,
                        (jnp.zeros((SUBLANES, rows), F32),
                         tuple(jnp.full((SUBLANES, rows), -jnp.inf, F32) for _ in range(N_Q_HEADS))))
    ms = [_fold_sublanes(m, jnp.maximum)[0:1, :] for m in ms]

    acc_ref[...] = jnp.zeros(acc_ref.shape, F32)

    def pv_body(j, ls):
        out = []
        for hd in range(N_Q_HEADS):
            p = jnp.exp2(lg_ref[hd, j] - ms[hd])
            out.append(ls[hd] + _fold_groups(p, jnp.add))
            p_ref[hd] = p.astype(BF16)
        vt = vt_ref[j]
        for hd in range(N_Q_HEADS):
            acc_ref[hd] += _dot(vt, p_ref[hd])
        return tuple(out)

    ls = _chunk_loop(nch, pv_body, tuple(jnp.zeros((SUBLANES, rows), F32) for _ in range(N_Q_HEADS)))

    outs = []
    for hd in range(N_Q_HEADS):
        l = _fold_sublanes(ls[hd], jnp.add)
        r0 = HEAD_DIM * (hd // Q_PER_KV)
        outs.append(acc_ref[hd, r0:r0 + HEAD_DIM, :] / l[0:1, :])
    out_ref[...] = jnp.concatenate(outs, axis=0).T.astype(BF16)


def _dsa(q, idx, kcat, k, vt, batch, seq):
    rows = DSA_ROWS
    nq = seq // rows
    topk = min(TOPK_MAX, seq // 4)
    assert topk <= rows and seq % rows == 0
    q3 = q.reshape(batch, seq, QPAD)
    idx3 = idx.reshape(batch, seq, IDXW)
    vt = vt.reshape(batch, nq, LANES, rows)
    full = pl.BlockSpec((None, seq, LANES), lambda b, i: (b, 0, 0))
    kern = functools.partial(_dsa_kernel, seq=seq, topk=topk)
    return pl.pallas_call(
        kern,
        grid=(batch, nq),
        in_specs=[pl.BlockSpec((None, rows, QPAD), lambda b, i: (b, i, 0)),
                  pl.BlockSpec((None, rows, IDXW), lambda b, i: (b, i, 0)),
                  full, full,
                  pl.BlockSpec((None, nq, LANES, rows), lambda b, i: (b, 0, 0, 0))],
        out_specs=pl.BlockSpec((None, rows, N_Q_HEADS * HEAD_DIM), lambda b, i: (b, i, 0)),
        out_shape=jax.ShapeDtypeStruct((batch, seq, N_Q_HEADS * HEAD_DIM), BF16),
        scratch_shapes=[pltpu.VMEM((nq, rows, rows), F32),
                        pltpu.VMEM((IDX_HEADS, rows, LANES), BF16),
                        pltpu.VMEM((N_Q_HEADS, LANES, rows), F32),
                        pltpu.VMEM((N_Q_HEADS, rows, rows), BF16),
                        pltpu.VMEM((N_Q_HEADS, nq, rows, rows), F32)],
        compiler_params=pltpu.CompilerParams(dimension_semantics=("arbitrary", "arbitrary"),
                                             vmem_limit_bytes=48 * 1024 * 1024),
        name="dsa",
    )(q3, idx3, kcat.reshape(batch, seq, LANES), k.reshape(batch, seq, LANES), vt)


def _mixffn_kernel(x_ref, xp_ref, pool_ref, pp_ref, b_ref, bp_ref, pw_ref, ps_ref, woa_ref, wob_ref,
                   g2_ref, wup_ref, cw_ref, cb_ref, wdn_ref, out_ref, act_ref):
    rows = FFN_ROWS
    ext = MIX_HALO + rows
    ti = pl.program_id(1)
    row = lax.broadcasted_iota(jnp.int32, (ext, 1), 0)
    t = ti * rows - MIX_HALO + row

    pool_ext = jnp.concatenate([jnp.where(ti == 0, 0.0, pp_ref[...]), pool_ref[...]], axis=0)
    lead = POOL_BACK - MIX_HALO
    parts = []
    for g, w in enumerate(POOL_WINDOWS):
        sl = slice(g * POOL_GROUP, (g + 1) * POOL_GROUP)
        a = pool_ext[:, sl]
        span = 1
        while span < w:
            a = a + pltpu.roll(a, span, 0)
            span *= 2
        cnt = jnp.clip(t + 1, 1, w).astype(F32)
        p = a[lead:] / cnt - pool_ext[lead:, sl]
        parts.append(_dot(p.astype(BF16), pw_ref[g]) * ps_ref[:, sl])
    a_out = jnp.concatenate(parts, axis=1).astype(BF16)

    x_ext = jnp.concatenate([xp_ref[...], x_ref[...]], axis=0)
    b_ext = jnp.concatenate([bp_ref[...], b_ref[...]], axis=0)
    x1 = x_ext + _dot(a_out, woa_ref[...]) + _dot(b_ext, wob_ref[...])

    h = x1 * lax.rsqrt(jnp.mean(x1 * x1, axis=-1, keepdims=True) + EPS) * g2_ref[...]
    h_ext = jnp.where(jnp.logical_and(ti == 0, row < MIX_HALO), 0.0, h).astype(BF16)

    chunk = 2 * LANES
    for c in range(D_FF // chunk):
        halves = []
        for off in (0, D_FF):
            sl = slice(off + c * chunk, off + (c + 1) * chunk)
            u = _dot(h_ext, wup_ref[:, sl])
            u1 = pltpu.roll(u, 1, 0)
            u2 = pltpu.roll(u, 2, 0)
            cv = cb_ref[:, sl] + u2[MIX_HALO:] * cw_ref[0:1, sl]
            cv = cv + u1[MIX_HALO:] * cw_ref[1:2, sl]
            cv = cv + u[MIX_HALO:] * cw_ref[2:3, sl]
            halves.append(cv)
        gate, val = halves
        act = gate * (1.0 / (1.0 + jnp.exp(-gate))) * val
        act_ref[:, c * chunk:(c + 1) * chunk] = act.astype(BF16)
    out_ref[...] = x1[MIX_HALO:] + _dot(act_ref[...], wdn_ref[...])


def _mixffn(x3, pool3, b_out, pw, ps, woa, wob, g2, wup, cw, cb, wdn):
    batch, seq, _ = x3.shape
    rows = FFN_ROWS
    tile = lambda w: pl.BlockSpec((None, rows, w), lambda b, i: (b, i, 0))
    back = lambda h, w: pl.BlockSpec((None, h, w), lambda b, i: (b, jnp.maximum(i * (rows // h) - 1, 0), 0))
    const = lambda a: pl.BlockSpec(a.shape, lambda b, i: (0,) * a.ndim, pipeline_mode=pl.Buffered(1))
    return pl.pallas_call(
        _mixffn_kernel,
        grid=(batch, seq // rows),
        in_specs=[tile(D_MODEL), back(MIX_HALO, D_MODEL), tile(POOL_WIDTH), back(POOL_BACK, POOL_WIDTH),
                  tile(POOL_WIDTH), back(MIX_HALO, POOL_WIDTH),
                  const(pw), const(ps), const(woa), const(wob), const(g2), const(wup), const(cw), const(cb),
                  const(wdn)],
        out_specs=tile(D_MODEL),
        out_shape=jax.ShapeDtypeStruct((batch, seq, D_MODEL), F32),
        scratch_shapes=[pltpu.VMEM((rows, D_FF), BF16)],
        compiler_params=pltpu.CompilerParams(dimension_semantics=("arbitrary", "arbitrary"),
                                             vmem_limit_bytes=56 * 1024 * 1024),
        name="mixffn",
    )(x3, x3, pool3, pool3, b_out, b_out, pw, ps, woa, wob, g2, wup, cw, cb, wdn)


def _rope_tables(seq, dim):
    half = dim // 2
    inv = jnp.exp(-jnp.log(jnp.float32(ROPE_THETA)) * jnp.arange(half, dtype=F32) / half)
    ang = jnp.arange(seq).astype(F32)[:, None] * inv[None, :]
    cos, sin = jnp.cos(ang), jnp.sin(ang)
    return jnp.concatenate([cos, cos], axis=1), jnp.concatenate([-sin, sin], axis=1)


def _layer(x, norm1_g, w_in, q_norm_g, k_norm_g, pool_w, pool_scale, w_out, norm2_g, w_up, conv_w, conv_b, w_down):
    batch, seq, _ = x.shape
    n = batch * seq
    v0 = MAIN_W - N_KV_HEADS * HEAD_DIM
    wmain = w_in[:, :v0].astype(BF16)
    wvt = w_in[:, v0:MAIN_W].T.astype(BF16)
    w_qi, w_ki, w_wi = jnp.split(w_in[:, MAIN_W:], [IDX_HEADS * IDX_DIM, (IDX_HEADS + 1) * IDX_DIM], axis=1)
    qg = jnp.concatenate([q_norm_g, q_norm_g])[None, :]
    kg = jnp.concatenate([k_norm_g, k_norm_g])[None, :]
    slots = LANES // IDX_DIM
    w_idx = jnp.concatenate([w_qi, w_wi, jnp.zeros((D_MODEL, LANES - IDX_HEADS), F32)] + [w_ki] * slots, axis=1)

    c64, s64 = _rope_tables(seq, HEAD_DIM)
    qcos, qsin = jnp.tile(c64, (1, LANES // HEAD_DIM)), jnp.tile(s64, (1, LANES // HEAD_DIM))
    c32, s32 = _rope_tables(seq, IDX_DIM)
    wscale = jnp.full((seq, IDX_HEADS), IDX_HEADS ** -0.5 * IDX_DIM ** -0.5, F32)
    gap = jnp.zeros((seq, LANES - IDX_HEADS), F32)
    icos = jnp.concatenate([jnp.tile(c32, (1, IDX_HEADS)), wscale, gap, jnp.tile(c32, (1, slots))], axis=1)
    isin = jnp.concatenate([jnp.tile(s32, (1, IDX_HEADS)), jnp.zeros((seq, LANES), F32),
                            jnp.tile(s32, (1, slots))], axis=1)

    x2 = x.reshape(n, D_MODEL)
    pool, q, k, vt, idx, kcat = _inproj(x2, norm1_g[None, :], wmain, w_idx.astype(BF16), wvt, qg, kg,
                                        qcos, qsin, icos, isin, seq)
    b_out = _dsa(q, idx, kcat, k, vt, batch, seq)
    return _mixffn(x, pool.reshape(batch, seq, POOL_WIDTH), b_out, pool_w.astype(BF16), pool_scale[None, :],
                   w_out[:POOL_WIDTH].astype(BF16), w_out[POOL_WIDTH:].astype(BF16), norm2_g[None, :],
                   w_up.astype(BF16), conv_w, conv_b[None, :], w_down.astype(BF16))


def kernel(x, norm1_g, w_in, q_norm_g, k_norm_g, pool_w, pool_scale, w_out, norm2_g, w_up, conv_w, conv_b, w_down):
    for l in range(norm1_g.shape[0]):
        x = _layer(x, norm1_g[l], w_in[l], q_norm_g[l], k_norm_g[l], pool_w[l], pool_scale[l], w_out[l],
                   norm2_g[l], w_up[l], conv_w[l], conv_b[l], w_down[l])
    return x
```

```python
import functools

import jax
import jax.numpy as jnp
from jax import lax
from jax.experimental import pallas as pl
from jax.experimental.pallas import tpu as pltpu

F32 = jnp.float32
BF16 = jnp.bfloat16

D_MODEL = 1024
POOL_WIDTH = 512
POOL_WINDOWS = (2, 4, 8, 16)
POOL_GROUP = 128
HEAD_DIM = 64
N_Q_HEADS = 8
N_KV_HEADS = 2
Q_PER_KV = 4
IDX_HEADS = 8
IDX_DIM = 32
TOPK_MAX = 256
ROPE_THETA = 10000.0
D_FF = 2816
CONV_WIDTH = 3
EPS = 1e-6
NEG = -1e30

LANES = 128
SUBLANES = 8
QPAD = N_Q_HEADS * LANES
IDXW = 3 * LANES
IDX_PROJ_W = IDXW + LANES
MAIN_W = POOL_WIDTH + (N_Q_HEADS + 2 * N_KV_HEADS) * HEAD_DIM

INPROJ_ROWS = 512
DSA_ROWS = 256
FFN_ROWS = 512
MIX_HALO = 16
POOL_BACK = 32

Q_SCALE = HEAD_DIM ** -0.5 * 1.4426950408889634

NT_DIMS = (((1,), (1,)), ((), ()))


def _dot(a, b):
    return jnp.dot(a, b, preferred_element_type=F32)


def _dot_nt(a, b):
    return lax.dot_general(a, b, NT_DIMS, preferred_element_type=F32)


def _swap_halves(xc, half):
    lane = lax.broadcasted_iota(jnp.int32, xc.shape, 1)
    first = (lane % (2 * half)) < half
    return jnp.where(first, pltpu.roll(xc, LANES - half, 1), pltpu.roll(xc, half, 1))


def _inproj_kernel(x_ref, g1_ref, wmain_ref, widx_ref, wvt_ref, qg_ref, kg_ref,
                   qcos_ref, qsin_ref, icos_ref, isin_ref,
                   pool_ref, q_ref, k_ref, vt_ref, idx_ref, kcat_ref):
    x = x_ref[...]
    ms = jnp.mean(x * x, axis=-1, keepdims=True)
    h = x * lax.rsqrt(ms + EPS) * g1_ref[...]
    h_hi = h.astype(BF16)

    qcos = qcos_ref[...]
    qsin = qsin_ref[...]
    lane = lax.broadcasted_iota(jnp.int32, (x.shape[0], LANES), 1)
    first = lane < HEAD_DIM

    def norm_rope(xc, gain):
        sq = xc * xc
        s0 = jnp.sum(jnp.where(first, sq, 0.0), axis=-1, keepdims=True)
        s1 = jnp.sum(jnp.where(first, 0.0, sq), axis=-1, keepdims=True)
        ss = jnp.where(first, s0, s1) * (1.0 / HEAD_DIM)
        xn = xc * lax.rsqrt(ss + EPS) * gain
        return xn * qcos + _swap_halves(xn, HEAD_DIM // 2) * qsin

    n_qcols = N_Q_HEADS * HEAD_DIM // LANES
    qk = _dot(h_hi, wmain_ref[:, POOL_WIDTH:])
    idxp = _dot(h_hi, widx_ref[...])

    for c in range(n_qcols):
        qr = norm_rope(qk[:, c * LANES:(c + 1) * LANES], qg_ref[...]) * Q_SCALE
        qs = pltpu.roll(qr, HEAD_DIM, 1)
        if (2 * c) // Q_PER_KV == 0:
            pair = (jnp.where(first, qr, 0.0), jnp.where(first, qs, 0.0))
        else:
            pair = (jnp.where(first, 0.0, qs), jnp.where(first, 0.0, qr))
        for i, qh in enumerate(pair):
            q_ref[:, (2 * c + i) * LANES:(2 * c + i + 1) * LANES] = qh.astype(BF16)
    k_ref[...] = norm_rope(qk[:, n_qcols * LANES:], kg_ref[...]).astype(BF16)

    for c in range(IDX_PROJ_W // LANES):
        sl = slice(c * LANES, (c + 1) * LANES)
        ic = idxp[:, sl]
        roped = ic * icos_ref[:, sl] + _swap_halves(ic, IDX_DIM // 2) * isin_ref[:, sl]
        if c < IDXW // LANES:
            idx_ref[:, sl] = roped
        else:
            kcat_ref[...] = roped.astype(BF16)

    pool_ref[...] = _dot(h_hi, wmain_ref[:, :POOL_WIDTH])
    vt = _dot_nt(wvt_ref[...], h_hi).astype(BF16)
    for i in range(vt_ref.shape[0]):
        vt_ref[i] = vt[:, i * DSA_ROWS:(i + 1) * DSA_ROWS]


def _inproj(x2, g1, wmain, widx, wvt, qg, kg, qcos, qsin, icos, isin, seq):
    n = x2.shape[0]
    rows = INPROJ_ROWS
    tper = seq // rows
    vblk = rows // DSA_ROWS
    row = lambda w: pl.BlockSpec((rows, w), lambda i: (i, 0))
    const = lambda a: pl.BlockSpec(a.shape, lambda i: (0, 0))
    table = lambda w: pl.BlockSpec((rows, w), lambda i: (i % tper, 0))
    return pl.pallas_call(
        _inproj_kernel,
        grid=(n // rows,),
        in_specs=[row(D_MODEL), const(g1), const(wmain), const(widx), const(wvt), const(qg), const(kg),
                  table(LANES), table(LANES), table(IDX_PROJ_W), table(IDX_PROJ_W)],
        out_specs=[row(POOL_WIDTH), row(QPAD), row(LANES),
                   pl.BlockSpec((vblk, LANES, DSA_ROWS), lambda i: (i, 0, 0)), row(IDXW), row(LANES)],
        out_shape=[jax.ShapeDtypeStruct((n, POOL_WIDTH), F32),
                   jax.ShapeDtypeStruct((n, QPAD), BF16),
                   jax.ShapeDtypeStruct((n, LANES), BF16),
                   jax.ShapeDtypeStruct((n // DSA_ROWS, LANES, DSA_ROWS), BF16),
                   jax.ShapeDtypeStruct((n, IDXW), F32),
                   jax.ShapeDtypeStruct((n, LANES), BF16)],
        compiler_params=pltpu.CompilerParams(dimension_semantics=("arbitrary",),
                                             vmem_limit_bytes=48 * 1024 * 1024),
        name="inproj",
    )(x2, g1, wmain, widx, wvt, qg, kg, qcos, qsin, icos, isin)


def _key_to_float(key):
    bits = key ^ ((key >> 31) & jnp.int32(0x7FFFFFFF))
    return lax.bitcast_convert_type(bits, F32)


def _fold_groups(x, op, ways=4):
    n = x.shape[0] // SUBLANES
    accs = [x[g * SUBLANES:(g + 1) * SUBLANES, :] for g in range(ways)]
    for g in range(ways, n):
        accs[g % ways] = op(accs[g % ways], x[g * SUBLANES:(g + 1) * SUBLANES, :])
    while len(accs) > 1:
        accs = [op(accs[i], accs[i + 1]) for i in range(0, len(accs), 2)]
    return accs[0]


def _fold_sublanes(x, op):
    for shift in (4, 2, 1):
        x = op(x, pltpu.roll(x, shift, 0))
    return x


def _chunk_loop(n, body, init, unroll=4):
    def run(start, width, carry):
        for w in range(width):
            carry = body(start + w, carry)
        return carry

    carry = lax.fori_loop(0, n // unroll, lambda i, c: run(unroll * i, unroll, c), init)
    done = (n // unroll) * unroll
    width = unroll // 2
    while width:
        carry = lax.cond((n & width) != 0, functools.partial(run, done, width), lambda c: c, carry)
        done = done + (n & width)
        width //= 2
    return carry


def _dsa_kernel(q_ref, idxq_ref, kcat_ref, k_ref, vt_ref, out_ref, sc_ref, qcat_ref, acc_ref, p_ref, lg_ref,
                *, seq, topk):
    rows = DSA_ROWS
    ck = DSA_ROWS
    ngrp = ck // SUBLANES
    qb = pl.program_id(1)
    nch = qb + 1
    lane = lax.broadcasted_iota(jnp.int32, (rows, LANES), 1)
    qidx = qb * rows + lax.broadcasted_iota(jnp.int32, (ck, rows), 1)
    krow = lax.broadcasted_iota(jnp.int32, (ck, rows), 0)

    slots = LANES // IDX_DIM
    for c in range(IDX_HEADS // slots):
        col = idxq_ref[:, c * LANES:(c + 1) * LANES]
        for r in range(slots):
            live = jnp.logical_and(lane >= r * IDX_DIM, lane < (r + 1) * IDX_DIM)
            qcat_ref[c * slots + r] = jnp.where(live, col, 0.0).astype(BF16)
    wt = idxq_ref[:, 2 * LANES:].T[0:IDX_HEADS, :]

    def score_body(j, nonneg):
        kc = kcat_ref[pl.ds(pl.multiple_of(j * ck, ck), ck), :]
        acc = jnp.zeros((ck, rows), F32)
        for hd in range(IDX_HEADS):
            y = _dot_nt(kc, qcat_ref[hd])
            acc = acc + wt[hd:hd + 1, :] * jnp.maximum(y, 0.0)
        sc = jnp.where(j * ck + krow <= qidx, acc, NEG)
        sc_ref[j] = sc
        return nonneg + _fold_groups(jnp.where(sc >= 0.0, 1.0, 0.0), jnp.add)

    nonneg = _chunk_loop(nch, score_body, jnp.zeros((SUBLANES, rows), F32))

    n_virtual = (seq - nch * ck).astype(F32)

    def count(pred, thr):
        def body(j, accs):
            accs = list(accs)
            for g in range(ngrp):
                hit = pred(sc_ref[j, g * SUBLANES:(g + 1) * SUBLANES, :], thr)
                accs[g % len(accs)] = accs[g % len(accs)] + jnp.where(hit, 1.0, 0.0)
            return tuple(accs)
        zero = jnp.zeros((SUBLANES, rows), F32)
        accs = _chunk_loop(nch, body, (zero, zero, zero, zero))
        tot = _fold_sublanes((accs[0] + accs[1]) + (accs[2] + accs[3]), jnp.add)
        return tot + jnp.where(pred(jnp.float32(NEG), thr), n_virtual, 0.0)

    ge = lambda a, b: a >= b
    gt = lambda a, b: a > b
    kf = jnp.float32(topk)

    key = jnp.where(_fold_sublanes(nonneg, jnp.add) >= kf, jnp.int32(0), jnp.int32(-2 ** 31))

    def bit_body(i, key):
        cand = key | jnp.left_shift(jnp.int32(1), 30 - i)
        cnt = count(ge, _key_to_float(cand))
        return jnp.where(cnt >= kf, cand, key)

    key = lax.fori_loop(0, 31, bit_body, key)
    thr8 = _key_to_float(key)
    thr = thr8[0:1, :]
    need = (kf - count(gt, thr8))[0:1, :]

    ki = lax.broadcasted_iota(jnp.int32, (ck, ck), 0)
    kj = lax.broadcasted_iota(jnp.int32, (ck, ck), 1)
    lower = jnp.where(kj < ki, 1.0, 0.0).astype(BF16)
    ones8 = jnp.ones((SUBLANES, ck), BF16)

    def select_body(j, carry):
        run, ms = carry
        sc = sc_ref[j]
        eq = sc == thr
        eqb = jnp.where(eq, 1.0, 0.0).astype(BF16)
        prior = run[0:1, :] + _dot(lower, eqb)
        tie = jnp.where(prior < need, 0.0, NEG)
        bias = jnp.where(sc > thr, 0.0, jnp.where(eq, tie, NEG))
        sc_ref[j] = jnp.where(j * ck + krow <= qidx, bias, NEG)
        kc = k_ref[pl.ds(pl.multiple_of(j * ck, ck), ck), :]
        out = []
        for hd in range(N_Q_HEADS):
            s = _dot_nt(kc, q_ref[:, hd * LANES:(hd + 1) * LANES]) + sc_ref[j]
            lg_ref[hd, j] = s
            out.append(jnp.maximum(ms[hd], _fold_groups(s, jnp.maximum)))
        return run + _dot(ones8, eqb), tuple(out)

    _, ms = _chunk_loop(nch, select_body,
                        (jnp.zeros((SUBLANES, rows), F32),
                         tuple(jnp.full((SUBLANES, rows), -jnp.inf, F32) for _ in range(N_Q_HEADS))))
    ms = [_fold_sublanes(m, jnp.maximum)[0:1, :] for m in ms]

    acc_ref[...] = jnp.zeros(acc_ref.shape, F32)

    def pv_body(j, ls):
        out = []
        for hd in range(N_Q_HEADS):
            p = jnp.exp2(lg_ref[hd, j] - ms[hd])
            out.append(ls[hd] + _fold_groups(p, jnp.add))
            p_ref[hd] = p.astype(BF16)
        vt = vt_ref[j]
        for hd in range(N_Q_HEADS):
            acc_ref[hd] += _dot(vt, p_ref[hd])
        return tuple(out)

    ls = _chunk_loop(nch, pv_body, tuple(jnp.zeros((SUBLANES, rows), F32) for _ in range(N_Q_HEADS)))

    outs = []
    for hd in range(N_Q_HEADS):
        l = _fold_sublanes(ls[hd], jnp.add)
        r0 = HEAD_DIM * (hd // Q_PER_KV)
        outs.append(acc_ref[hd, r0:r0 + HEAD_DIM, :] / l[0:1, :])
    out_ref[...] = jnp.concatenate(outs, axis=0).T.astype(BF16)


def _dsa(q, idx, kcat, k, vt, batch, seq):
    rows = DSA_ROWS
    nq = seq // rows
    topk = min(TOPK_MAX, seq // 4)
    assert topk <= rows and seq % rows == 0
    q3 = q.reshape(batch, seq, QPAD)
    idx3 = idx.reshape(batch, seq, IDXW)
    vt = vt.reshape(batch, nq, LANES, rows)
    full = pl.BlockSpec((None, seq, LANES), lambda b, i: (b, 0, 0))
    kern = functools.partial(_dsa_kernel, seq=seq, topk=topk)
    return pl.pallas_call(
        kern,
        grid=(batch, nq),
        in_specs=[pl.BlockSpec((None, rows, QPAD), lambda b, i: (b, i, 0)),
                  pl.BlockSpec((None, rows, IDXW), lambda b, i: (b, i, 0)),
                  full, full,
                  pl.BlockSpec((None, nq, LANES, rows), lambda b, i: (b, 0, 0, 0))],
        out_specs=pl.BlockSpec((None, rows, N_Q_HEADS * HEAD_DIM), lambda b, i: (b, i, 0)),
        out_shape=jax.ShapeDtypeStruct((batch, seq, N_Q_HEADS * HEAD_DIM), BF16),
        scratch_shapes=[pltpu.VMEM((nq, rows, rows), F32),
                        pltpu.VMEM((IDX_HEADS, rows, LANES), BF16),
                        pltpu.VMEM((N_Q_HEADS, LANES, rows), F32),
                        pltpu.VMEM((N_Q_HEADS, rows, rows), BF16),
                        pltpu.VMEM((N_Q_HEADS, nq, rows, rows), F32)],
        compiler_params=pltpu.CompilerParams(dimension_semantics=("arbitrary", "arbitrary"),
                                             vmem_limit_bytes=48 * 1024 * 1024),
        name="dsa",
    )(q3, idx3, kcat.reshape(batch, seq, LANES), k.reshape(batch, seq, LANES), vt)


def _mixffn_kernel(x_ref, xp_ref, pool_ref, pp_ref, b_ref, bp_ref, pw_ref, ps_ref, woa_ref, wob_ref,
                   g2_ref, wup_ref, cw_ref, cb_ref, wdn_ref, out_ref, act_ref):
    rows = FFN_ROWS
    ext = MIX_HALO + rows
    ti = pl.program_id(1)
    row = lax.broadcasted_iota(jnp.int32, (ext, 1), 0)
    t = ti * rows - MIX_HALO + row

    pool_ext = jnp.concatenate([jnp.where(ti == 0, 0.0, pp_ref[...]), pool_ref[...]], axis=0)
    lead = POOL_BACK - MIX_HALO
    parts = []
    for g, w in enumerate(POOL_WINDOWS):
        sl = slice(g * POOL_GROUP, (g + 1) * POOL_GROUP)
        a = pool_ext[:, sl]
        span = 1
        while span < w:
            a = a + pltpu.roll(a, span, 0)
            span *= 2
        cnt = jnp.clip(t + 1, 1, w).astype(F32)
        p = a[lead:] / cnt - pool_ext[lead:, sl]
        parts.append(_dot(p.astype(BF16), pw_ref[g]) * ps_ref[:, sl])
    a_out = jnp.concatenate(parts, axis=1).astype(BF16)

    x_ext = jnp.concatenate([xp_ref[...], x_ref[...]], axis=0)
    b_ext = jnp.concatenate([bp_ref[...], b_ref[...]], axis=0)
    x1 = x_ext + _dot(a_out, woa_ref[...]) + _dot(b_ext, wob_ref[...])

    h = x1 * lax.rsqrt(jnp.mean(x1 * x1, axis=-1, keepdims=True) + EPS) * g2_ref[...]
    h_ext = jnp.where(jnp.logical_and(ti == 0, row < MIX_HALO), 0.0, h).astype(BF16)

    chunk = 2 * LANES
    for c in range(D_FF // chunk):
        halves = []
        for off in (0, D_FF):
            sl = slice(off + c * chunk, off + (c + 1) * chunk)
            u = _dot(h_ext, wup_ref[:, sl])
            u1 = pltpu.roll(u, 1, 0)
            u2 = pltpu.roll(u, 2, 0)
            cv = cb_ref[:, sl] + u2[MIX_HALO:] * cw_ref[0:1, sl]
            cv = cv + u1[MIX_HALO:] * cw_ref[1:2, sl]
            cv = cv + u[MIX_HALO:] * cw_ref[2:3, sl]
            halves.append(cv)
        gate, val = halves
        act = gate * (1.0 / (1.0 + jnp.exp(-gate))) * val
        act_ref[:, c * chunk:(c + 1) * chunk] = act.astype(BF16)
    out_ref[...] = x1[MIX_HALO:] + _dot(act_ref[...], wdn_ref[...])


def _mixffn(x3, pool3, b_out, pw, ps, woa, wob, g2, wup, cw, cb, wdn):
    batch, seq, _ = x3.shape
    rows = FFN_ROWS
    tile = lambda w: pl.BlockSpec((None, rows, w), lambda b, i: (b, i, 0))
    back = lambda h, w: pl.BlockSpec((None, h, w), lambda b, i: (b, jnp.maximum(i * (rows // h) - 1, 0), 0))
    const = lambda a: pl.BlockSpec(a.shape, lambda b, i: (0,) * a.ndim, pipeline_mode=pl.Buffered(1))
    return pl.pallas_call(
        _mixffn_kernel,
        grid=(batch, seq // rows),
        in_specs=[tile(D_MODEL), back(MIX_HALO, D_MODEL), tile(POOL_WIDTH), back(POOL_BACK, POOL_WIDTH),
                  tile(POOL_WIDTH), back(MIX_HALO, POOL_WIDTH),
                  const(pw), const(ps), const(woa), const(wob), const(g2), const(wup), const(cw), const(cb),
                  const(wdn)],
        out_specs=tile(D_MODEL),
        out_shape=jax.ShapeDtypeStruct((batch, seq, D_MODEL), F32),
        scratch_shapes=[pltpu.VMEM((rows, D_FF), BF16)],
        compiler_params=pltpu.CompilerParams(dimension_semantics=("arbitrary", "arbitrary"),
                                             vmem_limit_bytes=56 * 1024 * 1024),
        name="mixffn",
    )(x3, x3, pool3, pool3, b_out, b_out, pw, ps, woa, wob, g2, wup, cw, cb, wdn)


def _rope_tables(seq, dim):
    half = dim // 2
    inv = jnp.exp(-jnp.log(jnp.float32(ROPE_THETA)) * jnp.arange(half, dtype=F32) / half)
    ang = jnp.arange(seq).astype(F32)[:, None] * inv[None, :]
    cos, sin = jnp.cos(ang), jnp.sin(ang)
    return jnp.concatenate([cos, cos], axis=1), jnp.concatenate([-sin, sin], axis=1)


def _layer(x, norm1_g, w_in, q_norm_g, k_norm_g, pool_w, pool_scale, w_out, norm2_g, w_up, conv_w, conv_b, w_down):
    batch, seq, _ = x.shape
    n = batch * seq
    v0 = MAIN_W - N_KV_HEADS * HEAD_DIM
    wmain = w_in[:, :v0].astype(BF16)
    wvt = w_in[:, v0:MAIN_W].T.astype(BF16)
    w_qi, w_ki, w_wi = jnp.split(w_in[:, MAIN_W:], [IDX_HEADS * IDX_DIM, (IDX_HEADS + 1) * IDX_DIM], axis=1)
    qg = jnp.concatenate([q_norm_g, q_norm_g])[None, :]
    kg = jnp.concatenate([k_norm_g, k_norm_g])[None, :]
    slots = LANES // IDX_DIM
    w_idx = jnp.concatenate([w_qi, w_wi, jnp.zeros((D_MODEL, LANES - IDX_HEADS), F32)] + [w_ki] * slots, axis=1)

    c64, s64 = _rope_tables(seq, HEAD_DIM)
    qcos, qsin = jnp.tile(c64, (1, LANES // HEAD_DIM)), jnp.tile(s64, (1, LANES // HEAD_DIM))
    c32, s32 = _rope_tables(seq, IDX_DIM)
    wscale = jnp.full((seq, IDX_HEADS), IDX_HEADS ** -0.5 * IDX_DIM ** -0.5, F32)
    gap = jnp.zeros((seq, LANES - IDX_HEADS), F32)
    icos = jnp.concatenate([jnp.tile(c32, (1, IDX_HEADS)), wscale, gap, jnp.tile(c32, (1, slots))], axis=1)
    isin = jnp.concatenate([jnp.tile(s32, (1, IDX_HEADS)), jnp.zeros((seq, LANES), F32),
                            jnp.tile(s32, (1, slots))], axis=1)

    x2 = x.reshape(n, D_MODEL)
    pool, q, k, vt, idx, kcat = _inproj(x2, norm1_g[None, :], wmain, w_idx.astype(BF16), wvt, qg, kg,
                                        qcos, qsin, icos, isin, seq)
    b_out = _dsa(q, idx, kcat, k, vt, batch, seq)
    return _mixffn(x, pool.reshape(batch, seq, POOL_WIDTH), b_out, pool_w.astype(BF16), pool_scale[None, :],
                   w_out[:POOL_WIDTH].astype(BF16), w_out[POOL_WIDTH:].astype(BF16), norm2_g[None, :],
                   w_up.astype(BF16), conv_w, conv_b[None, :], w_down.astype(BF16))


def kernel(x, norm1_g, w_in, q_norm_g, k_norm_g, pool_w, pool_scale, w_out, norm2_g, w_up, conv_w, conv_b, w_down):
    for l in range(norm1_g.shape[0]):
        x = _layer(x, norm1_g[l], w_in[l], q_norm_g[l], k_norm_g[l], pool_w[l], pool_scale[l], w_out[l],
                   norm2_g[l], w_up[l], conv_w[l], conv_b[l], w_down[l])
    return x
```

```python
import functools

import jax
import jax.numpy as jnp
from jax import lax
from jax.experimental import pallas as pl
from jax.experimental.pallas import tpu as pltpu

F32 = jnp.float32
BF16 = jnp.bfloat16

D_MODEL = 1024
POOL_WIDTH = 512
POOL_WINDOWS = (2, 4, 8, 16)
POOL_GROUP = 128
HEAD_DIM = 64
N_Q_HEADS = 8
N_KV_HEADS = 2
Q_PER_KV = 4
IDX_HEADS = 8
IDX_DIM = 32
TOPK_MAX = 256
ROPE_THETA = 10000.0
D_FF = 2816
CONV_WIDTH = 3
EPS = 1e-6
NEG = -1e30

LANES = 128
SUBLANES = 8
QPAD = N_Q_HEADS * LANES
IDXW = 3 * LANES
IDX_PROJ_W = IDXW + LANES
MAIN_W = POOL_WIDTH + (N_Q_HEADS + 2 * N_KV_HEADS) * HEAD_DIM

INPROJ_ROWS = 512
DSA_ROWS = 256
FFN_ROWS = 512
MIX_HALO = 16
POOL_BACK = 32

Q_SCALE = HEAD_DIM ** -0.5 * 1.4426950408889634

NT_DIMS = (((1,), (1,)), ((), ()))


def _dot(a, b):
    return jnp.dot(a, b, preferred_element_type=F32)


def _dot_nt(a, b):
    return lax.dot_general(a, b, NT_DIMS, preferred_element_type=F32)


def _swap_halves(xc, half):
    lane = lax.broadcasted_iota(jnp.int32, xc.shape, 1)
    first = (lane % (2 * half)) < half
    return jnp.where(first, pltpu.roll(xc, LANES - half, 1), pltpu.roll(xc, half, 1))


def _inproj_kernel(x_ref, g1_ref, wmain_ref, widx_ref, wvt_ref, qg_ref, kg_ref,
                   qcos_ref, qsin_ref, icos_ref, isin_ref,
                   pool_ref, q_ref, k_ref, vt_ref, idx_ref, kcat_ref):
    x = x_ref[...]
    ms = jnp.mean(x * x, axis=-1, keepdims=True)
    h = x * lax.rsqrt(ms + EPS) * g1_ref[...]
    h_hi = h.astype(BF16)

    qcos = qcos_ref[...]
    qsin = qsin_ref[...]
    lane = lax.broadcasted_iota(jnp.int32, (x.shape[0], LANES), 1)
    first = lane < HEAD_DIM

    def norm_rope(xc, gain):
        sq = xc * xc
        s0 = jnp.sum(jnp.where(first, sq, 0.0), axis=-1, keepdims=True)
        s1 = jnp.sum(jnp.where(first, 0.0, sq), axis=-1, keepdims=True)
        ss = jnp.where(first, s0, s1) * (1.0 / HEAD_DIM)
        xn = xc * lax.rsqrt(ss + EPS) * gain
        return xn * qcos + _swap_halves(xn, HEAD_DIM // 2) * qsin

    n_qcols = N_Q_HEADS * HEAD_DIM // LANES
    qk = _dot(h_hi, wmain_ref[:, POOL_WIDTH:])
    idxp = _dot(h_hi, widx_ref[...])

    for c in range(n_qcols):
        qr = norm_rope(qk[:, c * LANES:(c + 1) * LANES], qg_ref[...]) * Q_SCALE
        qs = pltpu.roll(qr, HEAD_DIM, 1)
        if (2 * c) // Q_PER_KV == 0:
            pair = (jnp.where(first, qr, 0.0), jnp.where(first, qs, 0.0))
        else:
            pair = (jnp.where(first, 0.0, qs), jnp.where(first, 0.0, qr))
        for i, qh in enumerate(pair):
            q_ref[:, (2 * c + i) * LANES:(2 * c + i + 1) * LANES] = qh.astype(BF16)
    k_ref[...] = norm_rope(qk[:, n_qcols * LANES:], kg_ref[...]).astype(BF16)

    for c in range(IDX_PROJ_W // LANES):
        sl = slice(c * LANES, (c + 1) * LANES)
        ic = idxp[:, sl]
        roped = ic * icos_ref[:, sl] + _swap_halves(ic, IDX_DIM // 2) * isin_ref[:, sl]
        if c < IDXW // LANES:
            idx_ref[:, sl] = roped
        else:
            kcat_ref[...] = roped.astype(BF16)

    pool_ref[...] = _dot(h_hi, wmain_ref[:, :POOL_WIDTH])
    vt = _dot_nt(wvt_ref[...], h_hi).astype(BF16)
    for i in range(vt_ref.shape[0]):
        vt_ref[i] = vt[:, i * DSA_ROWS:(i + 1) * DSA_ROWS]


def _inproj(x2, g1, wmain, widx, wvt, qg, kg, qcos, qsin, icos, isin, seq):
    n = x2.shape[0]
    rows = INPROJ_ROWS
    tper = seq // rows
    vblk = rows // DSA_ROWS
    row = lambda w: pl.BlockSpec((rows, w), lambda i: (i, 0))
    const = lambda a: pl.BlockSpec(a.shape, lambda i: (0, 0))
    table = lambda w: pl.BlockSpec((rows, w), lambda i: (i % tper, 0))
    return pl.pallas_call(
        _inproj_kernel,
        grid=(n // rows,),
        in_specs=[row(D_MODEL), const(g1), const(wmain), const(widx), const(wvt), const(qg), const(kg),
                  table(LANES), table(LANES), table(IDX_PROJ_W), table(IDX_PROJ_W)],
        out_specs=[row(POOL_WIDTH), row(QPAD), row(LANES),
                   pl.BlockSpec((vblk, LANES, DSA_ROWS), lambda i: (i, 0, 0)), row(IDXW), row(LANES)],
        out_shape=[jax.ShapeDtypeStruct((n, POOL_WIDTH), F32),
                   jax.ShapeDtypeStruct((n, QPAD), BF16),
                   jax.ShapeDtypeStruct((n, LANES), BF16),
                   jax.ShapeDtypeStruct((n // DSA_ROWS, LANES, DSA_ROWS), BF16),
                   jax.ShapeDtypeStruct((n, IDXW), F32),
                   jax.ShapeDtypeStruct((n, LANES), BF16)],
        compiler_params=pltpu.CompilerParams(dimension_semantics=("arbitrary",),
                                             vmem_limit_bytes=48 * 1024 * 1024),
        name="inproj",
    )(x2, g1, wmain, widx, wvt, qg, kg, qcos, qsin, icos, isin)


def _key_to_float(key):
    bits = key ^ ((key >> 31) & jnp.int32(0x7FFFFFFF))
    return lax.bitcast_convert_type(bits, F32)


def _fold_groups(x, op, ways=4):
    n = x.shape[0] // SUBLANES
    accs = [x[g * SUBLANES:(g + 1) * SUBLANES, :] for g in range(ways)]
    for g in range(ways, n):
        accs[g % ways] = op(accs[g % ways], x[g * SUBLANES:(g + 1) * SUBLANES, :])
    while len(accs) > 1:
        accs = [op(accs[i], accs[i + 1]) for i in range(0, len(accs), 2)]
    return accs[0]


def _fold_sublanes(x, op):
    for shift in (4, 2, 1):
        x = op(x, pltpu.roll(x, shift, 0))
    return x


def _chunk_loop(n, body, init, max_n):
    def unrolled(k):
        def run(carry):
            for j in range(k):
                carry = body(j, carry)
            return carry
        return run

    return lax.switch(n - 1, [unrolled(k) for k in range(1, max_n + 1)], init)


def _dsa_kernel(q_ref, idxq_ref, kcat_ref, k_ref, vt_ref, out_ref, sc_ref, qcat_ref, acc_ref, p_ref, lg_ref,
                *, seq, topk):
    rows = DSA_ROWS
    ck = DSA_ROWS
    ngrp = ck // SUBLANES
    qb = pl.program_id(1)
    nch = qb + 1
    lane = lax.broadcasted_iota(jnp.int32, (rows, LANES), 1)
    qidx = qb * rows + lax.broadcasted_iota(jnp.int32, (ck, rows), 1)
    krow = lax.broadcasted_iota(jnp.int32, (ck, rows), 0)

    slots = LANES // IDX_DIM
    for c in range(IDX_HEADS // slots):
        col = idxq_ref[:, c * LANES:(c + 1) * LANES]
        for r in range(slots):
            live = jnp.logical_and(lane >= r * IDX_DIM, lane < (r + 1) * IDX_DIM)
            qcat_ref[c * slots + r] = jnp.where(live, col, 0.0).astype(BF16)
    wt = idxq_ref[:, 2 * LANES:].T[0:IDX_HEADS, :]

    def score_body(j, nonneg):
        kc = kcat_ref[j * ck:(j + 1) * ck, :]
        acc = jnp.zeros((ck, rows), F32)
        for hd in range(IDX_HEADS):
            y = _dot_nt(kc, qcat_ref[hd])
            acc = acc + wt[hd:hd + 1, :] * jnp.maximum(y, 0.0)
        sc = jnp.where(j * ck + krow <= qidx, acc, NEG)
        sc_ref[j] = sc
        return nonneg + _fold_groups(jnp.where(sc >= 0.0, 1.0, 0.0), jnp.add)

    nq = seq // ck
    nonneg = _chunk_loop(nch, score_body, jnp.zeros((SUBLANES, rows), F32), nq)

    n_virtual = (seq - nch * ck).astype(F32)

    def count(pred, thr):
        def body(j, accs):
            accs = list(accs)
            for g in range(ngrp):
                hit = pred(sc_ref[j, g * SUBLANES:(g + 1) * SUBLANES, :], thr)
                accs[g % len(accs)] = accs[g % len(accs)] + jnp.where(hit, 1.0, 0.0)
            return tuple(accs)
        zero = jnp.zeros((SUBLANES, rows), F32)
        accs = _chunk_loop(nch, body, (zero, zero, zero, zero), nq)
        tot = _fold_sublanes((accs[0] + accs[1]) + (accs[2] + accs[3]), jnp.add)
        return tot + jnp.where(pred(jnp.float32(NEG), thr), n_virtual, 0.0)

    ge = lambda a, b: a >= b
    gt = lambda a, b: a > b
    kf = jnp.float32(topk)

    key = jnp.where(_fold_sublanes(nonneg, jnp.add) >= kf, jnp.int32(0), jnp.int32(-2 ** 31))

    def bit_body(i, key):
        cand = key | jnp.left_shift(jnp.int32(1), 30 - i)
        cnt = count(ge, _key_to_float(cand))
        return jnp.where(cnt >= kf, cand, key)

    key = lax.fori_loop(0, 31, bit_body, key)
    thr8 = _key_to_float(key)
    thr = thr8[0:1, :]
    need = (kf - count(gt, thr8))[0:1, :]

    ki = lax.broadcasted_iota(jnp.int32, (ck, ck), 0)
    kj = lax.broadcasted_iota(jnp.int32, (ck, ck), 1)
    lower = jnp.where(kj < ki, 1.0, 0.0).astype(BF16)
    ones8 = jnp.ones((SUBLANES, ck), BF16)

    def select_body(j, carry):
        run, ms = carry
        sc = sc_ref[j]
        eq = sc == thr
        eqb = jnp.where(eq, 1.0, 0.0).astype(BF16)
        prior = run[0:1, :] + _dot(lower, eqb)
        tie = jnp.where(prior < need, 0.0, NEG)
        bias = jnp.where(sc > thr, 0.0, jnp.where(eq, tie, NEG))
        sc_ref[j] = jnp.where(j * ck + krow <= qidx, bias, NEG)
        kc = k_ref[j * ck:(j + 1) * ck, :]
        out = []
        for hd in range(N_Q_HEADS):
            s = _dot_nt(kc, q_ref[:, hd * LANES:(hd + 1) * LANES]) + sc_ref[j]
            lg_ref[hd, j] = s
            out.append(jnp.maximum(ms[hd], _fold_groups(s, jnp.maximum)))
        return run + _dot(ones8, eqb), tuple(out)

    _, ms = _chunk_loop(nch, select_body,
                        (jnp.zeros((SUBLANES, rows), F32),
                         tuple(jnp.full((SUBLANES, rows), -jnp.inf, F32) for _ in range(N_Q_HEADS))), nq)
    ms = [_fold_sublanes(m, jnp.maximum)[0:1, :] for m in ms]

    acc_ref[...] = jnp.zeros(acc_ref.shape, F32)

    def pv_body(j, ls):
        out = []
        for hd in range(N_Q_HEADS):
            p = jnp.exp2(lg_ref[hd, j] - ms[hd])
            out.append(ls[hd] + _fold_groups(p, jnp.add))
            p_ref[hd] = p.astype(BF16)
        vt = vt_ref[j]
        for hd in range(N_Q_HEADS):
            acc_ref[hd] += _dot(vt, p_ref[hd])
        return tuple(out)

    ls = _chunk_loop(nch, pv_body, tuple(jnp.zeros((SUBLANES, rows), F32) for _ in range(N_Q_HEADS)), nq)

    outs = []
    for hd in range(N_Q_HEADS):
        l = _fold_sublanes(ls[hd], jnp.add)
        r0 = HEAD_DIM * (hd // Q_PER_KV)
        outs.append(acc_ref[hd, r0:r0 + HEAD_DIM, :] / l[0:1, :])
    out_ref[...] = jnp.concatenate(outs, axis=0).T.astype(BF16)


def _dsa(q, idx, kcat, k, vt, batch, seq):
    rows = DSA_ROWS
    nq = seq // rows
    topk = min(TOPK_MAX, seq // 4)
    assert topk <= rows and seq % rows == 0
    q3 = q.reshape(batch, seq, QPAD)
    idx3 = idx.reshape(batch, seq, IDXW)
    vt = vt.reshape(batch, nq, LANES, rows)
    full = pl.BlockSpec((None, seq, LANES), lambda b, i: (b, 0, 0))
    kern = functools.partial(_dsa_kernel, seq=seq, topk=topk)
    return pl.pallas_call(
        kern,
        grid=(batch, nq),
        in_specs=[pl.BlockSpec((None, rows, QPAD), lambda b, i: (b, i, 0)),
                  pl.BlockSpec((None, rows, IDXW), lambda b, i: (b, i, 0)),
                  full, full,
                  pl.BlockSpec((None, nq, LANES, rows), lambda b, i: (b, 0, 0, 0))],
        out_specs=pl.BlockSpec((None, rows, N_Q_HEADS * HEAD_DIM), lambda b, i: (b, i, 0)),
        out_shape=jax.ShapeDtypeStruct((batch, seq, N_Q_HEADS * HEAD_DIM), BF16),
        scratch_shapes=[pltpu.VMEM((nq, rows, rows), F32),
                        pltpu.VMEM((IDX_HEADS, rows, LANES), BF16),
                        pltpu.VMEM((N_Q_HEADS, LANES, rows), F32),
                        pltpu.VMEM((N_Q_HEADS, rows, rows), BF16),
                        pltpu.VMEM((N_Q_HEADS, nq, rows, rows), F32)],
        compiler_params=pltpu.CompilerParams(dimension_semantics=("arbitrary", "arbitrary"),
                                             vmem_limit_bytes=48 * 1024 * 1024),
        name="dsa",
    )(q3, idx3, kcat.reshape(batch, seq, LANES), k.reshape(batch, seq, LANES), vt)


def _mixffn_kernel(x_ref, xp_ref, pool_ref, pp_ref, b_ref, bp_ref, pw_ref, ps_ref, woa_ref, wob_ref,
                   g2_ref, wup_ref, cw_ref, cb_ref, wdn_ref, out_ref, act_ref):
    rows = FFN_ROWS
    ext = MIX_HALO + rows
    ti = pl.program_id(1)
    row = lax.broadcasted_iota(jnp.int32, (ext, 1), 0)
    t = ti * rows - MIX_HALO + row

    pool_ext = jnp.concatenate([jnp.where(ti == 0, 0.0, pp_ref[...]), pool_ref[...]], axis=0)
    lead = POOL_BACK - MIX_HALO
    parts = []
    for g, w in enumerate(POOL_WINDOWS):
        sl = slice(g * POOL_GROUP, (g + 1) * POOL_GROUP)
        a = pool_ext[:, sl]
        span = 1
        while span < w:
            a = a + pltpu.roll(a, span, 0)
            span *= 2
        cnt = jnp.clip(t + 1, 1, w).astype(F32)
        p = a[lead:] / cnt - pool_ext[lead:, sl]
        parts.append(_dot(p.astype(BF16), pw_ref[g]) * ps_ref[:, sl])
    a_out = jnp.concatenate(parts, axis=1).astype(BF16)

    x_ext = jnp.concatenate([xp_ref[...], x_ref[...]], axis=0)
    b_ext = jnp.concatenate([bp_ref[...], b_ref[...]], axis=0)
    x1 = x_ext + _dot(a_out, woa_ref[...]) + _dot(b_ext, wob_ref[...])

    h = x1 * lax.rsqrt(jnp.mean(x1 * x1, axis=-1, keepdims=True) + EPS) * g2_ref[...]
    h_ext = jnp.where(jnp.logical_and(ti == 0, row < MIX_HALO), 0.0, h).astype(BF16)

    chunk = 2 * LANES
    for c in range(D_FF // chunk):
        halves = []
        for off in (0, D_FF):
            sl = slice(off + c * chunk, off + (c + 1) * chunk)
            u = _dot(h_ext, wup_ref[:, sl])
            u1 = pltpu.roll(u, 1, 0)
            u2 = pltpu.roll(u, 2, 0)
            cv = cb_ref[:, sl] + u2[MIX_HALO:] * cw_ref[0:1, sl]
            cv = cv + u1[MIX_HALO:] * cw_ref[1:2, sl]
            cv = cv + u[MIX_HALO:] * cw_ref[2:3, sl]
            halves.append(cv)
        gate, val = halves
        act = gate * (1.0 / (1.0 + jnp.exp(-gate))) * val
        act_ref[:, c * chunk:(c + 1) * chunk] = act.astype(BF16)
    out_ref[...] = x1[MIX_HALO:] + _dot(act_ref[...], wdn_ref[...])


def _mixffn(x3, pool3, b_out, pw, ps, woa, wob, g2, wup, cw, cb, wdn):
    batch, seq, _ = x3.shape
    rows = FFN_ROWS
    tile = lambda w: pl.BlockSpec((None, rows, w), lambda b, i: (b, i, 0))
    back = lambda h, w: pl.BlockSpec((None, h, w), lambda b, i: (b, jnp.maximum(i * (rows // h) - 1, 0), 0))
    const = lambda a: pl.BlockSpec(a.shape, lambda b, i: (0,) * a.ndim, pipeline_mode=pl.Buffered(1))
    return pl.pallas_call(
        _mixffn_kernel,
        grid=(batch, seq // rows),
        in_specs=[tile(D_MODEL), back(MIX_HALO, D_MODEL), tile(POOL_WIDTH), back(POOL_BACK, POOL_WIDTH),
                  tile(POOL_WIDTH), back(MIX_HALO, POOL_WIDTH),
                  const(pw), const(ps), const(woa), const(wob), const(g2), const(wup), const(cw), const(cb),
                  const(wdn)],
        out_specs=tile(D_MODEL),
        out_shape=jax.ShapeDtypeStruct((batch, seq, D_MODEL), F32),
        scratch_shapes=[pltpu.VMEM((rows, D_FF), BF16)],
        compiler_params=pltpu.CompilerParams(dimension_semantics=("arbitrary", "arbitrary"),
                                             vmem_limit_bytes=56 * 1024 * 1024),
        name="mixffn",
    )(x3, x3, pool3, pool3, b_out, b_out, pw, ps, woa, wob, g2, wup, cw, cb, wdn)


def _rope_tables(seq, dim):
    half = dim // 2
    inv = jnp.exp(-jnp.log(jnp.float32(ROPE_THETA)) * jnp.arange(half, dtype=F32) / half)
    ang = jnp.arange(seq).astype(F32)[:, None] * inv[None, :]
    cos, sin = jnp.cos(ang), jnp.sin(ang)
    return jnp.concatenate([cos, cos], axis=1), jnp.concatenate([-sin, sin], axis=1)


def _layer(x, norm1_g, w_in, q_norm_g, k_norm_g, pool_w, pool_scale, w_out, norm2_g, w_up, conv_w, conv_b, w_down):
    batch, seq, _ = x.shape
    n = batch * seq
    v0 = MAIN_W - N_KV_HEADS * HEAD_DIM
    wmain = w_in[:, :v0].astype(BF16)
    wvt = w_in[:, v0:MAIN_W].T.astype(BF16)
    w_qi, w_ki, w_wi = jnp.split(w_in[:, MAIN_W:], [IDX_HEADS * IDX_DIM, (IDX_HEADS + 1) * IDX_DIM], axis=1)
    qg = jnp.concatenate([q_norm_g, q_norm_g])[None, :]
    kg = jnp.concatenate([k_norm_g, k_norm_g])[None, :]
    slots = LANES // IDX_DIM
    w_idx = jnp.concatenate([w_qi, w_wi, jnp.zeros((D_MODEL, LANES - IDX_HEADS), F32)] + [w_ki] * slots, axis=1)

    c64, s64 = _rope_tables(seq, HEAD_DIM)
    qcos, qsin = jnp.tile(c64, (1, LANES // HEAD_DIM)), jnp.tile(s64, (1, LANES // HEAD_DIM))
    c32, s32 = _rope_tables(seq, IDX_DIM)
    wscale = jnp.full((seq, IDX_HEADS), IDX_HEADS ** -0.5 * IDX_DIM ** -0.5, F32)
    gap = jnp.zeros((seq, LANES - IDX_HEADS), F32)
    icos = jnp.concatenate([jnp.tile(c32, (1, IDX_HEADS)), wscale, gap, jnp.tile(c32, (1, slots))], axis=1)
    isin = jnp.concatenate([jnp.tile(s32, (1, IDX_HEADS)), jnp.zeros((seq, LANES), F32),
                            jnp.tile(s32, (1, slots))], axis=1)

    x2 = x.reshape(n, D_MODEL)
    pool, q, k, vt, idx, kcat = _inproj(x2, norm1_g[None, :], wmain, w_idx.astype(BF16), wvt, qg, kg,
                                        qcos, qsin, icos, isin, seq)
    b_out = _dsa(q, idx, kcat, k, vt, batch, seq)
    return _mixffn(x, pool.reshape(batch, seq, POOL_WIDTH), b_out, pool_w.astype(BF16), pool_scale[None, :],
                   w_out[:POOL_WIDTH].astype(BF16), w_out[POOL_WIDTH:].astype(BF16), norm2_g[None, :],
                   w_up.astype(BF16), conv_w, conv_b[None, :], w_down.astype(BF16))


def kernel(x, norm1_g, w_in, q_norm_g, k_norm_g, pool_w, pool_scale, w_out, norm2_g, w_up, conv_w, conv_b, w_down):
    for l in range(norm1_g.shape[0]):
        x = _layer(x, norm1_g[l], w_in[l], q_norm_g[l], k_norm_g[l], pool_w[l], pool_scale[l], w_out[l],
                   norm2_g[l], w_up[l], conv_w[l], conv_b[l], w_down[l])
    return x
```

```python
import functools

import jax
import jax.numpy as jnp
from jax import lax
from jax.experimental import pallas as pl
from jax.experimental.pallas import tpu as pltpu

F32 = jnp.float32
BF16 = jnp.bfloat16

D_MODEL = 1024
POOL_WIDTH = 512
POOL_WINDOWS = (2, 4, 8, 16)
POOL_GROUP = 128
HEAD_DIM = 64
N_Q_HEADS = 8
N_KV_HEADS = 2
Q_PER_KV = 4
IDX_HEADS = 8
IDX_DIM = 32
TOPK_MAX = 256
ROPE_THETA = 10000.0
D_FF = 2816
CONV_WIDTH = 3
EPS = 1e-6
NEG = -1e30

LANES = 128
SUBLANES = 8
QPAD = N_Q_HEADS * LANES
IDXW = 3 * LANES
IDX_PROJ_W = IDXW + LANES
MAIN_W = POOL_WIDTH + (N_Q_HEADS + 2 * N_KV_HEADS) * HEAD_DIM

INPROJ_ROWS = 512
DSA_ROWS = 256
FFN_ROWS = 512
MIX_HALO = 16
POOL_BACK = 32

Q_SCALE = HEAD_DIM ** -0.5 * 1.4426950408889634

NT_DIMS = (((1,), (1,)), ((), ()))


def _dot(a, b):
    return jnp.dot(a, b, preferred_element_type=F32)


def _dot_nt(a, b):
    return lax.dot_general(a, b, NT_DIMS, preferred_element_type=F32)


def _swap_halves(xc, half):
    lane = lax.broadcasted_iota(jnp.int32, xc.shape, 1)
    first = (lane % (2 * half)) < half
    return jnp.where(first, pltpu.roll(xc, LANES - half, 1), pltpu.roll(xc, half, 1))


def _inproj_kernel(x_ref, g1_ref, wmain_ref, widx_ref, wvt_ref, qg_ref, kg_ref,
                   qcos_ref, qsin_ref, icos_ref, isin_ref,
                   pool_ref, q_ref, k_ref, vt_ref, idx_ref, kcat_ref):
    x = x_ref[...]
    ms = jnp.mean(x * x, axis=-1, keepdims=True)
    h = x * lax.rsqrt(ms + EPS) * g1_ref[...]
    h_hi = h.astype(BF16)

    qcos = qcos_ref[...]
    qsin = qsin_ref[...]
    lane = lax.broadcasted_iota(jnp.int32, (x.shape[0], LANES), 1)
    first = lane < HEAD_DIM

    def norm_rope(xc, gain):
        sq = xc * xc
        s0 = jnp.sum(jnp.where(first, sq, 0.0), axis=-1, keepdims=True)
        s1 = jnp.sum(jnp.where(first, 0.0, sq), axis=-1, keepdims=True)
        ss = jnp.where(first, s0, s1) * (1.0 / HEAD_DIM)
        xn = xc * lax.rsqrt(ss + EPS) * gain
        return xn * qcos + _swap_halves(xn, HEAD_DIM // 2) * qsin

    n_qcols = N_Q_HEADS * HEAD_DIM // LANES
    qk = _dot(h_hi, wmain_ref[:, POOL_WIDTH:])
    idxp = _dot(h_hi, widx_ref[...])

    for c in range(n_qcols):
        qr = norm_rope(qk[:, c * LANES:(c + 1) * LANES], qg_ref[...]) * Q_SCALE
        qs = pltpu.roll(qr, HEAD_DIM, 1)
        if (2 * c) // Q_PER_KV == 0:
            pair = (jnp.where(first, qr, 0.0), jnp.where(first, qs, 0.0))
        else:
            pair = (jnp.where(first, 0.0, qs), jnp.where(first, 0.0, qr))
        for i, qh in enumerate(pair):
            q_ref[:, (2 * c + i) * LANES:(2 * c + i + 1) * LANES] = qh.astype(BF16)
    k_ref[...] = norm_rope(qk[:, n_qcols * LANES:], kg_ref[...]).astype(BF16)

    for c in range(IDX_PROJ_W // LANES):
        sl = slice(c * LANES, (c + 1) * LANES)
        ic = idxp[:, sl]
        roped = ic * icos_ref[:, sl] + _swap_halves(ic, IDX_DIM // 2) * isin_ref[:, sl]
        if c < IDXW // LANES:
            idx_ref[:, sl] = roped
        else:
            kcat_ref[...] = roped.astype(BF16)

    pool_ref[...] = _dot(h_hi, wmain_ref[:, :POOL_WIDTH])
    vt = _dot_nt(wvt_ref[...], h_hi).astype(BF16)
    for i in range(vt_ref.shape[0]):
        vt_ref[i] = vt[:, i * DSA_ROWS:(i + 1) * DSA_ROWS]


def _inproj(x2, g1, wmain, widx, wvt, qg, kg, qcos, qsin, icos, isin, seq):
    n = x2.shape[0]
    rows = INPROJ_ROWS
    tper = seq // rows
    vblk = rows // DSA_ROWS
    row = lambda w: pl.BlockSpec((rows, w), lambda i: (i, 0))
    const = lambda a: pl.BlockSpec(a.shape, lambda i: (0, 0))
    table = lambda w: pl.BlockSpec((rows, w), lambda i: (i % tper, 0))
    return pl.pallas_call(
        _inproj_kernel,
        grid=(n // rows,),
        in_specs=[row(D_MODEL), const(g1), const(wmain), const(widx), const(wvt), const(qg), const(kg),
                  table(LANES), table(LANES), table(IDX_PROJ_W), table(IDX_PROJ_W)],
        out_specs=[row(POOL_WIDTH), row(QPAD), row(LANES),
                   pl.BlockSpec((vblk, LANES, DSA_ROWS), lambda i: (i, 0, 0)), row(IDXW), row(LANES)],
        out_shape=[jax.ShapeDtypeStruct((n, POOL_WIDTH), F32),
                   jax.ShapeDtypeStruct((n, QPAD), BF16),
                   jax.ShapeDtypeStruct((n, LANES), BF16),
                   jax.ShapeDtypeStruct((n // DSA_ROWS, LANES, DSA_ROWS), BF16),
                   jax.ShapeDtypeStruct((n, IDXW), F32),
                   jax.ShapeDtypeStruct((n, LANES), BF16)],
        compiler_params=pltpu.CompilerParams(dimension_semantics=("arbitrary",),
                                             vmem_limit_bytes=48 * 1024 * 1024),
        name="inproj",
    )(x2, g1, wmain, widx, wvt, qg, kg, qcos, qsin, icos, isin)


def _key_to_float(key):
    bits = key ^ ((key >> 31) & jnp.int32(0x7FFFFFFF))
    return lax.bitcast_convert_type(bits, F32)


def _fold_groups(x, op, ways=4):
    n = x.shape[0] // SUBLANES
    accs = [x[g * SUBLANES:(g + 1) * SUBLANES, :] for g in range(ways)]
    for g in range(ways, n):
        accs[g % ways] = op(accs[g % ways], x[g * SUBLANES:(g + 1) * SUBLANES, :])
    while len(accs) > 1:
        accs = [op(accs[i], accs[i + 1]) for i in range(0, len(accs), 2)]
    return accs[0]


def _fold_sublanes(x, op):
    for shift in (4, 2, 1):
        x = op(x, pltpu.roll(x, shift, 0))
    return x


def _chunk_loop(n, body, init, unroll=4):
    def run(start, width, carry):
        for w in range(width):
            carry = body(start + w, carry)
        return carry

    carry = lax.fori_loop(0, n // unroll, lambda i, c: run(unroll * i, unroll, c), init)
    done = (n // unroll) * unroll
    width = unroll // 2
    while width:
        carry = lax.cond((n & width) != 0, functools.partial(run, done, width), lambda c: c, carry)
        done = done + (n & width)
        width //= 2
    return carry


def _dsa_kernel(q_ref, idxq_ref, kcat_ref, k_ref, vt_ref, out_ref, sc_ref, qcat_ref, acc_ref, p_ref, lg_ref,
                *, seq, topk):
    rows = DSA_ROWS
    ck = DSA_ROWS
    ngrp = ck // SUBLANES
    qb = pl.program_id(1)
    nch = qb + 1
    lane = lax.broadcasted_iota(jnp.int32, (rows, LANES), 1)
    qidx = qb * rows + lax.broadcasted_iota(jnp.int32, (ck, rows), 1)
    krow = lax.broadcasted_iota(jnp.int32, (ck, rows), 0)

    slots = LANES // IDX_DIM
    for c in range(IDX_HEADS // slots):
        col = idxq_ref[:, c * LANES:(c + 1) * LANES]
        for r in range(slots):
            live = jnp.logical_and(lane >= r * IDX_DIM, lane < (r + 1) * IDX_DIM)
            qcat_ref[c * slots + r] = jnp.where(live, col, 0.0).astype(BF16)
    wt = idxq_ref[:, 2 * LANES:].T[0:IDX_HEADS, :]

    def score_body(j, nonneg):
        kc = kcat_ref[pl.ds(pl.multiple_of(j * ck, ck), ck), :]
        acc = jnp.zeros((ck, rows), F32)
        for hd in range(IDX_HEADS):
            y = _dot_nt(kc, qcat_ref[hd])
            acc = acc + wt[hd:hd + 1, :] * jnp.maximum(y, 0.0)
        sc = jnp.where(j * ck + krow <= qidx, acc, NEG)
        sc_ref[j] = sc
        return nonneg + _fold_groups(jnp.where(sc >= 0.0, 1.0, 0.0), jnp.add)

    nonneg = _chunk_loop(nch, score_body, jnp.zeros((SUBLANES, rows), F32))

    n_virtual = (seq - nch * ck).astype(F32)

    def count(pred, thr):
        def body(j, accs):
            accs = list(accs)
            for g in range(ngrp):
                hit = pred(sc_ref[j, g * SUBLANES:(g + 1) * SUBLANES, :], thr)
                accs[g % len(accs)] = accs[g % len(accs)] + jnp.where(hit, 1.0, 0.0)
            return tuple(accs)
        zero = jnp.zeros((SUBLANES, rows), F32)
        accs = _chunk_loop(nch, body, (zero, zero, zero, zero))
        tot = _fold_sublanes((accs[0] + accs[1]) + (accs[2] + accs[3]), jnp.add)
        return tot + jnp.where(pred(jnp.float32(NEG), thr), n_virtual, 0.0)

    ge = lambda a, b: a >= b
    gt = lambda a, b: a > b
    kf = jnp.float32(topk)

    nonneg = _fold_sublanes(nonneg, jnp.add)
    positive = nonneg >= kf
    key = jnp.where(positive, jnp.int32(0), jnp.int32(-2 ** 31))
    n_ge = jnp.where(positive, nonneg, jnp.float32(seq))

    def bit_body(i, carry):
        key, n_ge = carry
        cand = key | jnp.left_shift(jnp.int32(1), 30 - i)
        cnt = count(ge, _key_to_float(cand))
        return jnp.where(cnt >= kf, cand, key), jnp.where(cnt >= kf, cnt, n_ge)

    key, n_ge = lax.fori_loop(0, 31, bit_body, (key, n_ge))
    thr8 = _key_to_float(key)
    thr = thr8[0:1, :]

    def logits(j, ms):
        kc = k_ref[pl.ds(pl.multiple_of(j * ck, ck), ck), :]
        out = []
        for hd in range(N_Q_HEADS):
            s = _dot_nt(kc, q_ref[:, hd * LANES:(hd + 1) * LANES]) + sc_ref[j]
            lg_ref[hd, j] = s
            out.append(jnp.maximum(ms[hd], _fold_groups(s, jnp.maximum)))
        return tuple(out)

    ms0 = tuple(jnp.full((SUBLANES, rows), -jnp.inf, F32) for _ in range(N_Q_HEADS))

    def select_exact():
        def body(j, ms):
            sc_ref[j] = jnp.where(sc_ref[j] >= thr, 0.0, NEG)
            return logits(j, ms)
        return _chunk_loop(nch, body, ms0)

    def select_ties():
        need = (kf - count(gt, thr8))[0:1, :]
        ki = lax.broadcasted_iota(jnp.int32, (ck, ck), 0)
        kj = lax.broadcasted_iota(jnp.int32, (ck, ck), 1)
        lower = jnp.where(kj < ki, 1.0, 0.0).astype(BF16)
        ones8 = jnp.ones((SUBLANES, ck), BF16)

        def body(j, carry):
            run, ms = carry
            sc = sc_ref[j]
            eq = sc == thr
            eqb = jnp.where(eq, 1.0, 0.0).astype(BF16)
            prior = run[0:1, :] + _dot(lower, eqb)
            tie = jnp.where(prior < need, 0.0, NEG)
            bias = jnp.where(sc > thr, 0.0, jnp.where(eq, tie, NEG))
            sc_ref[j] = jnp.where(j * ck + krow <= qidx, bias, NEG)
            return run + _dot(ones8, eqb), logits(j, ms)
        return _chunk_loop(nch, body, (jnp.zeros((SUBLANES, rows), F32), ms0))[1]

    all_exact = jnp.min(jnp.where(n_ge == kf, 1.0, 0.0)) > 0.5
    ms = lax.cond(all_exact, select_exact, select_ties)
    ms = [_fold_sublanes(m, jnp.maximum)[0:1, :] for m in ms]

    acc_ref[...] = jnp.zeros(acc_ref.shape, F32)

    def pv_body(j, ls):
        out = []
        for hd in range(N_Q_HEADS):
            p = jnp.exp2(lg_ref[hd, j] - ms[hd])
            out.append(ls[hd] + _fold_groups(p, jnp.add))
            p_ref[hd] = p.astype(BF16)
        vt = vt_ref[j]
        for hd in range(N_Q_HEADS):
            acc_ref[hd] += _dot(vt, p_ref[hd])
        return tuple(out)

    ls = _chunk_loop(nch, pv_body, tuple(jnp.zeros((SUBLANES, rows), F32) for _ in range(N_Q_HEADS)))

    outs = []
    for hd in range(N_Q_HEADS):
        l = _fold_sublanes(ls[hd], jnp.add)
        r0 = HEAD_DIM * (hd // Q_PER_KV)
        outs.append(acc_ref[hd, r0:r0 + HEAD_DIM, :] / l[0:1, :])
    out_ref[...] = jnp.concatenate(outs, axis=0).T.astype(BF16)


def _dsa(q, idx, kcat, k, vt, batch, seq):
    rows = DSA_ROWS
    nq = seq // rows
    topk = min(TOPK_MAX, seq // 4)
    assert topk <= rows and seq % rows == 0
    q3 = q.reshape(batch, seq, QPAD)
    idx3 = idx.reshape(batch, seq, IDXW)
    vt = vt.reshape(batch, nq, LANES, rows)
    full = pl.BlockSpec((None, seq, LANES), lambda b, i: (b, 0, 0))
    kern = functools.partial(_dsa_kernel, seq=seq, topk=topk)
    return pl.pallas_call(
        kern,
        grid=(batch, nq),
        in_specs=[pl.BlockSpec((None, rows, QPAD), lambda b, i: (b, i, 0)),
                  pl.BlockSpec((None, rows, IDXW), lambda b, i: (b, i, 0)),
                  full, full,
                  pl.BlockSpec((None, nq, LANES, rows), lambda b, i: (b, 0, 0, 0))],
        out_specs=pl.BlockSpec((None, rows, N_Q_HEADS * HEAD_DIM), lambda b, i: (b, i, 0)),
        out_shape=jax.ShapeDtypeStruct((batch, seq, N_Q_HEADS * HEAD_DIM), BF16),
        scratch_shapes=[pltpu.VMEM((nq, rows, rows), F32),
                        pltpu.VMEM((IDX_HEADS, rows, LANES), BF16),
                        pltpu.VMEM((N_Q_HEADS, LANES, rows), F32),
                        pltpu.VMEM((N_Q_HEADS, rows, rows), BF16),
                        pltpu.VMEM((N_Q_HEADS, nq, rows, rows), F32)],
        compiler_params=pltpu.CompilerParams(dimension_semantics=("arbitrary", "arbitrary"),
                                             vmem_limit_bytes=48 * 1024 * 1024),
        name="dsa",
    )(q3, idx3, kcat.reshape(batch, seq, LANES), k.reshape(batch, seq, LANES), vt)


def _mixffn_kernel(x_ref, xp_ref, pool_ref, pp_ref, b_ref, bp_ref, pw_ref, ps_ref, woa_ref, wob_ref,
                   g2_ref, wup_ref, cw_ref, cb_ref, wdn_ref, out_ref, act_ref):
    rows = FFN_ROWS
    ext = MIX_HALO + rows
    ti = pl.program_id(1)
    row = lax.broadcasted_iota(jnp.int32, (ext, 1), 0)
    t = ti * rows - MIX_HALO + row

    pool_ext = jnp.concatenate([jnp.where(ti == 0, 0.0, pp_ref[...]), pool_ref[...]], axis=0)
    lead = POOL_BACK - MIX_HALO
    parts = []
    for g, w in enumerate(POOL_WINDOWS):
        sl = slice(g * POOL_GROUP, (g + 1) * POOL_GROUP)
        a = pool_ext[:, sl]
        span = 1
        while span < w:
            a = a + pltpu.roll(a, span, 0)
            span *= 2
        cnt = jnp.clip(t + 1, 1, w).astype(F32)
        p = a[lead:] / cnt - pool_ext[lead:, sl]
        parts.append(_dot(p.astype(BF16), pw_ref[g]) * ps_ref[:, sl])
    a_out = jnp.concatenate(parts, axis=1).astype(BF16)

    x_ext = jnp.concatenate([xp_ref[...], x_ref[...]], axis=0)
    b_ext = jnp.concatenate([bp_ref[...], b_ref[...]], axis=0)
    x1 = x_ext + _dot(a_out, woa_ref[...]) + _dot(b_ext, wob_ref[...])

    h = x1 * lax.rsqrt(jnp.mean(x1 * x1, axis=-1, keepdims=True) + EPS) * g2_ref[...]
    h_ext = jnp.where(jnp.logical_and(ti == 0, row < MIX_HALO), 0.0, h).astype(BF16)

    chunk = 2 * LANES
    for c in range(D_FF // chunk):
        halves = []
        for off in (0, D_FF):
            sl = slice(off + c * chunk, off + (c + 1) * chunk)
            u = _dot(h_ext, wup_ref[:, sl])
            u1 = pltpu.roll(u, 1, 0)
            u2 = pltpu.roll(u, 2, 0)
            cv = cb_ref[:, sl] + u2[MIX_HALO:] * cw_ref[0:1, sl]
            cv = cv + u1[MIX_HALO:] * cw_ref[1:2, sl]
            cv = cv + u[MIX_HALO:] * cw_ref[2:3, sl]
            halves.append(cv)
        gate, val = halves
        act = gate * (1.0 / (1.0 + jnp.exp(-gate))) * val
        act_ref[:, c * chunk:(c + 1) * chunk] = act.astype(BF16)
    out_ref[...] = x1[MIX_HALO:] + _dot(act_ref[...], wdn_ref[...])


def _mixffn(x3, pool3, b_out, pw, ps, woa, wob, g2, wup, cw, cb, wdn):
    batch, seq, _ = x3.shape
    rows = FFN_ROWS
    tile = lambda w: pl.BlockSpec((None, rows, w), lambda b, i: (b, i, 0))
    back = lambda h, w: pl.BlockSpec((None, h, w), lambda b, i: (b, jnp.maximum(i * (rows // h) - 1, 0), 0))
    const = lambda a: pl.BlockSpec(a.shape, lambda b, i: (0,) * a.ndim, pipeline_mode=pl.Buffered(1))
    return pl.pallas_call(
        _mixffn_kernel,
        grid=(batch, seq // rows),
        in_specs=[tile(D_MODEL), back(MIX_HALO, D_MODEL), tile(POOL_WIDTH), back(POOL_BACK, POOL_WIDTH),
                  tile(POOL_WIDTH), back(MIX_HALO, POOL_WIDTH),
                  const(pw), const(ps), const(woa), const(wob), const(g2), const(wup), const(cw), const(cb),
                  const(wdn)],
        out_specs=tile(D_MODEL),
        out_shape=jax.ShapeDtypeStruct((batch, seq, D_MODEL), F32),
        scratch_shapes=[pltpu.VMEM((rows, D_FF), BF16)],
        compiler_params=pltpu.CompilerParams(dimension_semantics=("arbitrary", "arbitrary"),
                                             vmem_limit_bytes=56 * 1024 * 1024),
        name="mixffn",
    )(x3, x3, pool3, pool3, b_out, b_out, pw, ps, woa, wob, g2, wup, cw, cb, wdn)


def _rope_tables(seq, dim):
    half = dim // 2
    inv = jnp.exp(-jnp.log(jnp.float32(ROPE_THETA)) * jnp.arange(half, dtype=F32) / half)
    ang = jnp.arange(seq).astype(F32)[:, None] * inv[None, :]
    cos, sin = jnp.cos(ang), jnp.sin(ang)
    return jnp.concatenate([cos, cos], axis=1), jnp.concatenate([-sin, sin], axis=1)


def _layer(x, norm1_g, w_in, q_norm_g, k_norm_g, pool_w, pool_scale, w_out, norm2_g, w_up, conv_w, conv_b, w_down):
    batch, seq, _ = x.shape
    n = batch * seq
    v0 = MAIN_W - N_KV_HEADS * HEAD_DIM
    wmain = w_in[:, :v0].astype(BF16)
    wvt = w_in[:, v0:MAIN_W].T.astype(BF16)
    w_qi, w_ki, w_wi = jnp.split(w_in[:, MAIN_W:], [IDX_HEADS * IDX_DIM, (IDX_HEADS + 1) * IDX_DIM], axis=1)
    qg = jnp.concatenate([q_norm_g, q_norm_g])[None, :]
    kg = jnp.concatenate([k_norm_g, k_norm_g])[None, :]
    slots = LANES // IDX_DIM
    w_idx = jnp.concatenate([w_qi, w_wi, jnp.zeros((D_MODEL, LANES - IDX_HEADS), F32)] + [w_ki] * slots, axis=1)

    c64, s64 = _rope_tables(seq, HEAD_DIM)
    qcos, qsin = jnp.tile(c64, (1, LANES // HEAD_DIM)), jnp.tile(s64, (1, LANES // HEAD_DIM))
    c32, s32 = _rope_tables(seq, IDX_DIM)
    wscale = jnp.full((seq, IDX_HEADS), IDX_HEADS ** -0.5 * IDX_DIM ** -0.5, F32)
    gap = jnp.zeros((seq, LANES - IDX_HEADS), F32)
    icos = jnp.concatenate([jnp.tile(c32, (1, IDX_HEADS)), wscale, gap, jnp.tile(c32, (1, slots))], axis=1)
    isin = jnp.concatenate([jnp.tile(s32, (1, IDX_HEADS)), jnp.zeros((seq, LANES), F32),
                            jnp.tile(s32, (1, slots))], axis=1)

    x2 = x.reshape(n, D_MODEL)
    pool, q, k, vt, idx, kcat = _inproj(x2, norm1_g[None, :], wmain, w_idx.astype(BF16), wvt, qg, kg,
                                        qcos, qsin, icos, isin, seq)
    b_out = _dsa(q, idx, kcat, k, vt, batch, seq)
    return _mixffn(x, pool.reshape(batch, seq, POOL_WIDTH), b_out, pool_w.astype(BF16), pool_scale[None, :],
                   w_out[:POOL_WIDTH].astype(BF16), w_out[POOL_WIDTH:].astype(BF16), norm2_g[None, :],
                   w_up.astype(BF16), conv_w, conv_b[None, :], w_down.astype(BF16))


def kernel(x, norm1_g, w_in, q_norm_g, k_norm_g, pool_w, pool_scale, w_out, norm2_g, w_up, conv_w, conv_b, w_down):
    for l in range(norm1_g.shape[0]):
        x = _layer(x, norm1_g[l], w_in[l], q_norm_g[l], k_norm_g[l], pool_w[l], pool_scale[l], w_out[l],
                   norm2_g[l], w_up[l], conv_w[l], conv_b[l], w_down[l])
    return x
```

```python
import functools
import math

import jax
import jax.numpy as jnp
from jax import lax
from jax.experimental import pallas as pl
from jax.experimental.pallas import tpu as pltpu

F32 = jnp.float32
BF16 = jnp.bfloat16

D_MODEL = 1024
POOL_WIDTH = 512
POOL_WINDOWS = (2, 4, 8, 16)
POOL_GROUP = 128
HEAD_DIM = 64
N_Q_HEADS = 8
N_KV_HEADS = 2
Q_PER_KV = 4
IDX_HEADS = 8
IDX_DIM = 32
TOPK_MAX = 256
ROPE_THETA = 10000.0
D_FF = 2816
CONV_WIDTH = 3
EPS = 1e-6
NEG = -1e30

LANES = 128
SUBLANES = 8
QPAD = N_Q_HEADS * LANES
IDXW = 3 * LANES
IDX_PROJ_W = IDXW + LANES
MAIN_W = POOL_WIDTH + (N_Q_HEADS + 2 * N_KV_HEADS) * HEAD_DIM

INPROJ_ROWS = 512
DSA_ROWS = 256
FFN_ROWS = 512
V7X_VMEM_BYTES = 64 * 1024 * 1024
VMEM_LIMIT_BYTES = 3 * V7X_VMEM_BYTES // 4
VMEM_LIMIT_FFN_BYTES = 7 * V7X_VMEM_BYTES // 8
MIX_HALO = 16
POOL_BACK = 32

Q_SCALE = HEAD_DIM ** -0.5 * math.log2(math.e)

NT_DIMS = (((1,), (1,)), ((), ()))


def _dot(a, b):
    return jnp.dot(a, b, preferred_element_type=F32)


def _dot_nt(a, b):
    return lax.dot_general(a, b, NT_DIMS, preferred_element_type=F32)


def _swap_halves(xc, half):
    lane = lax.broadcasted_iota(jnp.int32, xc.shape, 1)
    first = (lane % (2 * half)) < half
    return jnp.where(first, pltpu.roll(xc, LANES - half, 1), pltpu.roll(xc, half, 1))


def _inproj_kernel(x_ref, g1_ref, wmain_ref, widx_ref, wvt_ref, qg_ref, kg_ref,
                   qcos_ref, qsin_ref, icos_ref, isin_ref,
                   pool_ref, q_ref, k_ref, vt_ref, idx_ref, kcat_ref):
    x = x_ref[...]
    ms = jnp.mean(x * x, axis=-1, keepdims=True)
    h = x * lax.rsqrt(ms + EPS) * g1_ref[...]
    h_hi = h.astype(BF16)

    qcos = qcos_ref[...]
    qsin = qsin_ref[...]
    lane = lax.broadcasted_iota(jnp.int32, (x.shape[0], LANES), 1)
    first = lane < HEAD_DIM

    def norm_rope(xc, gain):
        sq = xc * xc
        s0 = jnp.sum(jnp.where(first, sq, 0.0), axis=-1, keepdims=True)
        s1 = jnp.sum(jnp.where(first, 0.0, sq), axis=-1, keepdims=True)
        ss = jnp.where(first, s0, s1) * (1.0 / HEAD_DIM)
        xn = xc * lax.rsqrt(ss + EPS) * gain
        return xn * qcos + _swap_halves(xn, HEAD_DIM // 2) * qsin

    n_qcols = N_Q_HEADS * HEAD_DIM // LANES
    qk = _dot(h_hi, wmain_ref[:, POOL_WIDTH:])
    idxp = _dot(h_hi, widx_ref[...])

    for c in range(n_qcols):
        qr = norm_rope(qk[:, c * LANES:(c + 1) * LANES], qg_ref[...]) * Q_SCALE
        qs = pltpu.roll(qr, HEAD_DIM, 1)
        if (2 * c) // Q_PER_KV == 0:
            pair = (jnp.where(first, qr, 0.0), jnp.where(first, qs, 0.0))
        else:
            pair = (jnp.where(first, 0.0, qs), jnp.where(first, 0.0, qr))
        for i, qh in enumerate(pair):
            q_ref[:, (2 * c + i) * LANES:(2 * c + i + 1) * LANES] = qh.astype(BF16)
    k_ref[...] = norm_rope(qk[:, n_qcols * LANES:], kg_ref[...]).astype(BF16)

    for c in range(IDX_PROJ_W // LANES):
        sl = slice(c * LANES, (c + 1) * LANES)
        ic = idxp[:, sl]
        roped = ic * icos_ref[:, sl] + _swap_halves(ic, IDX_DIM // 2) * isin_ref[:, sl]
        if c < IDXW // LANES:
            idx_ref[:, sl] = roped
        else:
            kcat_ref[...] = roped.astype(BF16)

    pool_ref[...] = _dot(h_hi, wmain_ref[:, :POOL_WIDTH])
    vt = _dot_nt(wvt_ref[...], h_hi).astype(BF16)
    for i in range(vt_ref.shape[0]):
        vt_ref[i] = vt[:, i * DSA_ROWS:(i + 1) * DSA_ROWS]


def _inproj(x2, g1, wmain, widx, wvt, qg, kg, qcos, qsin, icos, isin, seq):
    n = x2.shape[0]
    rows = INPROJ_ROWS
    tper = seq // rows
    vblk = rows // DSA_ROWS
    row = lambda w: pl.BlockSpec((rows, w), lambda i: (i, 0))
    const = lambda a: pl.BlockSpec(a.shape, lambda i: (0, 0))
    table = lambda w: pl.BlockSpec((rows, w), lambda i: (i % tper, 0))
    return pl.pallas_call(
        _inproj_kernel,
        grid=(n // rows,),
        in_specs=[row(D_MODEL), const(g1), const(wmain), const(widx), const(wvt), const(qg), const(kg),
                  table(LANES), table(LANES), table(IDX_PROJ_W), table(IDX_PROJ_W)],
        out_specs=[row(POOL_WIDTH), row(QPAD), row(LANES),
                   pl.BlockSpec((vblk, LANES, DSA_ROWS), lambda i: (i, 0, 0)), row(IDXW), row(LANES)],
        out_shape=[jax.ShapeDtypeStruct((n, POOL_WIDTH), F32),
                   jax.ShapeDtypeStruct((n, QPAD), BF16),
                   jax.ShapeDtypeStruct((n, LANES), BF16),
                   jax.ShapeDtypeStruct((n // DSA_ROWS, LANES, DSA_ROWS), BF16),
                   jax.ShapeDtypeStruct((n, IDXW), F32),
                   jax.ShapeDtypeStruct((n, LANES), BF16)],
        compiler_params=pltpu.CompilerParams(dimension_semantics=("arbitrary",),
                                             vmem_limit_bytes=VMEM_LIMIT_BYTES),
        name="inproj",
    )(x2, g1, wmain, widx, wvt, qg, kg, qcos, qsin, icos, isin)


def _key_to_float(key):
    bits = key ^ ((key >> 31) & jnp.int32(0x7FFFFFFF))
    return lax.bitcast_convert_type(bits, F32)


def _fold_groups(x, op, ways=4):
    n = x.shape[0] // SUBLANES
    accs = [x[g * SUBLANES:(g + 1) * SUBLANES, :] for g in range(ways)]
    for g in range(ways, n):
        accs[g % ways] = op(accs[g % ways], x[g * SUBLANES:(g + 1) * SUBLANES, :])
    while len(accs) > 1:
        accs = [op(accs[i], accs[i + 1]) for i in range(0, len(accs), 2)]
    return accs[0]


def _fold_sublanes(x, op):
    for shift in (4, 2, 1):
        x = op(x, pltpu.roll(x, shift, 0))
    return x


def _chunk_loop(n, body, init, unroll=4):
    def run(start, width, carry):
        for w in range(width):
            carry = body(start + w, carry)
        return carry

    carry = lax.fori_loop(0, n // unroll, lambda i, c: run(unroll * i, unroll, c), init)
    done = (n // unroll) * unroll
    width = unroll // 2
    while width:
        carry = lax.cond((n & width) != 0, functools.partial(run, done, width), lambda c: c, carry)
        done = done + (n & width)
        width //= 2
    return carry


def _dsa_kernel(q_ref, idxq_ref, kcat_ref, k_ref, vt_ref, out_ref, sc_ref, qcat_ref, acc_ref, p_ref, lg_ref,
                *, seq, topk):
    rows = DSA_ROWS
    ck = DSA_ROWS
    ngrp = ck // SUBLANES
    qb = pl.program_id(1)
    nch = qb + 1
    lane = lax.broadcasted_iota(jnp.int32, (rows, LANES), 1)
    qidx = qb * rows + lax.broadcasted_iota(jnp.int32, (ck, rows), 1)
    krow = lax.broadcasted_iota(jnp.int32, (ck, rows), 0)

    slots = LANES // IDX_DIM
    for c in range(IDX_HEADS // slots):
        col = idxq_ref[:, c * LANES:(c + 1) * LANES]
        for r in range(slots):
            live = jnp.logical_and(lane >= r * IDX_DIM, lane < (r + 1) * IDX_DIM)
            qcat_ref[c * slots + r] = jnp.where(live, col, 0.0).astype(BF16)
    wt = idxq_ref[:, 2 * LANES:].T[0:IDX_HEADS, :]

    def score_body(j, nonneg):
        kc = kcat_ref[pl.ds(pl.multiple_of(j * ck, ck), ck), :]
        acc = jnp.zeros((ck, rows), F32)
        for hd in range(IDX_HEADS):
            y = _dot_nt(kc, qcat_ref[hd])
            acc = acc + wt[hd:hd + 1, :] * jnp.maximum(y, 0.0)
        sc = jnp.where(j * ck + krow <= qidx, acc, NEG)
        sc_ref[j] = sc
        return nonneg + _fold_groups(jnp.where(sc >= 0.0, 1.0, 0.0), jnp.add)

    nonneg = _chunk_loop(nch, score_body, jnp.zeros((SUBLANES, rows), F32))

    n_virtual = (seq - nch * ck).astype(F32)

    def count(pred, thr):
        def body(j, accs):
            accs = list(accs)
            for g in range(ngrp):
                hit = pred(sc_ref[j, g * SUBLANES:(g + 1) * SUBLANES, :], thr)
                accs[g % len(accs)] = accs[g % len(accs)] + jnp.where(hit, 1.0, 0.0)
            return tuple(accs)
        zero = jnp.zeros((SUBLANES, rows), F32)
        accs = _chunk_loop(nch, body, (zero, zero, zero, zero))
        tot = _fold_sublanes((accs[0] + accs[1]) + (accs[2] + accs[3]), jnp.add)
        return tot + jnp.where(pred(jnp.float32(NEG), thr), n_virtual, 0.0)

    ge = lambda a, b: a >= b
    gt = lambda a, b: a > b
    kf = jnp.float32(topk)

    nonneg = _fold_sublanes(nonneg, jnp.add)
    positive = nonneg >= kf
    key = jnp.where(positive, jnp.int32(0), jnp.int32(-2 ** 31))
    n_ge = jnp.where(positive, nonneg, jnp.float32(seq))

    def bit_body(i, carry):
        key, n_ge = carry
        cand = key | jnp.left_shift(jnp.int32(1), 30 - i)
        cnt = count(ge, _key_to_float(cand))
        return jnp.where(cnt >= kf, cand, key), jnp.where(cnt >= kf, cnt, n_ge)

    key, n_ge = lax.fori_loop(0, 31, bit_body, (key, n_ge))
    thr8 = _key_to_float(key)
    thr = thr8[0:1, :]

    def logits(j, ms):
        kc = k_ref[pl.ds(pl.multiple_of(j * ck, ck), ck), :]
        out = []
        for hd in range(N_Q_HEADS):
            s = _dot_nt(kc, q_ref[:, hd * LANES:(hd + 1) * LANES]) + sc_ref[j]
            lg_ref[hd, j] = s
            out.append(jnp.maximum(ms[hd], _fold_groups(s, jnp.maximum)))
        return tuple(out)

    ms0 = tuple(jnp.full((SUBLANES, rows), -jnp.inf, F32) for _ in range(N_Q_HEADS))

    def select_exact():
        def body(j, ms):
            sc_ref[j] = jnp.where(sc_ref[j] >= thr, 0.0, NEG)
            return logits(j, ms)
        return _chunk_loop(nch, body, ms0)

    def select_ties():
        need = (kf - count(gt, thr8))[0:1, :]
        ki = lax.broadcasted_iota(jnp.int32, (ck, ck), 0)
        kj = lax.broadcasted_iota(jnp.int32, (ck, ck), 1)
        lower = jnp.where(kj < ki, 1.0, 0.0).astype(BF16)
        ones8 = jnp.ones((SUBLANES, ck), BF16)

        def body(j, carry):
            run, ms = carry
            sc = sc_ref[j]
            eq = sc == thr
            eqb = jnp.where(eq, 1.0, 0.0).astype(BF16)
            prior = run[0:1, :] + _dot(lower, eqb)
            tie = jnp.where(prior < need, 0.0, NEG)
            bias = jnp.where(sc > thr, 0.0, jnp.where(eq, tie, NEG))
            sc_ref[j] = jnp.where(j * ck + krow <= qidx, bias, NEG)
            return run + _dot(ones8, eqb), logits(j, ms)
        return _chunk_loop(nch, body, (jnp.zeros((SUBLANES, rows), F32), ms0))[1]

    all_exact = jnp.min(jnp.where(n_ge == kf, 1.0, 0.0)) > 0.5
    ms = lax.cond(all_exact, select_exact, select_ties)
    ms = [_fold_sublanes(m, jnp.maximum)[0:1, :] for m in ms]

    acc_ref[...] = jnp.zeros(acc_ref.shape, F32)

    def pv_body(j, ls):
        out = []
        for hd in range(N_Q_HEADS):
            p = jnp.exp2(lg_ref[hd, j] - ms[hd])
            out.append(ls[hd] + _fold_groups(p, jnp.add))
            p_ref[hd] = p.astype(BF16)
        vt = vt_ref[j]
        for hd in range(N_Q_HEADS):
            r0 = HEAD_DIM * (hd // Q_PER_KV)
            acc_ref[hd] += _dot(vt[r0:r0 + HEAD_DIM, :], p_ref[hd])
        return tuple(out)

    ls = _chunk_loop(nch, pv_body, tuple(jnp.zeros((SUBLANES, rows), F32) for _ in range(N_Q_HEADS)))

    outs = []
    for hd in range(N_Q_HEADS):
        l = _fold_sublanes(ls[hd], jnp.add)
        outs.append(acc_ref[hd] / l[0:1, :])
    out_ref[...] = jnp.concatenate(outs, axis=0).T.astype(BF16)


def _dsa(q, idx, kcat, k, vt, batch, seq):
    rows = DSA_ROWS
    nq = seq // rows
    topk = min(TOPK_MAX, seq // 4)
    assert topk <= rows and seq % rows == 0
    q3 = q.reshape(batch, seq, QPAD)
    idx3 = idx.reshape(batch, seq, IDXW)
    vt = vt.reshape(batch, nq, LANES, rows)
    full = pl.BlockSpec((None, seq, LANES), lambda b, i: (b, 0, 0))
    kern = functools.partial(_dsa_kernel, seq=seq, topk=topk)
    return pl.pallas_call(
        kern,
        grid=(batch, nq),
        in_specs=[pl.BlockSpec((None, rows, QPAD), lambda b, i: (b, i, 0)),
                  pl.BlockSpec((None, rows, IDXW), lambda b, i: (b, i, 0)),
                  full, full,
                  pl.BlockSpec((None, nq, LANES, rows), lambda b, i: (b, 0, 0, 0))],
        out_specs=pl.BlockSpec((None, rows, N_Q_HEADS * HEAD_DIM), lambda b, i: (b, i, 0)),
        out_shape=jax.ShapeDtypeStruct((batch, seq, N_Q_HEADS * HEAD_DIM), BF16),
        scratch_shapes=[pltpu.VMEM((nq, rows, rows), F32),
                        pltpu.VMEM((IDX_HEADS, rows, LANES), BF16),
                        pltpu.VMEM((N_Q_HEADS, HEAD_DIM, rows), F32),
                        pltpu.VMEM((N_Q_HEADS, rows, rows), BF16),
                        pltpu.VMEM((N_Q_HEADS, nq, rows, rows), F32)],
        compiler_params=pltpu.CompilerParams(dimension_semantics=("arbitrary", "arbitrary"),
                                             vmem_limit_bytes=VMEM_LIMIT_BYTES),
        name="dsa",
    )(q3, idx3, kcat.reshape(batch, seq, LANES), k.reshape(batch, seq, LANES), vt)


def _mixffn_kernel(x_ref, xp_ref, pool_ref, pp_ref, b_ref, bp_ref, pw_ref, ps_ref, woa_ref, wob_ref,
                   g2_ref, wup_ref, cw_ref, cb_ref, wdn_ref, out_ref, act_ref):
    rows = FFN_ROWS
    ext = MIX_HALO + rows
    ti = pl.program_id(1)
    row = lax.broadcasted_iota(jnp.int32, (ext, 1), 0)
    t = ti * rows - MIX_HALO + row

    pool_ext = jnp.concatenate([jnp.where(ti == 0, 0.0, pp_ref[...]), pool_ref[...]], axis=0)
    lead = POOL_BACK - MIX_HALO
    parts = []
    for g, w in enumerate(POOL_WINDOWS):
        sl = slice(g * POOL_GROUP, (g + 1) * POOL_GROUP)
        a = pool_ext[:, sl]
        span = 1
        while span < w:
            a = a + pltpu.roll(a, span, 0)
            span *= 2
        cnt = jnp.clip(t + 1, 1, w).astype(F32)
        p = a[lead:] / cnt - pool_ext[lead:, sl]
        parts.append(_dot(p.astype(BF16), pw_ref[g]) * ps_ref[:, sl])
    a_out = jnp.concatenate(parts, axis=1).astype(BF16)

    x_ext = jnp.concatenate([xp_ref[...], x_ref[...]], axis=0)
    b_ext = jnp.concatenate([bp_ref[...], b_ref[...]], axis=0)
    x1 = x_ext + _dot(a_out, woa_ref[...]) + _dot(b_ext, wob_ref[...])

    h = x1 * lax.rsqrt(jnp.mean(x1 * x1, axis=-1, keepdims=True) + EPS) * g2_ref[...]
    h_ext = jnp.where(jnp.logical_and(ti == 0, row < MIX_HALO), 0.0, h).astype(BF16)

    chunk = 2 * LANES
    for c in range(D_FF // chunk):
        halves = []
        for off in (0, D_FF):
            sl = slice(off + c * chunk, off + (c + 1) * chunk)
            u = _dot(h_ext, wup_ref[:, sl])
            u1 = pltpu.roll(u, 1, 0)
            u2 = pltpu.roll(u, 2, 0)
            cv = cb_ref[:, sl] + u2[MIX_HALO:] * cw_ref[0:1, sl]
            cv = cv + u1[MIX_HALO:] * cw_ref[1:2, sl]
            cv = cv + u[MIX_HALO:] * cw_ref[2:3, sl]
            halves.append(cv)
        gate, val = halves
        act = gate * (1.0 / (1.0 + jnp.exp(-gate))) * val
        act_ref[:, c * chunk:(c + 1) * chunk] = act.astype(BF16)
    out_ref[...] = x1[MIX_HALO:] + _dot(act_ref[...], wdn_ref[...])


def _mixffn(x3, pool3, b_out, pw, ps, woa, wob, g2, wup, cw, cb, wdn):
    batch, seq, _ = x3.shape
    rows = FFN_ROWS
    tile = lambda w: pl.BlockSpec((None, rows, w), lambda b, i: (b, i, 0))
    back = lambda h, w: pl.BlockSpec((None, h, w), lambda b, i: (b, jnp.maximum(i * (rows // h) - 1, 0), 0))
    const = lambda a: pl.BlockSpec(a.shape, lambda b, i: (0,) * a.ndim, pipeline_mode=pl.Buffered(1))
    return pl.pallas_call(
        _mixffn_kernel,
        grid=(batch, seq // rows),
        in_specs=[tile(D_MODEL), back(MIX_HALO, D_MODEL), tile(POOL_WIDTH), back(POOL_BACK, POOL_WIDTH),
                  tile(POOL_WIDTH), back(MIX_HALO, POOL_WIDTH),
                  const(pw), const(ps), const(woa), const(wob), const(g2), const(wup), const(cw), const(cb),
                  const(wdn)],
        out_specs=tile(D_MODEL),
        out_shape=jax.ShapeDtypeStruct((batch, seq, D_MODEL), F32),
        scratch_shapes=[pltpu.VMEM((rows, D_FF), BF16)],
        compiler_params=pltpu.CompilerParams(dimension_semantics=("arbitrary", "arbitrary"),
                                             vmem_limit_bytes=VMEM_LIMIT_FFN_BYTES),
        name="mixffn",
    )(x3, x3, pool3, pool3, b_out, b_out, pw, ps, woa, wob, g2, wup, cw, cb, wdn)


def _rope_tables(seq, dim):
    half = dim // 2
    inv = jnp.exp(-jnp.log(jnp.float32(ROPE_THETA)) * jnp.arange(half, dtype=F32) / half)
    ang = jnp.arange(seq).astype(F32)[:, None] * inv[None, :]
    cos, sin = jnp.cos(ang), jnp.sin(ang)
    return jnp.concatenate([cos, cos], axis=1), jnp.concatenate([-sin, sin], axis=1)


def _layer(x, norm1_g, w_in, q_norm_g, k_norm_g, pool_w, pool_scale, w_out, norm2_g, w_up, conv_w, conv_b, w_down):
    batch, seq, _ = x.shape
    n = batch * seq
    v0 = MAIN_W - N_KV_HEADS * HEAD_DIM
    wmain = w_in[:, :v0].astype(BF16)
    wvt = w_in[:, v0:MAIN_W].T.astype(BF16)
    w_qi, w_ki, w_wi = jnp.split(w_in[:, MAIN_W:], [IDX_HEADS * IDX_DIM, (IDX_HEADS + 1) * IDX_DIM], axis=1)
    qg = jnp.concatenate([q_norm_g, q_norm_g])[None, :]
    kg = jnp.concatenate([k_norm_g, k_norm_g])[None, :]
    slots = LANES // IDX_DIM
    w_idx = jnp.concatenate([w_qi, w_wi, jnp.zeros((D_MODEL, LANES - IDX_HEADS), F32)] + [w_ki] * slots, axis=1)

    c64, s64 = _rope_tables(seq, HEAD_DIM)
    qcos, qsin = jnp.tile(c64, (1, LANES // HEAD_DIM)), jnp.tile(s64, (1, LANES // HEAD_DIM))
    c32, s32 = _rope_tables(seq, IDX_DIM)
    wscale = jnp.full((seq, IDX_HEADS), IDX_HEADS ** -0.5 * IDX_DIM ** -0.5, F32)
    gap = jnp.zeros((seq, LANES - IDX_HEADS), F32)
    icos = jnp.concatenate([jnp.tile(c32, (1, IDX_HEADS)), wscale, gap, jnp.tile(c32, (1, slots))], axis=1)
    isin = jnp.concatenate([jnp.tile(s32, (1, IDX_HEADS)), jnp.zeros((seq, LANES), F32),
                            jnp.tile(s32, (1, slots))], axis=1)

    x2 = x.reshape(n, D_MODEL)
    pool, q, k, vt, idx, kcat = _inproj(x2, norm1_g[None, :], wmain, w_idx.astype(BF16), wvt, qg, kg,
                                        qcos, qsin, icos, isin, seq)
    b_out = _dsa(q, idx, kcat, k, vt, batch, seq)
    return _mixffn(x, pool.reshape(batch, seq, POOL_WIDTH), b_out, pool_w.astype(BF16), pool_scale[None, :],
                   w_out[:POOL_WIDTH].astype(BF16), w_out[POOL_WIDTH:].astype(BF16), norm2_g[None, :],
                   w_up.astype(BF16), conv_w, conv_b[None, :], w_down.astype(BF16))


def kernel(x, norm1_g, w_in, q_norm_g, k_norm_g, pool_w, pool_scale, w_out, norm2_g, w_up, conv_w, conv_b, w_down):
    for l in range(norm1_g.shape[0]):
        x = _layer(x, norm1_g[l], w_in[l], q_norm_g[l], k_norm_g[l], pool_w[l], pool_scale[l], w_out[l],
                   norm2_g[l], w_up[l], conv_w[l], conv_b[l], w_down[l])
    return x
```

```python
import functools
import math

import jax
import jax.numpy as jnp
from jax import lax
from jax.experimental import pallas as pl
from jax.experimental.pallas import tpu as pltpu

F32 = jnp.float32
BF16 = jnp.bfloat16

D_MODEL = 1024
POOL_WIDTH = 512
POOL_WINDOWS = (2, 4, 8, 16)
POOL_GROUP = 128
HEAD_DIM = 64
N_Q_HEADS = 8
N_KV_HEADS = 2
Q_PER_KV = 4
IDX_HEADS = 8
IDX_DIM = 32
TOPK_MAX = 256
ROPE_THETA = 10000.0
D_FF = 2816
CONV_WIDTH = 3
EPS = 1e-6
NEG = -1e30

LANES = 128
SUBLANES = 8
QPAD = N_Q_HEADS * LANES
IDXW = 3 * LANES
IDX_PROJ_W = IDXW + LANES
MAIN_W = POOL_WIDTH + (N_Q_HEADS + 2 * N_KV_HEADS) * HEAD_DIM

INPROJ_ROWS = 512
DSA_ROWS = 256
FFN_ROWS = 512
V7X_VMEM_BYTES = 64 * 1024 * 1024
VMEM_LIMIT_BYTES = 3 * V7X_VMEM_BYTES // 4
VMEM_LIMIT_FFN_BYTES = 7 * V7X_VMEM_BYTES // 8
MIX_HALO = 16
POOL_BACK = 32

Q_SCALE = HEAD_DIM ** -0.5 * math.log2(math.e)

NT_DIMS = (((1,), (1,)), ((), ()))


def _dot(a, b):
    return jnp.dot(a, b, preferred_element_type=F32)


def _dot_nt(a, b):
    return lax.dot_general(a, b, NT_DIMS, preferred_element_type=F32)


def _swap_halves(xc, half):
    lane = lax.broadcasted_iota(jnp.int32, xc.shape, 1)
    first = (lane % (2 * half)) < half
    return jnp.where(first, pltpu.roll(xc, LANES - half, 1), pltpu.roll(xc, half, 1))


def _inproj_kernel(x_ref, g1_ref, wmain_ref, widx_ref, wvt_ref, qg_ref, kg_ref,
                   qcos_ref, qsin_ref, icos_ref, isin_ref,
                   pool_ref, q_ref, k_ref, vt_ref, idx_ref, kcat_ref):
    x = x_ref[...]
    ms = jnp.mean(x * x, axis=-1, keepdims=True)
    h = x * lax.rsqrt(ms + EPS) * g1_ref[...]
    h_hi = h.astype(BF16)

    qcos = qcos_ref[...]
    qsin = qsin_ref[...]
    lane = lax.broadcasted_iota(jnp.int32, (x.shape[0], LANES), 1)
    first = lane < HEAD_DIM

    def norm_rope(xc, gain):
        sq = xc * xc
        s0 = jnp.sum(jnp.where(first, sq, 0.0), axis=-1, keepdims=True)
        s1 = jnp.sum(jnp.where(first, 0.0, sq), axis=-1, keepdims=True)
        ss = jnp.where(first, s0, s1) * (1.0 / HEAD_DIM)
        xn = xc * lax.rsqrt(ss + EPS) * gain
        return xn * qcos + _swap_halves(xn, HEAD_DIM // 2) * qsin

    n_qcols = N_Q_HEADS * HEAD_DIM // LANES
    qk = _dot(h_hi, wmain_ref[:, POOL_WIDTH:])
    idxp = _dot(h_hi, widx_ref[...])

    for c in range(n_qcols):
        qr = norm_rope(qk[:, c * LANES:(c + 1) * LANES], qg_ref[...]) * Q_SCALE
        qs = pltpu.roll(qr, HEAD_DIM, 1)
        if (2 * c) // Q_PER_KV == 0:
            pair = (jnp.where(first, qr, 0.0), jnp.where(first, qs, 0.0))
        else:
            pair = (jnp.where(first, 0.0, qs), jnp.where(first, 0.0, qr))
        for i, qh in enumerate(pair):
            q_ref[:, (2 * c + i) * LANES:(2 * c + i + 1) * LANES] = qh.astype(BF16)
    k_ref[...] = norm_rope(qk[:, n_qcols * LANES:], kg_ref[...]).astype(BF16)

    for c in range(IDX_PROJ_W // LANES):
        sl = slice(c * LANES, (c + 1) * LANES)
        ic = idxp[:, sl]
        roped = ic * icos_ref[:, sl] + _swap_halves(ic, IDX_DIM // 2) * isin_ref[:, sl]
        if c < IDXW // LANES:
            idx_ref[:, sl] = roped
        else:
            kcat_ref[...] = roped.astype(BF16)

    pool_ref[...] = _dot(h_hi, wmain_ref[:, :POOL_WIDTH])
    vt = _dot_nt(wvt_ref[...], h_hi).astype(BF16)
    for i in range(vt_ref.shape[0]):
        vt_ref[i] = vt[:, i * DSA_ROWS:(i + 1) * DSA_ROWS]


def _inproj(x2, g1, wmain, widx, wvt, qg, kg, qcos, qsin, icos, isin, seq):
    n = x2.shape[0]
    rows = INPROJ_ROWS
    tper = seq // rows
    vblk = rows // DSA_ROWS
    row = lambda w: pl.BlockSpec((rows, w), lambda i: (i, 0))
    const = lambda a: pl.BlockSpec(a.shape, lambda i: (0, 0))
    table = lambda w: pl.BlockSpec((rows, w), lambda i: (i % tper, 0))
    return pl.pallas_call(
        _inproj_kernel,
        grid=(n // rows,),
        in_specs=[row(D_MODEL), const(g1), const(wmain), const(widx), const(wvt), const(qg), const(kg),
                  table(LANES), table(LANES), table(IDX_PROJ_W), table(IDX_PROJ_W)],
        out_specs=[row(POOL_WIDTH), row(QPAD), row(LANES),
                   pl.BlockSpec((vblk, LANES, DSA_ROWS), lambda i: (i, 0, 0)), row(IDXW), row(LANES)],
        out_shape=[jax.ShapeDtypeStruct((n, POOL_WIDTH), F32),
                   jax.ShapeDtypeStruct((n, QPAD), BF16),
                   jax.ShapeDtypeStruct((n, LANES), BF16),
                   jax.ShapeDtypeStruct((n // DSA_ROWS, LANES, DSA_ROWS), BF16),
                   jax.ShapeDtypeStruct((n, IDXW), F32),
                   jax.ShapeDtypeStruct((n, LANES), BF16)],
        compiler_params=pltpu.CompilerParams(dimension_semantics=("arbitrary",),
                                             vmem_limit_bytes=VMEM_LIMIT_BYTES),
        name="inproj",
    )(x2, g1, wmain, widx, wvt, qg, kg, qcos, qsin, icos, isin)


def _key_to_float(key):
    bits = key ^ ((key >> 31) & jnp.int32(0x7FFFFFFF))
    return lax.bitcast_convert_type(bits, F32)


def _fold_groups(x, op, ways=4):
    n = x.shape[0] // SUBLANES
    accs = [x[g * SUBLANES:(g + 1) * SUBLANES, :] for g in range(ways)]
    for g in range(ways, n):
        accs[g % ways] = op(accs[g % ways], x[g * SUBLANES:(g + 1) * SUBLANES, :])
    while len(accs) > 1:
        accs = [op(accs[i], accs[i + 1]) for i in range(0, len(accs), 2)]
    return accs[0]


def _fold_sublanes(x, op):
    for shift in (4, 2, 1):
        x = op(x, pltpu.roll(x, shift, 0))
    return x


def _chunk_loop(n, body, init, unroll=4):
    def run(start, width, carry):
        for w in range(width):
            carry = body(start + w, carry)
        return carry

    carry = lax.fori_loop(0, n // unroll, lambda i, c: run(unroll * i, unroll, c), init)
    done = (n // unroll) * unroll
    width = unroll // 2
    while width:
        carry = lax.cond((n & width) != 0, functools.partial(run, done, width), lambda c: c, carry)
        done = done + (n & width)
        width //= 2
    return carry


def _dsa_kernel(q_ref, idxq_ref, kcat_ref, k_ref, vt_ref, out_ref, sc_ref, qcat_ref, acc_ref, lg_ref,
                *, seq, topk):
    rows = DSA_ROWS
    ck = DSA_ROWS
    ngrp = ck // SUBLANES
    qb = pl.program_id(1)
    nch = qb + 1
    lane = lax.broadcasted_iota(jnp.int32, (rows, LANES), 1)
    qidx = qb * rows + lax.broadcasted_iota(jnp.int32, (ck, rows), 1)
    krow = lax.broadcasted_iota(jnp.int32, (ck, rows), 0)

    slots = LANES // IDX_DIM
    for c in range(IDX_HEADS // slots):
        col = idxq_ref[:, c * LANES:(c + 1) * LANES]
        for r in range(slots):
            live = jnp.logical_and(lane >= r * IDX_DIM, lane < (r + 1) * IDX_DIM)
            qcat_ref[c * slots + r] = jnp.where(live, col, 0.0).astype(BF16)
    wt = idxq_ref[:, 2 * LANES:].T[0:IDX_HEADS, :]

    def score_body(j, nonneg):
        kc = kcat_ref[pl.ds(pl.multiple_of(j * ck, ck), ck), :]
        acc = jnp.zeros((ck, rows), F32)
        for hd in range(IDX_HEADS):
            y = _dot_nt(kc, qcat_ref[hd])
            acc = acc + wt[hd:hd + 1, :] * jnp.maximum(y, 0.0)
        sc = jnp.where(j * ck + krow <= qidx, acc, NEG)
        sc_ref[j] = sc
        return nonneg + _fold_groups(jnp.where(sc >= 0.0, 1.0, 0.0), jnp.add)

    nonneg = _chunk_loop(nch, score_body, jnp.zeros((SUBLANES, rows), F32))

    n_virtual = (seq - nch * ck).astype(F32)

    def count(pred, thr):
        def body(j, accs):
            accs = list(accs)
            for g in range(ngrp):
                hit = pred(sc_ref[j, g * SUBLANES:(g + 1) * SUBLANES, :], thr)
                accs[g % len(accs)] = accs[g % len(accs)] + jnp.where(hit, 1.0, 0.0)
            return tuple(accs)
        zero = jnp.zeros((SUBLANES, rows), F32)
        accs = _chunk_loop(nch, body, (zero, zero, zero, zero))
        tot = _fold_sublanes((accs[0] + accs[1]) + (accs[2] + accs[3]), jnp.add)
        return tot + jnp.where(pred(jnp.float32(NEG), thr), n_virtual, 0.0)

    ge = lambda a, b: a >= b
    gt = lambda a, b: a > b
    kf = jnp.float32(topk)

    nonneg = _fold_sublanes(nonneg, jnp.add)
    positive = nonneg >= kf
    key = jnp.where(positive, jnp.int32(0), jnp.int32(-2 ** 31))
    n_ge = jnp.where(positive, nonneg, jnp.float32(seq))

    def bit_body(i, carry):
        key, n_ge = carry
        cand = key | jnp.left_shift(jnp.int32(1), 30 - i)
        cnt = count(ge, _key_to_float(cand))
        return jnp.where(cnt >= kf, cand, key), jnp.where(cnt >= kf, cnt, n_ge)

    key, n_ge = lax.fori_loop(0, 31, bit_body, (key, n_ge))
    thr8 = _key_to_float(key)
    thr = thr8[0:1, :]

    def logits(j, ms):
        kc = k_ref[pl.ds(pl.multiple_of(j * ck, ck), ck), :]
        out = []
        for hd in range(N_Q_HEADS):
            s = _dot_nt(kc, q_ref[:, hd * LANES:(hd + 1) * LANES]) + sc_ref[j]
            lg_ref[hd, j] = s
            out.append(jnp.maximum(ms[hd], _fold_groups(s, jnp.maximum)))
        return tuple(out)

    ms0 = tuple(jnp.full((SUBLANES, rows), -jnp.inf, F32) for _ in range(N_Q_HEADS))

    def select_exact():
        def body(j, ms):
            sc_ref[j] = jnp.where(sc_ref[j] >= thr, 0.0, NEG)
            return logits(j, ms)
        return _chunk_loop(nch, body, ms0)

    def select_ties():
        need = (kf - count(gt, thr8))[0:1, :]
        ki = lax.broadcasted_iota(jnp.int32, (ck, ck), 0)
        kj = lax.broadcasted_iota(jnp.int32, (ck, ck), 1)
        lower = jnp.where(kj < ki, 1.0, 0.0).astype(BF16)
        ones8 = jnp.ones((SUBLANES, ck), BF16)

        def body(j, carry):
            run, ms = carry
            sc = sc_ref[j]
            eq = sc == thr
            eqb = jnp.where(eq, 1.0, 0.0).astype(BF16)
            prior = run[0:1, :] + _dot(lower, eqb)
            tie = jnp.where(prior < need, 0.0, NEG)
            bias = jnp.where(sc > thr, 0.0, jnp.where(eq, tie, NEG))
            sc_ref[j] = jnp.where(j * ck + krow <= qidx, bias, NEG)
            return run + _dot(ones8, eqb), logits(j, ms)
        return _chunk_loop(nch, body, (jnp.zeros((SUBLANES, rows), F32), ms0))[1]

    all_exact = jnp.min(jnp.where(n_ge == kf, 1.0, 0.0)) > 0.5
    ms = lax.cond(all_exact, select_exact, select_ties)
    ms = [_fold_sublanes(m, jnp.maximum)[0:1, :] for m in ms]

    acc_ref[...] = jnp.zeros(acc_ref.shape, F32)

    def pv_body(j, ls):
        vt = vt_ref[j]
        out = []
        for hd in range(N_Q_HEADS):
            p = jnp.exp2(lg_ref[hd, j] - ms[hd])
            out.append(ls[hd] + _fold_groups(p, jnp.add))
            r0 = HEAD_DIM * (hd // Q_PER_KV)
            acc_ref[hd] += _dot(vt[r0:r0 + HEAD_DIM, :], p.astype(BF16))
        return tuple(out)

    ls = _chunk_loop(nch, pv_body, tuple(jnp.zeros((SUBLANES, rows), F32) for _ in range(N_Q_HEADS)))

    outs = []
    for hd in range(N_Q_HEADS):
        l = _fold_sublanes(ls[hd], jnp.add)
        outs.append(acc_ref[hd] / l[0:1, :])
    out_ref[...] = jnp.concatenate(outs, axis=0).T.astype(BF16)


def _dsa(q, idx, kcat, k, vt, batch, seq):
    rows = DSA_ROWS
    nq = seq // rows
    topk = min(TOPK_MAX, seq // 4)
    assert topk <= rows and seq % rows == 0
    q3 = q.reshape(batch, seq, QPAD)
    idx3 = idx.reshape(batch, seq, IDXW)
    vt = vt.reshape(batch, nq, LANES, rows)
    full = pl.BlockSpec((None, seq, LANES), lambda b, i: (b, 0, 0))
    kern = functools.partial(_dsa_kernel, seq=seq, topk=topk)
    return pl.pallas_call(
        kern,
        grid=(batch, nq),
        in_specs=[pl.BlockSpec((None, rows, QPAD), lambda b, i: (b, i, 0)),
                  pl.BlockSpec((None, rows, IDXW), lambda b, i: (b, i, 0)),
                  full, full,
                  pl.BlockSpec((None, nq, LANES, rows), lambda b, i: (b, 0, 0, 0))],
        out_specs=pl.BlockSpec((None, rows, N_Q_HEADS * HEAD_DIM), lambda b, i: (b, i, 0)),
        out_shape=jax.ShapeDtypeStruct((batch, seq, N_Q_HEADS * HEAD_DIM), BF16),
        scratch_shapes=[pltpu.VMEM((nq, rows, rows), F32),
                        pltpu.VMEM((IDX_HEADS, rows, LANES), BF16),
                        pltpu.VMEM((N_Q_HEADS, HEAD_DIM, rows), F32),
                        pltpu.VMEM((N_Q_HEADS, nq, rows, rows), F32)],
        compiler_params=pltpu.CompilerParams(dimension_semantics=("arbitrary", "arbitrary"),
                                             vmem_limit_bytes=VMEM_LIMIT_BYTES),
        name="dsa",
    )(q3, idx3, kcat.reshape(batch, seq, LANES), k.reshape(batch, seq, LANES), vt)


def _mixffn_kernel(x_ref, xp_ref, pool_ref, pp_ref, b_ref, bp_ref, pw_ref, ps_ref, woa_ref, wob_ref,
                   g2_ref, wup_ref, cw_ref, cb_ref, wdn_ref, out_ref, act_ref):
    rows = FFN_ROWS
    ext = MIX_HALO + rows
    ti = pl.program_id(1)
    row = lax.broadcasted_iota(jnp.int32, (ext, 1), 0)
    t = ti * rows - MIX_HALO + row

    pool_ext = jnp.concatenate([jnp.where(ti == 0, 0.0, pp_ref[...]), pool_ref[...]], axis=0)
    lead = POOL_BACK - MIX_HALO
    parts = []
    for g, w in enumerate(POOL_WINDOWS):
        sl = slice(g * POOL_GROUP, (g + 1) * POOL_GROUP)
        a = pool_ext[:, sl]
        span = 1
        while span < w:
            a = a + pltpu.roll(a, span, 0)
            span *= 2
        cnt = jnp.clip(t + 1, 1, w).astype(F32)
        p = a[lead:] / cnt - pool_ext[lead:, sl]
        parts.append(_dot(p.astype(BF16), pw_ref[g]) * ps_ref[:, sl])
    a_out = jnp.concatenate(parts, axis=1).astype(BF16)

    x_ext = jnp.concatenate([xp_ref[...], x_ref[...]], axis=0)
    b_ext = jnp.concatenate([bp_ref[...], b_ref[...]], axis=0)
    x1 = x_ext + _dot(a_out, woa_ref[...]) + _dot(b_ext, wob_ref[...])

    h = x1 * lax.rsqrt(jnp.mean(x1 * x1, axis=-1, keepdims=True) + EPS) * g2_ref[...]
    h_ext = jnp.where(jnp.logical_and(ti == 0, row < MIX_HALO), 0.0, h).astype(BF16)

    chunk = 2 * LANES
    for c in range(D_FF // chunk):
        halves = []
        for off in (0, D_FF):
            sl = slice(off + c * chunk, off + (c + 1) * chunk)
            u = _dot(h_ext, wup_ref[:, sl])
            u1 = pltpu.roll(u, 1, 0)
            u2 = pltpu.roll(u, 2, 0)
            cv = cb_ref[:, sl] + u2[MIX_HALO:] * cw_ref[0:1, sl]
            cv = cv + u1[MIX_HALO:] * cw_ref[1:2, sl]
            cv = cv + u[MIX_HALO:] * cw_ref[2:3, sl]
            halves.append(cv)
        gate, val = halves
        act = gate * (1.0 / (1.0 + jnp.exp(-gate))) * val
        act_ref[:, c * chunk:(c + 1) * chunk] = act.astype(BF16)
    out_ref[...] = x1[MIX_HALO:] + _dot(act_ref[...], wdn_ref[...])


def _mixffn(x3, pool3, b_out, pw, ps, woa, wob, g2, wup, cw, cb, wdn):
    batch, seq, _ = x3.shape
    rows = FFN_ROWS
    tile = lambda w: pl.BlockSpec((None, rows, w), lambda b, i: (b, i, 0))
    back = lambda h, w: pl.BlockSpec((None, h, w), lambda b, i: (b, jnp.maximum(i * (rows // h) - 1, 0), 0))
    const = lambda a: pl.BlockSpec(a.shape, lambda b, i: (0,) * a.ndim, pipeline_mode=pl.Buffered(1))
    return pl.pallas_call(
        _mixffn_kernel,
        grid=(batch, seq // rows),
        in_specs=[tile(D_MODEL), back(MIX_HALO, D_MODEL), tile(POOL_WIDTH), back(POOL_BACK, POOL_WIDTH),
                  tile(POOL_WIDTH), back(MIX_HALO, POOL_WIDTH),
                  const(pw), const(ps), const(woa), const(wob), const(g2), const(wup), const(cw), const(cb),
                  const(wdn)],
        out_specs=tile(D_MODEL),
        out_shape=jax.ShapeDtypeStruct((batch, seq, D_MODEL), F32),
        scratch_shapes=[pltpu.VMEM((rows, D_FF), BF16)],
        compiler_params=pltpu.CompilerParams(dimension_semantics=("arbitrary", "arbitrary"),
                                             vmem_limit_bytes=VMEM_LIMIT_FFN_BYTES),
        name="mixffn",
    )(x3, x3, pool3, pool3, b_out, b_out, pw, ps, woa, wob, g2, wup, cw, cb, wdn)


def _rope_tables(seq, dim):
    half = dim // 2
    inv = jnp.exp(-jnp.log(jnp.float32(ROPE_THETA)) * jnp.arange(half, dtype=F32) / half)
    ang = jnp.arange(seq).astype(F32)[:, None] * inv[None, :]
    cos, sin = jnp.cos(ang), jnp.sin(ang)
    return jnp.concatenate([cos, cos], axis=1), jnp.concatenate([-sin, sin], axis=1)


def _layer(x, norm1_g, w_in, q_norm_g, k_norm_g, pool_w, pool_scale, w_out, norm2_g, w_up, conv_w, conv_b, w_down):
    batch, seq, _ = x.shape
    n = batch * seq
    v0 = MAIN_W - N_KV_HEADS * HEAD_DIM
    wmain = w_in[:, :v0].astype(BF16)
    wvt = w_in[:, v0:MAIN_W].T.astype(BF16)
    w_qi, w_ki, w_wi = jnp.split(w_in[:, MAIN_W:], [IDX_HEADS * IDX_DIM, (IDX_HEADS + 1) * IDX_DIM], axis=1)
    qg = jnp.concatenate([q_norm_g, q_norm_g])[None, :]
    kg = jnp.concatenate([k_norm_g, k_norm_g])[None, :]
    slots = LANES // IDX_DIM
    w_idx = jnp.concatenate([w_qi, w_wi, jnp.zeros((D_MODEL, LANES - IDX_HEADS), F32)] + [w_ki] * slots, axis=1)

    c64, s64 = _rope_tables(seq, HEAD_DIM)
    qcos, qsin = jnp.tile(c64, (1, LANES // HEAD_DIM)), jnp.tile(s64, (1, LANES // HEAD_DIM))
    c32, s32 = _rope_tables(seq, IDX_DIM)
    wscale = jnp.full((seq, IDX_HEADS), IDX_HEADS ** -0.5 * IDX_DIM ** -0.5, F32)
    gap = jnp.zeros((seq, LANES - IDX_HEADS), F32)
    icos = jnp.concatenate([jnp.tile(c32, (1, IDX_HEADS)), wscale, gap, jnp.tile(c32, (1, slots))], axis=1)
    isin = jnp.concatenate([jnp.tile(s32, (1, IDX_HEADS)), jnp.zeros((seq, LANES), F32),
                            jnp.tile(s32, (1, slots))], axis=1)

    x2 = x.reshape(n, D_MODEL)
    pool, q, k, vt, idx, kcat = _inproj(x2, norm1_g[None, :], wmain, w_idx.astype(BF16), wvt, qg, kg,
                                        qcos, qsin, icos, isin, seq)
    b_out = _dsa(q, idx, kcat, k, vt, batch, seq)
    return _mixffn(x, pool.reshape(batch, seq, POOL_WIDTH), b_out, pool_w.astype(BF16), pool_scale[None, :],
                   w_out[:POOL_WIDTH].astype(BF16), w_out[POOL_WIDTH:].astype(BF16), norm2_g[None, :],
                   w_up.astype(BF16), conv_w, conv_b[None, :], w_down.astype(BF16))


def kernel(x, norm1_g, w_in, q_norm_g, k_norm_g, pool_w, pool_scale, w_out, norm2_g, w_up, conv_w, conv_b, w_down):
    for l in range(norm1_g.shape[0]):
        x = _layer(x, norm1_g[l], w_in[l], q_norm_g[l], k_norm_g[l], pool_w[l], pool_scale[l], w_out[l],
                   norm2_g[l], w_up[l], conv_w[l], conv_b[l], w_down[l])
    return x
```

```python
import functools
import math

import jax
import jax.numpy as jnp
from jax import lax
from jax.experimental import pallas as pl
from jax.experimental.pallas import tpu as pltpu

F32 = jnp.float32
BF16 = jnp.bfloat16

D_MODEL = 1024
POOL_WIDTH = 512
POOL_WINDOWS = (2, 4, 8, 16)
POOL_GROUP = 128
HEAD_DIM = 64
N_Q_HEADS = 8
N_KV_HEADS = 2
Q_PER_KV = 4
IDX_HEADS = 8
IDX_DIM = 32
TOPK_MAX = 256
ROPE_THETA = 10000.0
D_FF = 2816
CONV_WIDTH = 3
EPS = 1e-6
NEG = -1e30

LANES = 128
SUBLANES = 8
QPAD = N_Q_HEADS * LANES
IDXW = 3 * LANES
IDX_PROJ_W = IDXW + LANES
MAIN_W = POOL_WIDTH + (N_Q_HEADS + 2 * N_KV_HEADS) * HEAD_DIM

INPROJ_ROWS = 512
DSA_ROWS = 256
FFN_ROWS = 512
V7X_VMEM_BYTES = 64 * 1024 * 1024
VMEM_LIMIT_BYTES = 3 * V7X_VMEM_BYTES // 4
VMEM_LIMIT_FFN_BYTES = 7 * V7X_VMEM_BYTES // 8
MIX_HALO = 16
POOL_BACK = 32

Q_SCALE = HEAD_DIM ** -0.5 * math.log2(math.e)

NT_DIMS = (((1,), (1,)), ((), ()))


def _dot(a, b):
    return jnp.dot(a, b, preferred_element_type=F32)


def _dot_nt(a, b):
    return lax.dot_general(a, b, NT_DIMS, preferred_element_type=F32)


def _swap_halves(xc, half):
    lane = lax.broadcasted_iota(jnp.int32, xc.shape, 1)
    first = (lane % (2 * half)) < half
    return jnp.where(first, pltpu.roll(xc, LANES - half, 1), pltpu.roll(xc, half, 1))


def _inproj_kernel(x_ref, g1_ref, wmain_ref, widx_ref, wvt_ref, qg_ref, kg_ref,
                   qcos_ref, qsin_ref, icos_ref, isin_ref,
                   pool_ref, q_ref, k_ref, vt_ref, idx_ref, kcat_ref):
    x = x_ref[...]
    ms = jnp.mean(x * x, axis=-1, keepdims=True)
    h = x * lax.rsqrt(ms + EPS) * g1_ref[...]
    h_hi = h.astype(BF16)

    qcos = qcos_ref[...]
    qsin = qsin_ref[...]
    lane = lax.broadcasted_iota(jnp.int32, (x.shape[0], LANES), 1)
    first = lane < HEAD_DIM

    def norm_rope(xc, gain):
        sq = xc * xc
        s0 = jnp.sum(jnp.where(first, sq, 0.0), axis=-1, keepdims=True)
        s1 = jnp.sum(jnp.where(first, 0.0, sq), axis=-1, keepdims=True)
        ss = jnp.where(first, s0, s1) * (1.0 / HEAD_DIM)
        xn = xc * lax.rsqrt(ss + EPS) * gain
        return xn * qcos + _swap_halves(xn, HEAD_DIM // 2) * qsin

    n_qcols = N_Q_HEADS * HEAD_DIM // LANES
    qk = _dot(h_hi, wmain_ref[:, POOL_WIDTH:])
    idxp = _dot(h_hi, widx_ref[...])

    for c in range(n_qcols):
        qr = norm_rope(qk[:, c * LANES:(c + 1) * LANES], qg_ref[...]) * Q_SCALE
        qs = pltpu.roll(qr, HEAD_DIM, 1)
        if (2 * c) // Q_PER_KV == 0:
            pair = (jnp.where(first, qr, 0.0), jnp.where(first, qs, 0.0))
        else:
            pair = (jnp.where(first, 0.0, qs), jnp.where(first, 0.0, qr))
        for i, qh in enumerate(pair):
            q_ref[:, (2 * c + i) * LANES:(2 * c + i + 1) * LANES] = qh.astype(BF16)
    k_ref[...] = norm_rope(qk[:, n_qcols * LANES:], kg_ref[...]).astype(BF16)

    for c in range(IDX_PROJ_W // LANES):
        sl = slice(c * LANES, (c + 1) * LANES)
        ic = idxp[:, sl]
        roped = ic * icos_ref[:, sl] + _swap_halves(ic, IDX_DIM // 2) * isin_ref[:, sl]
        if c < IDXW // LANES:
            idx_ref[:, sl] = roped
        else:
            kcat_ref[...] = roped.astype(BF16)

    pool_ref[...] = _dot(h_hi, wmain_ref[:, :POOL_WIDTH])
    vt = _dot_nt(wvt_ref[...], h_hi).astype(BF16)
    for i in range(vt_ref.shape[0]):
        vt_ref[i] = vt[:, i * DSA_ROWS:(i + 1) * DSA_ROWS]


def _inproj(x2, g1, wmain, widx, wvt, qg, kg, qcos, qsin, icos, isin, seq):
    n = x2.shape[0]
    rows = INPROJ_ROWS
    tper = seq // rows
    vblk = rows // DSA_ROWS
    row = lambda w: pl.BlockSpec((rows, w), lambda i: (i, 0))
    const = lambda a: pl.BlockSpec(a.shape, lambda i: (0, 0))
    table = lambda w: pl.BlockSpec((rows, w), lambda i: (i % tper, 0))
    return pl.pallas_call(
        _inproj_kernel,
        grid=(n // rows,),
        in_specs=[row(D_MODEL), const(g1), const(wmain), const(widx), const(wvt), const(qg), const(kg),
                  table(LANES), table(LANES), table(IDX_PROJ_W), table(IDX_PROJ_W)],
        out_specs=[row(POOL_WIDTH), row(QPAD), row(LANES),
                   pl.BlockSpec((vblk, LANES, DSA_ROWS), lambda i: (i, 0, 0)), row(IDXW), row(LANES)],
        out_shape=[jax.ShapeDtypeStruct((n, POOL_WIDTH), F32),
                   jax.ShapeDtypeStruct((n, QPAD), BF16),
                   jax.ShapeDtypeStruct((n, LANES), BF16),
                   jax.ShapeDtypeStruct((n // DSA_ROWS, LANES, DSA_ROWS), BF16),
                   jax.ShapeDtypeStruct((n, IDXW), F32),
                   jax.ShapeDtypeStruct((n, LANES), BF16)],
        compiler_params=pltpu.CompilerParams(dimension_semantics=("arbitrary",),
                                             vmem_limit_bytes=VMEM_LIMIT_BYTES),
        name="inproj",
    )(x2, g1, wmain, widx, wvt, qg, kg, qcos, qsin, icos, isin)


def _key_to_float(key):
    bits = key ^ ((key >> 31) & jnp.int32(0x7FFFFFFF))
    return lax.bitcast_convert_type(bits, F32)


def _fold_groups(x, op, ways=4):
    n = x.shape[0] // SUBLANES
    accs = [x[g * SUBLANES:(g + 1) * SUBLANES, :] for g in range(ways)]
    for g in range(ways, n):
        accs[g % ways] = op(accs[g % ways], x[g * SUBLANES:(g + 1) * SUBLANES, :])
    while len(accs) > 1:
        accs = [op(accs[i], accs[i + 1]) for i in range(0, len(accs), 2)]
    return accs[0]


def _fold_sublanes(x, op):
    for shift in (4, 2, 1):
        x = op(x, pltpu.roll(x, shift, 0))
    return x


def _chunk_loop(n, body, init, unroll=4):
    def run(start, width, carry):
        for w in range(width):
            carry = body(start + w, carry)
        return carry

    carry = lax.fori_loop(0, n // unroll, lambda i, c: run(unroll * i, unroll, c), init)
    done = (n // unroll) * unroll
    tails = [lambda c: c] + [functools.partial(run, done, width) for width in range(1, unroll)]
    return lax.switch(n % unroll, tails, carry)


def _dsa_kernel(q_ref, idxq_ref, kcat_ref, k_ref, vt_ref, out_ref, sc_ref, qcat_ref, acc_ref, lg_ref,
                *, seq, topk):
    rows = DSA_ROWS
    ck = DSA_ROWS
    ngrp = ck // SUBLANES
    qb = pl.program_id(1)
    nch = qb + 1
    lane = lax.broadcasted_iota(jnp.int32, (rows, LANES), 1)
    qidx = qb * rows + lax.broadcasted_iota(jnp.int32, (ck, rows), 1)
    krow = lax.broadcasted_iota(jnp.int32, (ck, rows), 0)

    slots = LANES // IDX_DIM
    for c in range(IDX_HEADS // slots):
        col = idxq_ref[:, c * LANES:(c + 1) * LANES]
        for r in range(slots):
            live = jnp.logical_and(lane >= r * IDX_DIM, lane < (r + 1) * IDX_DIM)
            qcat_ref[c * slots + r] = jnp.where(live, col, 0.0).astype(BF16)
    wt = idxq_ref[:, 2 * LANES:].T[0:IDX_HEADS, :]

    def score_body(j, nonneg):
        kc = kcat_ref[pl.ds(pl.multiple_of(j * ck, ck), ck), :]
        acc = jnp.zeros((ck, rows), F32)
        for hd in range(IDX_HEADS):
            y = _dot_nt(kc, qcat_ref[hd])
            acc = acc + wt[hd:hd + 1, :] * jnp.maximum(y, 0.0)
        sc = jnp.where(j * ck + krow <= qidx, acc, NEG)
        sc_ref[j] = sc
        return nonneg + _fold_groups(jnp.where(sc >= 0.0, 1.0, 0.0), jnp.add)

    nonneg = _chunk_loop(nch, score_body, jnp.zeros((SUBLANES, rows), F32))

    n_virtual = (seq - nch * ck).astype(F32)

    def count(pred, thr):
        def body(j, accs):
            accs = list(accs)
            for g in range(ngrp):
                hit = pred(sc_ref[j, g * SUBLANES:(g + 1) * SUBLANES, :], thr)
                accs[g % len(accs)] = accs[g % len(accs)] + jnp.where(hit, 1.0, 0.0)
            return tuple(accs)
        zero = jnp.zeros((SUBLANES, rows), F32)
        accs = _chunk_loop(nch, body, (zero, zero, zero, zero))
        tot = _fold_sublanes((accs[0] + accs[1]) + (accs[2] + accs[3]), jnp.add)
        return tot + jnp.where(pred(jnp.float32(NEG), thr), n_virtual, 0.0)

    ge = lambda a, b: a >= b
    gt = lambda a, b: a > b
    kf = jnp.float32(topk)

    nonneg = _fold_sublanes(nonneg, jnp.add)
    positive = nonneg >= kf
    key = jnp.where(positive, jnp.int32(0), jnp.int32(-2 ** 31))
    n_ge = jnp.where(positive, nonneg, jnp.float32(seq))

    def bit_body(i, carry):
        key, n_ge = carry
        cand = key | jnp.left_shift(jnp.int32(1), 30 - i)
        cnt = count(ge, _key_to_float(cand))
        return jnp.where(cnt >= kf, cand, key), jnp.where(cnt >= kf, cnt, n_ge)

    key, n_ge = lax.fori_loop(0, 31, bit_body, (key, n_ge))
    thr8 = _key_to_float(key)
    thr = thr8[0:1, :]

    def logits(j, ms):
        kc = k_ref[pl.ds(pl.multiple_of(j * ck, ck), ck), :]
        out = []
        for hd in range(N_Q_HEADS):
            s = _dot_nt(kc, q_ref[:, hd * LANES:(hd + 1) * LANES]) + sc_ref[j]
            lg_ref[hd, j] = s
            out.append(jnp.maximum(ms[hd], _fold_groups(s, jnp.maximum)))
        return tuple(out)

    ms0 = tuple(jnp.full((SUBLANES, rows), -jnp.inf, F32) for _ in range(N_Q_HEADS))

    def select_exact():
        def body(j, ms):
            sc_ref[j] = jnp.where(sc_ref[j] >= thr, 0.0, NEG)
            return logits(j, ms)
        return _chunk_loop(nch, body, ms0)

    def select_ties():
        need = (kf - count(gt, thr8))[0:1, :]
        ki = lax.broadcasted_iota(jnp.int32, (ck, ck), 0)
        kj = lax.broadcasted_iota(jnp.int32, (ck, ck), 1)
        lower = jnp.where(kj < ki, 1.0, 0.0).astype(BF16)
        ones8 = jnp.ones((SUBLANES, ck), BF16)

        def body(j, carry):
            run, ms = carry
            sc = sc_ref[j]
            eq = sc == thr
            eqb = jnp.where(eq, 1.0, 0.0).astype(BF16)
            prior = run[0:1, :] + _dot(lower, eqb)
            tie = jnp.where(prior < need, 0.0, NEG)
            bias = jnp.where(sc > thr, 0.0, jnp.where(eq, tie, NEG))
            sc_ref[j] = jnp.where(j * ck + krow <= qidx, bias, NEG)
            return run + _dot(ones8, eqb), logits(j, ms)
        return _chunk_loop(nch, body, (jnp.zeros((SUBLANES, rows), F32), ms0))[1]

    all_exact = jnp.min(jnp.where(n_ge == kf, 1.0, 0.0)) > 0.5
    ms = lax.cond(all_exact, select_exact, select_ties)
    ms = [_fold_sublanes(m, jnp.maximum)[0:1, :] for m in ms]

    acc_ref[...] = jnp.zeros(acc_ref.shape, F32)

    def pv_body(j, ls):
        vt = vt_ref[j]
        out = []
        for hd in range(N_Q_HEADS):
            p = jnp.exp2(lg_ref[hd, j] - ms[hd])
            out.append(ls[hd] + _fold_groups(p, jnp.add))
            r0 = HEAD_DIM * (hd // Q_PER_KV)
            acc_ref[hd] += _dot(vt[r0:r0 + HEAD_DIM, :], p.astype(BF16))
        return tuple(out)

    ls = _chunk_loop(nch, pv_body, tuple(jnp.zeros((SUBLANES, rows), F32) for _ in range(N_Q_HEADS)))

    outs = []
    for hd in range(N_Q_HEADS):
        l = _fold_sublanes(ls[hd], jnp.add)
        outs.append(acc_ref[hd] / l[0:1, :])
    out_ref[...] = jnp.concatenate(outs, axis=0).T.astype(BF16)


def _dsa(q, idx, kcat, k, vt, batch, seq):
    rows = DSA_ROWS
    nq = seq // rows
    topk = min(TOPK_MAX, seq // 4)
    assert topk <= rows and seq % rows == 0
    q3 = q.reshape(batch, seq, QPAD)
    idx3 = idx.reshape(batch, seq, IDXW)
    vt = vt.reshape(batch, nq, LANES, rows)
    full = pl.BlockSpec((None, seq, LANES), lambda b, i: (b, 0, 0))
    kern = functools.partial(_dsa_kernel, seq=seq, topk=topk)
    return pl.pallas_call(
        kern,
        grid=(batch, nq),
        in_specs=[pl.BlockSpec((None, rows, QPAD), lambda b, i: (b, i, 0)),
                  pl.BlockSpec((None, rows, IDXW), lambda b, i: (b, i, 0)),
                  full, full,
                  pl.BlockSpec((None, nq, LANES, rows), lambda b, i: (b, 0, 0, 0))],
        out_specs=pl.BlockSpec((None, rows, N_Q_HEADS * HEAD_DIM), lambda b, i: (b, i, 0)),
        out_shape=jax.ShapeDtypeStruct((batch, seq, N_Q_HEADS * HEAD_DIM), BF16),
        scratch_shapes=[pltpu.VMEM((nq, rows, rows), F32),
                        pltpu.VMEM((IDX_HEADS, rows, LANES), BF16),
                        pltpu.VMEM((N_Q_HEADS, HEAD_DIM, rows), F32),
                        pltpu.VMEM((N_Q_HEADS, nq, rows, rows), F32)],
        compiler_params=pltpu.CompilerParams(dimension_semantics=("arbitrary", "arbitrary"),
                                             vmem_limit_bytes=VMEM_LIMIT_BYTES),
        name="dsa",
    )(q3, idx3, kcat.reshape(batch, seq, LANES), k.reshape(batch, seq, LANES), vt)


def _mixffn_kernel(x_ref, xp_ref, pool_ref, pp_ref, b_ref, bp_ref, pw_ref, ps_ref, woa_ref, wob_ref,
                   g2_ref, wup_ref, cw_ref, cb_ref, wdn_ref, out_ref, act_ref):
    rows = FFN_ROWS
    ext = MIX_HALO + rows
    ti = pl.program_id(1)
    row = lax.broadcasted_iota(jnp.int32, (ext, 1), 0)
    t = ti * rows - MIX_HALO + row

    pool_ext = jnp.concatenate([jnp.where(ti == 0, 0.0, pp_ref[...]), pool_ref[...]], axis=0)
    lead = POOL_BACK - MIX_HALO
    parts = []
    for g, w in enumerate(POOL_WINDOWS):
        sl = slice(g * POOL_GROUP, (g + 1) * POOL_GROUP)
        a = pool_ext[:, sl]
        span = 1
        while span < w:
            a = a + pltpu.roll(a, span, 0)
            span *= 2
        cnt = jnp.clip(t + 1, 1, w).astype(F32)
        p = a[lead:] / cnt - pool_ext[lead:, sl]
        parts.append(_dot(p.astype(BF16), pw_ref[g]) * ps_ref[:, sl])
    a_out = jnp.concatenate(parts, axis=1).astype(BF16)

    x_ext = jnp.concatenate([xp_ref[...], x_ref[...]], axis=0)
    b_ext = jnp.concatenate([bp_ref[...], b_ref[...]], axis=0)
    x1 = x_ext + _dot(a_out, woa_ref[...]) + _dot(b_ext, wob_ref[...])

    h = x1 * lax.rsqrt(jnp.mean(x1 * x1, axis=-1, keepdims=True) + EPS) * g2_ref[...]
    h_ext = jnp.where(jnp.logical_and(ti == 0, row < MIX_HALO), 0.0, h).astype(BF16)

    chunk = 2 * LANES
    for c in range(D_FF // chunk):
        halves = []
        for off in (0, D_FF):
            sl = slice(off + c * chunk, off + (c + 1) * chunk)
            u = _dot(h_ext, wup_ref[:, sl])
            u1 = pltpu.roll(u, 1, 0)
            u2 = pltpu.roll(u, 2, 0)
            cv = cb_ref[:, sl] + u2[MIX_HALO:] * cw_ref[0:1, sl]
            cv = cv + u1[MIX_HALO:] * cw_ref[1:2, sl]
            cv = cv + u[MIX_HALO:] * cw_ref[2:3, sl]
            halves.append(cv)
        gate, val = halves
        act = gate * (1.0 / (1.0 + jnp.exp(-gate))) * val
        act_ref[:, c * chunk:(c + 1) * chunk] = act.astype(BF16)
    out_ref[...] = x1[MIX_HALO:] + _dot(act_ref[...], wdn_ref[...])


def _mixffn(x3, pool3, b_out, pw, ps, woa, wob, g2, wup, cw, cb, wdn):
    batch, seq, _ = x3.shape
    rows = FFN_ROWS
    tile = lambda w: pl.BlockSpec((None, rows, w), lambda b, i: (b, i, 0))
    back = lambda h, w: pl.BlockSpec((None, h, w), lambda b, i: (b, jnp.maximum(i * (rows // h) - 1, 0), 0))
    const = lambda a: pl.BlockSpec(a.shape, lambda b, i: (0,) * a.ndim, pipeline_mode=pl.Buffered(1))
    return pl.pallas_call(
        _mixffn_kernel,
        grid=(batch, seq // rows),
        in_specs=[tile(D_MODEL), back(MIX_HALO, D_MODEL), tile(POOL_WIDTH), back(POOL_BACK, POOL_WIDTH),
                  tile(POOL_WIDTH), back(MIX_HALO, POOL_WIDTH),
                  const(pw), const(ps), const(woa), const(wob), const(g2), const(wup), const(cw), const(cb),
                  const(wdn)],
        out_specs=tile(D_MODEL),
        out_shape=jax.ShapeDtypeStruct((batch, seq, D_MODEL), F32),
        scratch_shapes=[pltpu.VMEM((rows, D_FF), BF16)],
        compiler_params=pltpu.CompilerParams(dimension_semantics=("arbitrary", "arbitrary"),
                                             vmem_limit_bytes=VMEM_LIMIT_FFN_BYTES),
        name="mixffn",
    )(x3, x3, pool3, pool3, b_out, b_out, pw, ps, woa, wob, g2, wup, cw, cb, wdn)


def _rope_tables(seq, dim):
    half = dim // 2
    inv = jnp.exp(-jnp.log(jnp.float32(ROPE_THETA)) * jnp.arange(half, dtype=F32) / half)
    ang = jnp.arange(seq).astype(F32)[:, None] * inv[None, :]
    cos, sin = jnp.cos(ang), jnp.sin(ang)
    return jnp.concatenate([cos, cos], axis=1), jnp.concatenate([-sin, sin], axis=1)


def _layer(x, norm1_g, w_in, q_norm_g, k_norm_g, pool_w, pool_scale, w_out, norm2_g, w_up, conv_w, conv_b, w_down):
    batch, seq, _ = x.shape
    n = batch * seq
    v0 = MAIN_W - N_KV_HEADS * HEAD_DIM
    wmain = w_in[:, :v0].astype(BF16)
    wvt = w_in[:, v0:MAIN_W].T.astype(BF16)
    w_qi, w_ki, w_wi = jnp.split(w_in[:, MAIN_W:], [IDX_HEADS * IDX_DIM, (IDX_HEADS + 1) * IDX_DIM], axis=1)
    qg = jnp.concatenate([q_norm_g, q_norm_g])[None, :]
    kg = jnp.concatenate([k_norm_g, k_norm_g])[None, :]
    slots = LANES // IDX_DIM
    w_idx = jnp.concatenate([w_qi, w_wi, jnp.zeros((D_MODEL, LANES - IDX_HEADS), F32)] + [w_ki] * slots, axis=1)

    c64, s64 = _rope_tables(seq, HEAD_DIM)
    qcos, qsin = jnp.tile(c64, (1, LANES // HEAD_DIM)), jnp.tile(s64, (1, LANES // HEAD_DIM))
    c32, s32 = _rope_tables(seq, IDX_DIM)
    wscale = jnp.full((seq, IDX_HEADS), IDX_HEADS ** -0.5 * IDX_DIM ** -0.5, F32)
    gap = jnp.zeros((seq, LANES - IDX_HEADS), F32)
    icos = jnp.concatenate([jnp.tile(c32, (1, IDX_HEADS)), wscale, gap, jnp.tile(c32, (1, slots))], axis=1)
    isin = jnp.concatenate([jnp.tile(s32, (1, IDX_HEADS)), jnp.zeros((seq, LANES), F32),
                            jnp.tile(s32, (1, slots))], axis=1)

    x2 = x.reshape(n, D_MODEL)
    pool, q, k, vt, idx, kcat = _inproj(x2, norm1_g[None, :], wmain, w_idx.astype(BF16), wvt, qg, kg,
                                        qcos, qsin, icos, isin, seq)
    b_out = _dsa(q, idx, kcat, k, vt, batch, seq)
    return _mixffn(x, pool.reshape(batch, seq, POOL_WIDTH), b_out, pool_w.astype(BF16), pool_scale[None, :],
                   w_out[:POOL_WIDTH].astype(BF16), w_out[POOL_WIDTH:].astype(BF16), norm2_g[None, :],
                   w_up.astype(BF16), conv_w, conv_b[None, :], w_down.astype(BF16))


def kernel(x, norm1_g, w_in, q_norm_g, k_norm_g, pool_w, pool_scale, w_out, norm2_g, w_up, conv_w, conv_b, w_down):
    for l in range(norm1_g.shape[0]):
        x = _layer(x, norm1_g[l], w_in[l], q_norm_g[l], k_norm_g[l], pool_w[l], pool_scale[l], w_out[l],
                   norm2_g[l], w_up[l], conv_w[l], conv_b[l], w_down[l])
    return x
```

```python
import functools
import math

import jax
import jax.numpy as jnp
from jax import lax
from jax.experimental import pallas as pl
from jax.experimental.pallas import tpu as pltpu

F32 = jnp.float32
BF16 = jnp.bfloat16

D_MODEL = 1024
POOL_WIDTH = 512
POOL_WINDOWS = (2, 4, 8, 16)
POOL_GROUP = 128
HEAD_DIM = 64
N_Q_HEADS = 8
N_KV_HEADS = 2
Q_PER_KV = 4
IDX_HEADS = 8
IDX_DIM = 32
TOPK_MAX = 256
ROPE_THETA = 10000.0
D_FF = 2816
CONV_WIDTH = 3
EPS = 1e-6
NEG = -1e30

LANES = 128
SUBLANES = 8
QPAD = N_Q_HEADS * LANES
IDXW = 3 * LANES
IDX_PROJ_W = IDXW + LANES
MAIN_W = POOL_WIDTH + (N_Q_HEADS + 2 * N_KV_HEADS) * HEAD_DIM

INPROJ_ROWS = 512
DSA_ROWS = 256
FFN_ROWS = 512
V7X_VMEM_BYTES = 64 * 1024 * 1024
VMEM_LIMIT_BYTES = 3 * V7X_VMEM_BYTES // 4
VMEM_LIMIT_FFN_BYTES = 7 * V7X_VMEM_BYTES // 8
MIX_HALO = 16
POOL_BACK = 32

Q_SCALE = HEAD_DIM ** -0.5 * math.log2(math.e)

NT_DIMS = (((1,), (1,)), ((), ()))


def _dot(a, b):
    return jnp.dot(a, b, preferred_element_type=F32)


def _dot_nt(a, b):
    return lax.dot_general(a, b, NT_DIMS, preferred_element_type=F32)


def _swap_halves(xc, half):
    lane = lax.broadcasted_iota(jnp.int32, xc.shape, 1)
    first = (lane % (2 * half)) < half
    return jnp.where(first, pltpu.roll(xc, LANES - half, 1), pltpu.roll(xc, half, 1))


def _inproj_kernel(x_ref, g1_ref, wmain_ref, widx_ref, wvt_ref, qg_ref, kg_ref,
                   qcos_ref, qsin_ref, icos_ref, isin_ref,
                   pool_ref, q_ref, k_ref, vt_ref, idx_ref, kcat_ref):
    x = x_ref[...]
    ms = jnp.mean(x * x, axis=-1, keepdims=True)
    h = x * lax.rsqrt(ms + EPS) * g1_ref[...]
    h_hi = h.astype(BF16)

    qcos = qcos_ref[...]
    qsin = qsin_ref[...]
    lane = lax.broadcasted_iota(jnp.int32, (x.shape[0], LANES), 1)
    first = lane < HEAD_DIM

    def norm_rope(xc, gain):
        sq = xc * xc
        s0 = jnp.sum(jnp.where(first, sq, 0.0), axis=-1, keepdims=True)
        s1 = jnp.sum(jnp.where(first, 0.0, sq), axis=-1, keepdims=True)
        ss = jnp.where(first, s0, s1) * (1.0 / HEAD_DIM)
        xn = xc * lax.rsqrt(ss + EPS) * gain
        return xn * qcos + _swap_halves(xn, HEAD_DIM // 2) * qsin

    n_qcols = N_Q_HEADS * HEAD_DIM // LANES
    qk = _dot(h_hi, wmain_ref[:, POOL_WIDTH:])
    idxp = _dot(h_hi, widx_ref[...])

    for c in range(n_qcols):
        qr = norm_rope(qk[:, c * LANES:(c + 1) * LANES], qg_ref[...]) * Q_SCALE
        qs = pltpu.roll(qr, HEAD_DIM, 1)
        if (2 * c) // Q_PER_KV == 0:
            pair = (jnp.where(first, qr, 0.0), jnp.where(first, qs, 0.0))
        else:
            pair = (jnp.where(first, 0.0, qs), jnp.where(first, 0.0, qr))
        for i, qh in enumerate(pair):
            q_ref[:, (2 * c + i) * LANES:(2 * c + i + 1) * LANES] = qh.astype(BF16)
    k_ref[...] = norm_rope(qk[:, n_qcols * LANES:], kg_ref[...]).astype(BF16)

    for c in range(IDX_PROJ_W // LANES):
        sl = slice(c * LANES, (c + 1) * LANES)
        ic = idxp[:, sl]
        roped = ic * icos_ref[:, sl] + _swap_halves(ic, IDX_DIM // 2) * isin_ref[:, sl]
        if c < IDXW // LANES:
            idx_ref[:, sl] = roped
        else:
            kcat_ref[...] = roped.astype(BF16)

    pool_ref[...] = _dot(h_hi, wmain_ref[:, :POOL_WIDTH])
    vt = _dot_nt(wvt_ref[...], h_hi).astype(BF16)
    for i in range(vt_ref.shape[0]):
        vt_ref[i] = vt[:, i * DSA_ROWS:(i + 1) * DSA_ROWS]


def _inproj(x2, g1, wmain, widx, wvt, qg, kg, qcos, qsin, icos, isin, seq):
    n = x2.shape[0]
    rows = INPROJ_ROWS
    tper = seq // rows
    vblk = rows // DSA_ROWS
    row = lambda w: pl.BlockSpec((rows, w), lambda i: (i, 0))
    const = lambda a: pl.BlockSpec(a.shape, lambda i: (0, 0))
    table = lambda w: pl.BlockSpec((rows, w), lambda i: (i % tper, 0))
    return pl.pallas_call(
        _inproj_kernel,
        grid=(n // rows,),
        in_specs=[row(D_MODEL), const(g1), const(wmain), const(widx), const(wvt), const(qg), const(kg),
                  table(LANES), table(LANES), table(IDX_PROJ_W), table(IDX_PROJ_W)],
        out_specs=[row(POOL_WIDTH), row(QPAD), row(LANES),
                   pl.BlockSpec((vblk, LANES, DSA_ROWS), lambda i: (i, 0, 0)), row(IDXW), row(LANES)],
        out_shape=[jax.ShapeDtypeStruct((n, POOL_WIDTH), F32),
                   jax.ShapeDtypeStruct((n, QPAD), BF16),
                   jax.ShapeDtypeStruct((n, LANES), BF16),
                   jax.ShapeDtypeStruct((n // DSA_ROWS, LANES, DSA_ROWS), BF16),
                   jax.ShapeDtypeStruct((n, IDXW), F32),
                   jax.ShapeDtypeStruct((n, LANES), BF16)],
        compiler_params=pltpu.CompilerParams(dimension_semantics=("arbitrary",),
                                             vmem_limit_bytes=VMEM_LIMIT_BYTES),
        name="inproj",
    )(x2, g1, wmain, widx, wvt, qg, kg, qcos, qsin, icos, isin)


def _key_to_float(key):
    bits = key ^ ((key >> 31) & jnp.int32(0x7FFFFFFF))
    return lax.bitcast_convert_type(bits, F32)


def _fold_groups(x, op, ways=4):
    n = x.shape[0] // SUBLANES
    accs = [x[g * SUBLANES:(g + 1) * SUBLANES, :] for g in range(ways)]
    for g in range(ways, n):
        accs[g % ways] = op(accs[g % ways], x[g * SUBLANES:(g + 1) * SUBLANES, :])
    while len(accs) > 1:
        accs = [op(accs[i], accs[i + 1]) for i in range(0, len(accs), 2)]
    return accs[0]


def _fold_sublanes(x, op):
    for shift in (4, 2, 1):
        x = op(x, pltpu.roll(x, shift, 0))
    return x


def _chunk_loop(n, body, init, unroll=4):
    def run(start, width, carry):
        for w in range(width):
            carry = body(start + w, carry)
        return carry

    carry = lax.fori_loop(0, n // unroll, lambda i, c: run(unroll * i, unroll, c), init)
    done = (n // unroll) * unroll
    width = unroll // 2
    while width:
        carry = lax.cond((n & width) != 0, functools.partial(run, done, width), lambda c: c, carry)
        done = done + (n & width)
        width //= 2
    return carry


def _dsa_kernel(q_ref, idxq_ref, kcat_ref, k_ref, vt_ref, out_ref, sc_ref, qcat_ref, acc_ref, lg_ref,
                *, seq, topk):
    rows = DSA_ROWS
    ck = DSA_ROWS
    ngrp = ck // SUBLANES
    qb = pl.program_id(1)
    nch = qb + 1
    lane = lax.broadcasted_iota(jnp.int32, (rows, LANES), 1)
    qidx = qb * rows + lax.broadcasted_iota(jnp.int32, (ck, rows), 1)
    krow = lax.broadcasted_iota(jnp.int32, (ck, rows), 0)

    slots = LANES // IDX_DIM
    for c in range(IDX_HEADS // slots):
        col = idxq_ref[:, c * LANES:(c + 1) * LANES]
        for r in range(slots):
            live = jnp.logical_and(lane >= r * IDX_DIM, lane < (r + 1) * IDX_DIM)
            qcat_ref[c * slots + r] = jnp.where(live, col, 0.0).astype(BF16)
    wt = idxq_ref[:, 2 * LANES:].T[0:IDX_HEADS, :]

    def score_body(j, nonneg):
        kc = kcat_ref[pl.ds(pl.multiple_of(j * ck, ck), ck), :]
        acc = jnp.zeros((ck, rows), F32)
        for hd in range(IDX_HEADS):
            y = _dot_nt(kc, qcat_ref[hd])
            acc = acc + wt[hd:hd + 1, :] * jnp.maximum(y, 0.0)
        sc = jnp.where(j * ck + krow <= qidx, acc, NEG)
        sc_ref[j] = sc
        return nonneg + _fold_groups(jnp.where(sc >= 0.0, 1.0, 0.0), jnp.add)

    nonneg = _chunk_loop(nch, score_body, jnp.zeros((SUBLANES, rows), F32))

    n_virtual = (seq - nch * ck).astype(F32)

    def count(pred, thr):
        def body(j, accs):
            accs = list(accs)
            for g in range(ngrp):
                hit = pred(sc_ref[j, g * SUBLANES:(g + 1) * SUBLANES, :], thr)
                accs[g % len(accs)] = accs[g % len(accs)] + jnp.where(hit, 1.0, 0.0)
            return tuple(accs)
        zero = jnp.zeros((SUBLANES, rows), F32)
        accs = _chunk_loop(nch, body, (zero, zero, zero, zero))
        tot = _fold_sublanes((accs[0] + accs[1]) + (accs[2] + accs[3]), jnp.add)
        return tot + jnp.where(pred(jnp.float32(NEG), thr), n_virtual, 0.0)

    ge = lambda a, b: a >= b
    gt = lambda a, b: a > b
    kf = jnp.float32(topk)

    nonneg = _fold_sublanes(nonneg, jnp.add)
    positive = nonneg >= kf
    key = jnp.where(positive, jnp.int32(0), jnp.int32(-2 ** 31))
    n_ge = jnp.where(positive, nonneg, jnp.float32(seq))

    def bit_body(i, carry):
        key, n_ge = carry
        cand = key | jnp.left_shift(jnp.int32(1), 30 - i)
        cnt = count(ge, _key_to_float(cand))
        return jnp.where(cnt >= kf, cand, key), jnp.where(cnt >= kf, cnt, n_ge)

    key, n_ge = lax.fori_loop(0, 31, bit_body, (key, n_ge))
    thr8 = _key_to_float(key)
    thr = thr8[0:1, :]

    def logits(j, ms):
        kc = k_ref[pl.ds(pl.multiple_of(j * ck, ck), ck), :]
        out = []
        for hd in range(N_Q_HEADS):
            s = _dot_nt(kc, q_ref[:, hd * LANES:(hd + 1) * LANES]) + sc_ref[j]
            lg_ref[hd, j] = s
            out.append(jnp.maximum(ms[hd], _fold_groups(s, jnp.maximum)))
        return tuple(out)

    ms0 = tuple(jnp.full((SUBLANES, rows), -jnp.inf, F32) for _ in range(N_Q_HEADS))

    def select_exact():
        def body(j, ms):
            sc_ref[j] = jnp.where(sc_ref[j] >= thr, 0.0, NEG)
            return logits(j, ms)
        return _chunk_loop(nch, body, ms0)

    def select_ties():
        need = (kf - count(gt, thr8))[0:1, :]
        ki = lax.broadcasted_iota(jnp.int32, (ck, ck), 0)
        kj = lax.broadcasted_iota(jnp.int32, (ck, ck), 1)
        lower = jnp.where(kj < ki, 1.0, 0.0).astype(BF16)
        ones8 = jnp.ones((SUBLANES, ck), BF16)

        def body(j, carry):
            run, ms = carry
            sc = sc_ref[j]
            eq = sc == thr
            eqb = jnp.where(eq, 1.0, 0.0).astype(BF16)
            prior = run[0:1, :] + _dot(lower, eqb)
            tie = jnp.where(prior < need, 0.0, NEG)
            bias = jnp.where(sc > thr, 0.0, jnp.where(eq, tie, NEG))
            sc_ref[j] = jnp.where(j * ck + krow <= qidx, bias, NEG)
            return run + _dot(ones8, eqb), logits(j, ms)
        return _chunk_loop(nch, body, (jnp.zeros((SUBLANES, rows), F32), ms0))[1]

    all_exact = jnp.min(jnp.where(n_ge == kf, 1.0, 0.0)) > 0.5
    ms = lax.cond(all_exact, select_exact, select_ties)
    ms = [_fold_sublanes(m, jnp.maximum)[0:1, :] for m in ms]

    acc_ref[...] = jnp.zeros(acc_ref.shape, F32)

    def pv_body(j, ls):
        vt = vt_ref[j]
        out = []
        for hd in range(N_Q_HEADS):
            p = jnp.exp2(lg_ref[hd, j] - ms[hd])
            out.append(ls[hd] + _fold_groups(p, jnp.add))
            r0 = HEAD_DIM * (hd // Q_PER_KV)
            acc_ref[hd] += _dot(vt[r0:r0 + HEAD_DIM, :], p.astype(BF16))
        return tuple(out)

    ls = _chunk_loop(nch, pv_body, tuple(jnp.zeros((SUBLANES, rows), F32) for _ in range(N_Q_HEADS)))

    outs = []
    for hd in range(N_Q_HEADS):
        l = _fold_sublanes(ls[hd], jnp.add)
        outs.append(acc_ref[hd] / l[0:1, :])
    out_ref[...] = jnp.concatenate(outs, axis=0).T.astype(BF16)


def _dsa(q, idx, kcat, k, vt, batch, seq):
    rows = DSA_ROWS
    nq = seq // rows
    topk = min(TOPK_MAX, seq // 4)
    assert topk <= rows and seq % rows == 0
    q3 = q.reshape(batch, seq, QPAD)
    idx3 = idx.reshape(batch, seq, IDXW)
    vt = vt.reshape(batch, nq, LANES, rows)
    full = pl.BlockSpec((None, seq, LANES), lambda b, i: (b, 0, 0))
    kern = functools.partial(_dsa_kernel, seq=seq, topk=topk)
    return pl.pallas_call(
        kern,
        grid=(batch, nq),
        in_specs=[pl.BlockSpec((None, rows, QPAD), lambda b, i: (b, i, 0)),
                  pl.BlockSpec((None, rows, IDXW), lambda b, i: (b, i, 0)),
                  full, full,
                  pl.BlockSpec((None, nq, LANES, rows), lambda b, i: (b, 0, 0, 0))],
        out_specs=pl.BlockSpec((None, rows, N_Q_HEADS * HEAD_DIM), lambda b, i: (b, i, 0)),
        out_shape=jax.ShapeDtypeStruct((batch, seq, N_Q_HEADS * HEAD_DIM), BF16),
        scratch_shapes=[pltpu.VMEM((nq, rows, rows), F32),
                        pltpu.VMEM((IDX_HEADS, rows, LANES), BF16),
                        pltpu.VMEM((N_Q_HEADS, HEAD_DIM, rows), F32),
                        pltpu.VMEM((N_Q_HEADS, nq, rows, rows), F32)],
        compiler_params=pltpu.CompilerParams(dimension_semantics=("arbitrary", "arbitrary"),
                                             vmem_limit_bytes=VMEM_LIMIT_BYTES),
        name="dsa",
    )(q3, idx3, kcat.reshape(batch, seq, LANES), k.reshape(batch, seq, LANES), vt)


def _mixffn_kernel(x_ref, xp_ref, pool_ref, pp_ref, b_ref, bp_ref, pw_ref, ps_ref, woa_ref, wob_ref,
                   g2_ref, wup_ref, cw_ref, cb_ref, wdn_ref, out_ref, act_ref):
    rows = FFN_ROWS
    ext = MIX_HALO + rows
    ti = pl.program_id(1)
    row = lax.broadcasted_iota(jnp.int32, (ext, 1), 0)
    t = ti * rows - MIX_HALO + row

    pool_ext = jnp.concatenate([jnp.where(ti == 0, 0.0, pp_ref[...]), pool_ref[...]], axis=0)
    lead = POOL_BACK - MIX_HALO
    parts = []
    for g, w in enumerate(POOL_WINDOWS):
        sl = slice(g * POOL_GROUP, (g + 1) * POOL_GROUP)
        a = pool_ext[:, sl]
        span = 1
        while span < w:
            a = a + pltpu.roll(a, span, 0)
            span *= 2
        cnt = jnp.clip(t + 1, 1, w).astype(F32)
        p = a[lead:] / cnt - pool_ext[lead:, sl]
        parts.append(_dot(p.astype(BF16), pw_ref[g]) * ps_ref[:, sl])
    a_out = jnp.concatenate(parts, axis=1).astype(BF16)

    x_ext = jnp.concatenate([xp_ref[...], x_ref[...]], axis=0)
    b_ext = jnp.concatenate([bp_ref[...], b_ref[...]], axis=0)
    x1 = x_ext + _dot(a_out, woa_ref[...]) + _dot(b_ext, wob_ref[...])

    h = x1 * lax.rsqrt(jnp.mean(x1 * x1, axis=-1, keepdims=True) + EPS) * g2_ref[...]
    h_ext = jnp.where(jnp.logical_and(ti == 0, row < MIX_HALO), 0.0, h).astype(BF16)

    chunk = 2 * LANES
    for c in range(D_FF // chunk):
        halves = []
        for off in (0, D_FF):
            sl = slice(off + c * chunk, off + (c + 1) * chunk)
            u = _dot(h_ext, wup_ref[:, sl])
            u1 = pltpu.roll(u, 1, 0)
            u2 = pltpu.roll(u, 2, 0)
            cv = cb_ref[:, sl] + u2[MIX_HALO:] * cw_ref[0:1, sl]
            cv = cv + u1[MIX_HALO:] * cw_ref[1:2, sl]
            cv = cv + u[MIX_HALO:] * cw_ref[2:3, sl]
            halves.append(cv)
        gate, val = halves
        act = gate * (1.0 / (1.0 + jnp.exp2(gate * -math.log2(math.e)))) * val
        act_ref[:, c * chunk:(c + 1) * chunk] = act.astype(BF16)
    out_ref[...] = x1[MIX_HALO:] + _dot(act_ref[...], wdn_ref[...])


def _mixffn(x3, pool3, b_out, pw, ps, woa, wob, g2, wup, cw, cb, wdn):
    batch, seq, _ = x3.shape
    rows = FFN_ROWS
    tile = lambda w: pl.BlockSpec((None, rows, w), lambda b, i: (b, i, 0))
    back = lambda h, w: pl.BlockSpec((None, h, w), lambda b, i: (b, jnp.maximum(i * (rows // h) - 1, 0), 0))
    const = lambda a: pl.BlockSpec(a.shape, lambda b, i: (0,) * a.ndim, pipeline_mode=pl.Buffered(1))
    return pl.pallas_call(
        _mixffn_kernel,
        grid=(batch, seq // rows),
        in_specs=[tile(D_MODEL), back(MIX_HALO, D_MODEL), tile(POOL_WIDTH), back(POOL_BACK, POOL_WIDTH),
                  tile(POOL_WIDTH), back(MIX_HALO, POOL_WIDTH),
                  const(pw), const(ps), const(woa), const(wob), const(g2), const(wup), const(cw), const(cb),
                  const(wdn)],
        out_specs=tile(D_MODEL),
        out_shape=jax.ShapeDtypeStruct((batch, seq, D_MODEL), F32),
        scratch_shapes=[pltpu.VMEM((rows, D_FF), BF16)],
        compiler_params=pltpu.CompilerParams(dimension_semantics=("arbitrary", "arbitrary"),
                                             vmem_limit_bytes=VMEM_LIMIT_FFN_BYTES),
        name="mixffn",
    )(x3, x3, pool3, pool3, b_out, b_out, pw, ps, woa, wob, g2, wup, cw, cb, wdn)


def _rope_tables(seq, dim):
    half = dim // 2
    inv = jnp.exp(-jnp.log(jnp.float32(ROPE_THETA)) * jnp.arange(half, dtype=F32) / half)
    ang = jnp.arange(seq).astype(F32)[:, None] * inv[None, :]
    cos, sin = jnp.cos(ang), jnp.sin(ang)
    return jnp.concatenate([cos, cos], axis=1), jnp.concatenate([-sin, sin], axis=1)


def _layer(x, norm1_g, w_in, q_norm_g, k_norm_g, pool_w, pool_scale, w_out, norm2_g, w_up, conv_w, conv_b, w_down):
    batch, seq, _ = x.shape
    n = batch * seq
    v0 = MAIN_W - N_KV_HEADS * HEAD_DIM
    wmain = w_in[:, :v0].astype(BF16)
    wvt = w_in[:, v0:MAIN_W].T.astype(BF16)
    w_qi, w_ki, w_wi = jnp.split(w_in[:, MAIN_W:], [IDX_HEADS * IDX_DIM, (IDX_HEADS + 1) * IDX_DIM], axis=1)
    qg = jnp.concatenate([q_norm_g, q_norm_g])[None, :]
    kg = jnp.concatenate([k_norm_g, k_norm_g])[None, :]
    slots = LANES // IDX_DIM
    w_idx = jnp.concatenate([w_qi, w_wi, jnp.zeros((D_MODEL, LANES - IDX_HEADS), F32)] + [w_ki] * slots, axis=1)

    c64, s64 = _rope_tables(seq, HEAD_DIM)
    qcos, qsin = jnp.tile(c64, (1, LANES // HEAD_DIM)), jnp.tile(s64, (1, LANES // HEAD_DIM))
    c32, s32 = _rope_tables(seq, IDX_DIM)
    wscale = jnp.full((seq, IDX_HEADS), IDX_HEADS ** -0.5 * IDX_DIM ** -0.5, F32)
    gap = jnp.zeros((seq, LANES - IDX_HEADS), F32)
    icos = jnp.concatenate([jnp.tile(c32, (1, IDX_HEADS)), wscale, gap, jnp.tile(c32, (1, slots))], axis=1)
    isin = jnp.concatenate([jnp.tile(s32, (1, IDX_HEADS)), jnp.zeros((seq, LANES), F32),
                            jnp.tile(s32, (1, slots))], axis=1)

    x2 = x.reshape(n, D_MODEL)
    pool, q, k, vt, idx, kcat = _inproj(x2, norm1_g[None, :], wmain, w_idx.astype(BF16), wvt, qg, kg,
                                        qcos, qsin, icos, isin, seq)
    b_out = _dsa(q, idx, kcat, k, vt, batch, seq)
    return _mixffn(x, pool.reshape(batch, seq, POOL_WIDTH), b_out, pool_w.astype(BF16), pool_scale[None, :],
                   w_out[:POOL_WIDTH].astype(BF16), w_out[POOL_WIDTH:].astype(BF16), norm2_g[None, :],
                   w_up.astype(BF16), conv_w, conv_b[None, :], w_down.astype(BF16))


def kernel(x, norm1_g, w_in, q_norm_g, k_norm_g, pool_w, pool_scale, w_out, norm2_g, w_up, conv_w, conv_b, w_down):
    for l in range(norm1_g.shape[0]):
        x = _layer(x, norm1_g[l], w_in[l], q_norm_g[l], k_norm_g[l], pool_w[l], pool_scale[l], w_out[l],
                   norm2_g[l], w_up[l], conv_w[l], conv_b[l], w_down[l])
    return x
```
